```python
import math
import jax, jax.numpy as jnp
from jax import lax
import numpy as np

D_MODEL = 2048
BATCH = 2
SEQ = 4096
DEPTH = 1

RET_HEADS = 8
RET_DK = 128
RET_DV = 256
RET_QK = RET_HEADS * RET_DK
RET_V = RET_HEADS * RET_DV
RET_CHUNK = 128
ROPE_BASE = 10000.0
SSD_INNER = 2 * D_MODEL
SSD_HEAD_DIM = 64
SSD_HEADS = SSD_INNER // SSD_HEAD_DIM
SSD_GROUPS = 8
SSD_STATE = 128
SSD_CONV = 4
SSD_CHUNK = 128
SSD_CONV_DIM = SSD_INNER + 2 * SSD_GROUPS * SSD_STATE
MEM_LEN = 256
XA_HEADS = 4
XA_HEAD_DIM = D_MODEL // XA_HEADS
D_FF = 4 * D_MODEL
EPS = 1e-6

IN_SIZES = [RET_QK, RET_QK, RET_V, RET_V, SSD_INNER, SSD_CONV_DIM, SSD_HEADS, D_MODEL, D_MODEL]
IN_WIDTH = sum(IN_SIZES)
IN_SPLIT_IDX = [int(i) for i in np.cumsum(IN_SIZES)[:-1]]

kernel_name = 'hybrid_retention_ssd_gated_block'


def rms_norm(x, w):
    xf = x.astype(jnp.float32)
    y = xf * lax.rsqrt(jnp.mean(xf * xf, axis=-1, keepdims=True) + EPS)
    return (y * w.astype(jnp.float32)).astype(x.dtype)


def rotary(x, pos):
    d = x.shape[-1]
    half = d // 2
    inv = jnp.exp(-math.log(ROPE_BASE) * jnp.arange(half, dtype=jnp.float32) / half)
    ang = pos[:, None] * inv[None, :]
    cos = jnp.cos(ang)[None, :, None, :]
    sin = jnp.sin(ang)[None, :, None, :]
    x1, x2 = x[..., :half], x[..., half:]
    return jnp.concatenate([x1 * cos - x2 * sin, x1 * sin + x2 * cos], axis=-1)


def retention(q, k, v):
    bsz, seq, nh, dk = q.shape
    dv = v.shape[-1]
    c = RET_CHUNK
    nc = seq // c
    log_g = jnp.log(1.0 - jnp.exp2(-5.0 - jnp.arange(nh, dtype=jnp.float32)))
    q = q.reshape(bsz, nc, c, nh, dk)
    k = k.reshape(bsz, nc, c, nh, dk)
    v = v.reshape(bsz, nc, c, nh, dv)
    idx = jnp.arange(c, dtype=jnp.float32)
    rel = idx[:, None] - idx[None, :]
    decay_in = jnp.where(rel[None] >= 0, jnp.exp(jnp.maximum(rel, 0.0)[None] * log_g[:, None, None]), 0.0)
    scores = jnp.einsum('bnihd,bnjhd->bnhij', q, k) * decay_in
    inner = jnp.einsum('bnhij,bnjhe->bnihe', scores, v)
    zeta = jnp.exp((c - 1 - idx)[None, :] * log_g[:, None])
    chunk_kv = jnp.einsum('bnjhd,hj,bnjhe->bnhde', k, zeta, v)
    chunk_decay = jnp.exp(c * log_g)[:, None, None]

    def step(state, kv):
        return chunk_decay * state + kv, state

    init = jnp.zeros((bsz, nh, dk, dv), chunk_kv.dtype)
    _, prev = lax.scan(step, init, jnp.moveaxis(chunk_kv, 1, 0))
    prev = jnp.moveaxis(prev, 0, 1)
    xi = jnp.exp((idx + 1.0)[None, :] * log_g[:, None])
    cross = jnp.einsum('bnihd,bnhde,hi->bnihe', q, prev, xi)
    return (inner + cross).reshape(bsz, seq, nh, dv)


def head_group_norm(y, w):
    bsz, seq = y.shape[:2]
    yf = y.astype(jnp.float32)
    mu = jnp.mean(yf, axis=-1, keepdims=True)
    var = jnp.mean(jnp.square(yf - mu), axis=-1, keepdims=True)
    yn = ((yf - mu) * lax.rsqrt(var + EPS)).reshape(bsz, seq, -1)
    return yn * w.astype(jnp.float32)


def causal_conv(x, w, b):
    ch = x.shape[-1]
    y = lax.conv_general_dilated(x, w[:, None, :].astype(x.dtype), window_strides=(1,),
                                 padding=[(SSD_CONV - 1, 0)],
                                 dimension_numbers=('NWC', 'WIO', 'NWC'),
                                 feature_group_count=ch)
    return y + b


def ssd_scan(x, dt, a, bm, cm):
    bsz, seq, nh, hp = x.shape
    g, n = bm.shape[2], bm.shape[3]
    r = nh // g
    qc = SSD_CHUNK
    nc = seq // qc
    x = x.reshape(bsz, nc, qc, g, r, hp)
    dt = dt.reshape(bsz, nc, qc, g, r)
    bm = bm.reshape(bsz, nc, qc, g, n)
    cm = cm.reshape(bsz, nc, qc, g, n)
    da = dt * a.reshape(g, r)
    a_cs = jnp.cumsum(da, axis=2)
    xdt = x * dt[..., None]
    mask = jnp.tril(jnp.ones((qc, qc), dtype=bool))[None, None, :, :, None, None]
    seg = a_cs[:, :, :, None] - a_cs[:, :, None, :]
    lmat = jnp.exp(jnp.where(mask, seg, -jnp.inf))
    cb = jnp.einsum('bclgn,bcsgn->bclsg', cm, bm)
    y_diag = jnp.einsum('bclsg,bclsgr,bcsgrp->bclgrp', cb, lmat, xdt)
    decay_states = jnp.exp(a_cs[:, :, -1:] - a_cs)
    states = jnp.einsum('bclgn,bclgr,bclgrp->bcgrpn', bm, decay_states, xdt)
    chunk_decay = jnp.exp(a_cs[:, :, -1])[..., None, None]

    def step(s, inp):
        st, dec = inp
        return dec * s + st, s

    init = jnp.zeros((bsz, g, r, hp, n), states.dtype)
    _, prev = lax.scan(step, init, (jnp.moveaxis(states, 1, 0), jnp.moveaxis(chunk_decay, 1, 0)))
    prev = jnp.moveaxis(prev, 0, 1)
    y_off = jnp.einsum('bclgn,bcgrpn,bclgr->bclgrp', cm, prev, jnp.exp(a_cs))
    return (y_diag + y_off).reshape(bsz, seq, nh, hp)


def gated_rms_norm(y, z, w):
    bsz, seq, d = y.shape
    gy = (y * jax.nn.silu(z)).astype(jnp.float32).reshape(bsz, seq, SSD_GROUPS, d // SSD_GROUPS)
    gy = gy * lax.rsqrt(jnp.mean(gy * gy, axis=-1, keepdims=True) + EPS)
    return gy.reshape(bsz, seq, d) * w.astype(jnp.float32)


def setup_inputs(seed: int = 0) -> dict:
    key = jax.random.key(seed)
    ks = iter(jax.random.split(key, 32))
    f32 = jnp.float32

    def nrm(shape, fan_in):
        return jax.random.normal(next(ks), shape, f32) * (fan_in ** -0.5)

    def gain(shape):
        return 1.0 + 0.02 * jax.random.normal(next(ks), shape, f32)

    x = jax.random.normal(next(ks), (BATCH, SEQ, D_MODEL), f32)
    mem = jax.random.normal(next(ks), (BATCH, MEM_LEN, D_MODEL), f32)
    dt0 = jnp.exp(jax.random.uniform(next(ks), (DEPTH, SSD_HEADS), f32, math.log(1e-3), math.log(1e-1)))
    dt_bias = dt0 + jnp.log(-jnp.expm1(-dt0))
    a_log = jnp.log(jax.random.uniform(next(ks), (DEPTH, SSD_HEADS), f32, 1.0, 16.0))
    return {
        'x': x,
        'mem': mem,
        'norm_mix_w': gain((DEPTH, D_MODEL)),
        'w_in': nrm((DEPTH, D_MODEL, IN_WIDTH), D_MODEL),
        'conv_w': nrm((DEPTH, SSD_CONV, SSD_CONV_DIM), SSD_CONV),
        'conv_b': 0.02 * jax.random.normal(next(ks), (DEPTH, SSD_CONV_DIM), f32),
        'dt_bias': dt_bias,
        'a_log': a_log,
        'd_skip': gain((DEPTH, SSD_HEADS)),
        'ret_norm_w': gain((DEPTH, RET_V)),
        'ssd_norm_w': gain((DEPTH, SSD_INNER)),
        'w_ret_out': nrm((DEPTH, RET_V, D_MODEL), RET_V),
        'w_ssd_out': nrm((DEPTH, SSD_INNER, D_MODEL), SSD_INNER),
        'w_mix_out': nrm((DEPTH, D_MODEL, D_MODEL), D_MODEL),
        'norm_xa_w': gain((DEPTH, D_MODEL)),
        'mem_norm_w': gain((DEPTH, D_MODEL)),
        'w_xq': nrm((DEPTH, D_MODEL, D_MODEL), D_MODEL),
        'w_xkv': nrm((DEPTH, D_MODEL, 2 * D_MODEL), D_MODEL),
        'w_xo': nrm((DEPTH, D_MODEL, D_MODEL), D_MODEL),
        'norm_ff_w': gain((DEPTH, D_MODEL)),
        'w_ff1': nrm((DEPTH, D_MODEL, D_FF), D_MODEL),
        'w_ff2': nrm((DEPTH, D_FF, D_MODEL), D_FF),
        'final_norm_w': gain((D_MODEL,)),
    }


def reference(x, mem, norm_mix_w, w_in, conv_w, conv_b, dt_bias, a_log, d_skip, ret_norm_w, ssd_norm_w,
              w_ret_out, w_ssd_out, w_mix_out, norm_xa_w, mem_norm_w, w_xq, w_xkv, w_xo,
              norm_ff_w, w_ff1, w_ff2, final_norm_w):
    bsz, seq, _ = x.shape
    mem_len = mem.shape[1]
    pos = jnp.arange(seq, dtype=jnp.float32)
    h = x
    for l in range(DEPTH):
        xn = rms_norm(h, norm_mix_w[l])
        q, k, v, g_ret, z, xbc, dt_raw, gate_a, gate_b = jnp.split(xn @ w_in[l], IN_SPLIT_IDX, axis=-1)
        q = rotary(q.reshape(bsz, seq, RET_HEADS, RET_DK), pos) * (RET_DK ** -0.5)
        k = rotary(k.reshape(bsz, seq, RET_HEADS, RET_DK), pos)
        v = v.reshape(bsz, seq, RET_HEADS, RET_DV)
        y_ret = head_group_norm(retention(q, k, v), ret_norm_w[l]) * jax.nn.silu(g_ret)
        br_ret = y_ret @ w_ret_out[l]
        xbc = jax.nn.silu(causal_conv(xbc, conv_w[l], conv_b[l]))
        xs, bm, cm = jnp.split(xbc, [SSD_INNER, SSD_INNER + SSD_GROUPS * SSD_STATE], axis=-1)
        dt = jax.nn.softplus(dt_raw.astype(jnp.float32) + dt_bias[l].astype(jnp.float32))
        a = -jnp.exp(a_log[l].astype(jnp.float32))
        xs = xs.reshape(bsz, seq, SSD_HEADS, SSD_HEAD_DIM)
        y_ssd = ssd_scan(xs, dt, a,
                         bm.reshape(bsz, seq, SSD_GROUPS, SSD_STATE),
                         cm.reshape(bsz, seq, SSD_GROUPS, SSD_STATE))
        y_ssd = y_ssd + d_skip[l][:, None] * xs
        y_ssd = gated_rms_norm(y_ssd.reshape(bsz, seq, SSD_INNER), z, ssd_norm_w[l])
        br_ssd = y_ssd @ w_ssd_out[l]
        merged = jax.nn.sigmoid(gate_a) * br_ret + jax.nn.sigmoid(gate_b) * br_ssd
        h = h + (merged @ w_mix_out[l]).astype(h.dtype)
        xn = rms_norm(h, norm_xa_w[l])
        mn = rms_norm(mem, mem_norm_w[l])
        qx = (xn @ w_xq[l]).reshape(bsz, seq, XA_HEADS, XA_HEAD_DIM)
        kx, vx = jnp.split(mn @ w_xkv[l], 2, axis=-1)
        kx = kx.reshape(bsz, mem_len, XA_HEADS, XA_HEAD_DIM)
        vx = vx.reshape(bsz, mem_len, XA_HEADS, XA_HEAD_DIM)
        s = jnp.einsum('bqhd,bmhd->bhqm', qx, kx).astype(jnp.float32) * (XA_HEAD_DIM ** -0.5)
        p = jax.nn.softmax(s, axis=-1).astype(vx.dtype)
        o = jnp.einsum('bhqm,bmhd->bqhd', p, vx).reshape(bsz, seq, D_MODEL)
        h = h + (o @ w_xo[l]).astype(h.dtype)
        xn = rms_norm(h, norm_ff_w[l])
        h = h + (jnp.square(jax.nn.relu(xn @ w_ff1[l])) @ w_ff2[l]).astype(h.dtype)
    return rms_norm(h, final_norm_w)
```

```python
import functools
import math

import numpy as np
import jax
import jax.numpy as jnp
from jax import lax
from jax.experimental import pallas as pl
from jax.experimental.pallas import tpu as pltpu

F32 = jnp.float32
BF16 = jnp.bfloat16
EPS = 1e-6

RET_HEADS = 8
RET_DK = 128
RET_DV = 256
CHUNK = 128
ROPE_BASE = 10000.0
SSD_HEAD_DIM = 64
SSD_GROUPS = 8
SSD_STATE = 128
SSD_CONV = 4
XA_HEADS = 4

V7X_VMEM_BYTES = 64 * 1024 * 1024
VMEM_LIMIT = V7X_VMEM_BYTES - 8 * 1024 * 1024
LANES = 128

_LOG_G = np.log(1.0 - np.exp2(-5.0 - np.arange(RET_HEADS, dtype=np.float32))).astype(np.float32)
_RET_CHUNK_DECAY = np.exp(np.float32(CHUNK) * _LOG_G).astype(np.float32)


def _params(*sem):
    return pltpu.CompilerParams(dimension_semantics=sem, vmem_limit_bytes=VMEM_LIMIT)


def _rms(x, w):
    ms = jnp.mean(x * x, axis=-1, keepdims=True)
    return x * lax.rsqrt(ms + EPS) * w


def _silu(x):
    return x * jax.nn.sigmoid(x)


def _dot(a, b):
    return jnp.dot(a, b, preferred_element_type=F32)


def _dot_nt(a, b):
    return lax.dot_general(a, b, (((1,), (1,)), ((), ())), preferred_element_type=F32)


def _norm_matmul_body(x_ref, nw_ref, w_ref, o_ref, xn_ref):
    @pl.when(pl.program_id(1) == 0)
    def _():
        xn_ref[...] = _rms(x_ref[...], nw_ref[...]).astype(BF16)

    o_ref[...] = _dot(xn_ref[...], w_ref[...].astype(BF16)).astype(o_ref.dtype)


def _norm_matmul(x, nw, w, col_off, n_cols, out_dtype, tm, tn, name):
    m, k = x.shape
    off = col_off // tn
    assert col_off % tn == 0 and n_cols % tn == 0 and m % tm == 0
    return pl.pallas_call(
        _norm_matmul_body,
        grid=(m // tm, n_cols // tn),
        in_specs=[
            pl.BlockSpec((tm, k), lambda i, j: (i, 0)),
            pl.BlockSpec((1, k), lambda i, j: (0, 0)),
            pl.BlockSpec((k, tn), lambda i, j: (0, j + off)),
        ],
        out_specs=pl.BlockSpec((tm, tn), lambda i, j: (i, j)),
        out_shape=jax.ShapeDtypeStruct((m, n_cols), out_dtype),
        scratch_shapes=[pltpu.VMEM((tm, k), BF16)],
        compiler_params=_params("parallel", "arbitrary"),
        name=name,
    )(x, nw, w)


def _matmul_res_body(a_ref, w_ref, r_ref, o_ref):
    o_ref[...] = r_ref[...] + _dot(a_ref[...], w_ref[...].astype(BF16))


def _matmul_res(a, w, res, tm, tn, name):
    m, k = a.shape
    n = w.shape[1]
    return pl.pallas_call(
        _matmul_res_body,
        grid=(m // tm, n // tn),
        in_specs=[
            pl.BlockSpec((tm, k), lambda i, j: (i, 0)),
            pl.BlockSpec((k, tn), lambda i, j: (0, j)),
            pl.BlockSpec((tm, tn), lambda i, j: (i, j)),
        ],
        out_specs=pl.BlockSpec((tm, tn), lambda i, j: (i, j)),
        out_shape=jax.ShapeDtypeStruct((m, n), F32),
        compiler_params=_params("parallel", "arbitrary"),
        name=name,
    )(a, w, res)


def _retention_body(q_ref, k_ref, v_ref, g_ref, cos_ref, sin_ref, nw_ref, o_ref, state_ref):
    @pl.when(pl.program_id(1) == 0)
    def _():
        state_ref[...] = jnp.zeros_like(state_ref)

    c = CHUNK
    cos = cos_ref[...]
    sin = sin_ref[...]
    row = lax.broadcasted_iota(jnp.int32, (c, c), 0)
    col = lax.broadcasted_iota(jnp.int32, (c, c), 1)
    rel = (row - col).astype(F32)
    rowf = row.astype(F32)
    scale = RET_DK ** -0.5
    for h in range(RET_HEADS):
        lg = float(_LOG_G[h])
        q = q_ref[:, h * RET_DK:(h + 1) * RET_DK].astype(F32)
        k = k_ref[:, h * RET_DK:(h + 1) * RET_DK].astype(F32)
        v = v_ref[:, h * RET_DV:(h + 1) * RET_DV]
        qr = (q * cos + pltpu.roll(q, RET_DK // 2, 1) * sin) * scale
        kr = k * cos + pltpu.roll(k, RET_DK // 2, 1) * sin
        decay = jnp.where(rel >= 0, jnp.exp(jnp.maximum(rel, 0.0) * lg), 0.0)
        s = _dot_nt(qr.astype(BF16), kr.astype(BF16)) * decay
        xi = jnp.exp((rowf + 1.0) * lg)
        zeta = jnp.exp((c - 1.0 - rowf) * lg)
        st = state_ref[h]
        lhs = jnp.concatenate([s.astype(BF16), (qr * xi).astype(BF16)], axis=1)
        rhs = jnp.concatenate([v, st.astype(BF16)], axis=0)
        y = _dot(lhs, rhs)
        kz_t = (kr * zeta).T.astype(BF16)
        state_ref[h] = float(_RET_CHUNK_DECAY[h]) * st + _dot(kz_t, v)
        mu = jnp.mean(y, axis=-1, keepdims=True)
        yc = y - mu
        var = jnp.mean(yc * yc, axis=-1, keepdims=True)
        yn = yc * lax.rsqrt(var + EPS)
        g = g_ref[:, h * RET_DV:(h + 1) * RET_DV].astype(F32)
        o_ref[:, h * RET_DV:(h + 1) * RET_DV] = (
            yn * nw_ref[:, h * RET_DV:(h + 1) * RET_DV] * _silu(g)).astype(BF16)


def _retention(p_ret, cos2, sin2, nw, bsz, seq):
    nc = seq // CHUNK
    qk = RET_HEADS * RET_DK
    vd = RET_HEADS * RET_DV
    assert vd == 2 * qk
    return pl.pallas_call(
        _retention_body,
        grid=(bsz, nc),
        in_specs=[
            pl.BlockSpec((CHUNK, qk), lambda b, c: (b * nc + c, 0)),
            pl.BlockSpec((CHUNK, qk), lambda b, c: (b * nc + c, 1)),
            pl.BlockSpec((CHUNK, vd), lambda b, c: (b * nc + c, 1)),
            pl.BlockSpec((CHUNK, vd), lambda b, c: (b * nc + c, 2)),
            pl.BlockSpec((CHUNK, RET_DK), lambda b, c: (c, 0)),
            pl.BlockSpec((CHUNK, RET_DK), lambda b, c: (c, 0)),
            pl.BlockSpec((1, vd), lambda b, c: (0, 0)),
        ],
        out_specs=pl.BlockSpec((CHUNK, vd), lambda b, c: (b * nc + c, 0)),
        out_shape=jax.ShapeDtypeStruct((bsz * seq, vd), BF16),
        scratch_shapes=[pltpu.VMEM((RET_HEADS, RET_DK, RET_DV), F32)],
        compiler_params=_params("parallel", "arbitrary"),
        name="retention",
    )(p_ret, p_ret, p_ret, p_ret, cos2, sin2, nw)


def _ssd_body(z_ref, xbc_ref, dt_ref, cw_ref, cb_ref, dtb_ref, alog_ref, dsk_ref, nw_ref,
              o_ref, ext_ref, xc_ref, prev_ref, *, inner, heads_per_group):
    c = CHUNK
    pad = 8
    conv_dim = xbc_ref.shape[1]
    gw = heads_per_group * SSD_HEAD_DIM
    first = pl.program_id(1) == 0

    @pl.when(first)
    def _():
        ext_ref[0:pad, :] = jnp.zeros((pad, conv_dim), F32)
        prev_ref[...] = jnp.zeros_like(prev_ref)

    @pl.when(jnp.logical_not(first))
    def _():
        ext_ref[0:pad, :] = ext_ref[c:c + pad, :]

    ext_ref[pad:pad + c, :] = xbc_ref[...].astype(F32)

    cblk = 512
    for j in range(conv_dim // cblk):
        sl = slice(j * cblk, (j + 1) * cblk)
        acc = cb_ref[:, sl]
        for t in range(SSD_CONV):
            lo = pad - (SSD_CONV - 1) + t
            acc = acc + cw_ref[t:t + 1, sl] * ext_ref[lo:lo + c, sl]
        xc_ref[:, sl] = _silu(acc)

    dtr = dt_ref[...] + dtb_ref[...]
    dt = jnp.maximum(dtr, 0.0) + jnp.log1p(jnp.exp(-jnp.abs(dtr)))
    a = -jnp.exp(alog_ref[...])
    da = dt * a
    row = lax.broadcasted_iota(jnp.int32, (c, c), 0)
    col = lax.broadcasted_iota(jnp.int32, (c, c), 1)
    tri = row >= col
    a_cs = jnp.dot(tri.astype(F32), da, precision=lax.Precision.HIGHEST,
                   preferred_element_type=F32)
    a_last = a_cs[c - 1:c, :]
    w1 = jnp.exp(a_last - a_cs) * dt
    a_cs_t = a_cs.T
    dt_t = dt.T
    w1_t = w1.T
    lane_lo = lax.broadcasted_iota(jnp.int32, (c, LANES), 1) < SSD_HEAD_DIM
    neg_inf = jnp.float32(-jnp.inf)

    for g in range(SSD_GROUPS):
        b_g = xc_ref[:, inner + g * SSD_STATE: inner + (g + 1) * SSD_STATE]
        c_g = xc_ref[:, inner + (SSD_GROUPS + g) * SSD_STATE: inner + (SSD_GROUPS + g + 1) * SSD_STATE]
        cb = _dot_nt(c_g.astype(BF16), b_g.astype(BF16))
        b_gt = b_g.T
        y_pairs = []
        for pr in range(heads_per_group // 2):
            h0 = g * heads_per_group + 2 * pr
            psl = slice(h0 * SSD_HEAD_DIM, (h0 + 2) * SSD_HEAD_DIM)
            xs_b = xc_ref[:, psl].astype(BF16)
            prev = prev_ref[:, psl]
            rhs = jnp.concatenate([xs_b, prev.astype(BF16)], axis=0)
            ys, sts, cds = [], [], []
            for h in (h0, h0 + 1):
                colb = jnp.broadcast_to(a_cs[:, h:h + 1], (c, c))
                rowb = jnp.broadcast_to(a_cs_t[h:h + 1, :], (c, c))
                lmat = jnp.exp(jnp.where(tri, colb - rowb, neg_inf))
                m = cb * lmat * jnp.broadcast_to(dt_t[h:h + 1, :], (c, c))
                e = jnp.exp(colb)
                lhs = jnp.concatenate([m.astype(BF16), (e * c_g).astype(BF16)], axis=1)
                ys.append(_dot(lhs, rhs))
                bw = (b_gt * jnp.broadcast_to(w1_t[h:h + 1, :], (c, c))).astype(BF16)
                sts.append(_dot(bw, xs_b))
                cds.append(e[c - 1:c, :])
            y_pairs.append(jnp.where(lane_lo, ys[0], ys[1]))
            cd = jnp.where(lane_lo[0:1, :], cds[0], cds[1])
            prev_ref[:, psl] = cd * prev + jnp.where(lane_lo, sts[0], sts[1])
        gsl = slice(g * gw, (g + 1) * gw)
        y = jnp.concatenate(y_pairs, axis=1) + dsk_ref[:, gsl] * xc_ref[:, gsl]
        gy = y * _silu(z_ref[:, gsl].astype(F32))
        ms = jnp.mean(gy * gy, axis=-1, keepdims=True)
        o_ref[:, gsl] = (gy * lax.rsqrt(ms + EPS) * nw_ref[:, gsl]).astype(BF16)


def _ssd(p_z, p_xbc, dt_raw, conv_w, conv_b, dtb, alog, dskip, nw, bsz, seq):
    nc = seq // CHUNK
    inner = p_z.shape[1]
    conv_dim = p_xbc.shape[1]
    n_heads = inner // SSD_HEAD_DIM
    heads_per_group = n_heads // SSD_GROUPS
    assert conv_dim == inner + 2 * SSD_GROUPS * SSD_STATE and SSD_STATE == CHUNK
    assert n_heads <= LANES and heads_per_group % 2 == 0 and 2 * SSD_HEAD_DIM == LANES
    body = functools.partial(_ssd_body, inner=inner, heads_per_group=heads_per_group)
    row = lambda b, c: (b * nc + c, 0)
    fixed = lambda b, c: (0, 0)
    return pl.pallas_call(
        body,
        grid=(bsz, nc),
        in_specs=[
            pl.BlockSpec((CHUNK, inner), row),
            pl.BlockSpec((CHUNK, conv_dim), row),
            pl.BlockSpec((CHUNK, LANES), row),
            pl.BlockSpec((SSD_CONV, conv_dim), fixed),
            pl.BlockSpec((1, conv_dim), fixed),
            pl.BlockSpec((1, LANES), fixed),
            pl.BlockSpec((1, LANES), fixed),
            pl.BlockSpec((1, inner), fixed),
            pl.BlockSpec((1, inner), fixed),
        ],
        out_specs=pl.BlockSpec((CHUNK, inner), row),
        out_shape=jax.ShapeDtypeStruct((bsz * seq, inner), BF16),
        scratch_shapes=[
            pltpu.VMEM((CHUNK + 8, conv_dim), F32),
            pltpu.VMEM((CHUNK, conv_dim), F32),
            pltpu.VMEM((SSD_STATE, inner), F32),
        ],
        compiler_params=_params("parallel", "arbitrary"),
        name="ssd",
    )(p_z, p_xbc, dt_raw, conv_w, conv_b, dtb, alog, dskip, nw)


def _merge_body(yr_ref, ys_ref, wr_ref, ws_ref, ga_ref, gb_ref, o_ref):
    br = _dot(yr_ref[...], wr_ref[...].astype(BF16))
    bs = _dot(ys_ref[...], ws_ref[...].astype(BF16))
    ga = jax.nn.sigmoid(ga_ref[...].astype(F32))
    gb = jax.nn.sigmoid(gb_ref[...].astype(F32))
    o_ref[...] = (ga * br + gb * bs).astype(o_ref.dtype)


def _merge(y_ret, y_ssd, w_ret, w_ssd, gates, tm, tn):
    m, kr = y_ret.shape
    ks = y_ssd.shape[1]
    n = w_ret.shape[1]
    nb = n // tn
    return pl.pallas_call(
        _merge_body,
        grid=(m // tm, nb),
        in_specs=[
            pl.BlockSpec((tm, kr), lambda i, j: (i, 0)),
            pl.BlockSpec((tm, ks), lambda i, j: (i, 0)),
            pl.BlockSpec((kr, tn), lambda i, j: (0, j)),
            pl.BlockSpec((ks, tn), lambda i, j: (0, j)),
            pl.BlockSpec((tm, tn), lambda i, j: (i, j)),
            pl.BlockSpec((tm, tn), lambda i, j: (i, j + nb)),
        ],
        out_specs=pl.BlockSpec((tm, tn), lambda i, j: (i, j)),
        out_shape=jax.ShapeDtypeStruct((m, n), BF16),
        compiler_params=_params("parallel", "arbitrary"),
        name="merge",
    )(y_ret, y_ssd, w_ret, w_ssd, gates, gates)


def _xattn_body(q_ref, k_ref, v_ref, o_ref):
    hd = q_ref.shape[1] // XA_HEADS
    scale = hd ** -0.5
    for h in range(XA_HEADS):
        sl = slice(h * hd, (h + 1) * hd)
        s = _dot_nt(q_ref[:, sl], k_ref[:, sl]) * scale
        s = s - jnp.max(s, axis=-1, keepdims=True)
        p = jnp.exp(s)
        p = p / jnp.sum(p, axis=-1, keepdims=True)
        o_ref[:, sl] = _dot(p.astype(BF16), v_ref[:, sl]).astype(o_ref.dtype)


def _xattn(q, kv, bsz, seq, mem_len, tq):
    d = q.shape[1]
    nq = seq // tq
    return pl.pallas_call(
        _xattn_body,
        grid=(bsz, nq),
        in_specs=[
            pl.BlockSpec((tq, d), lambda b, i: (b * nq + i, 0)),
            pl.BlockSpec((mem_len, d), lambda b, i: (b, 0)),
            pl.BlockSpec((mem_len, d), lambda b, i: (b, 1)),
        ],
        out_specs=pl.BlockSpec((tq, d), lambda b, i: (b * nq + i, 0)),
        out_shape=jax.ShapeDtypeStruct((bsz * seq, d), BF16),
        compiler_params=_params("parallel", "arbitrary"),
        name="xattn",
    )(q, kv, kv)


def _mlp_body(h_ref, nw_ref, w1_ref, w2_ref, fw_ref, o_ref, xn_ref):
    f = pl.program_id(1)

    @pl.when(f == 0)
    def _():
        hh = h_ref[...]
        xn_ref[...] = _rms(hh, nw_ref[...]).astype(BF16)
        o_ref[...] = hh

    u = _dot(xn_ref[...], w1_ref[...].astype(BF16))
    u = jnp.square(jnp.maximum(u, 0.0))
    o_ref[...] += _dot(u.astype(BF16), w2_ref[...].astype(BF16))

    @pl.when(f == pl.num_programs(1) - 1)
    def _():
        o_ref[...] = _rms(o_ref[...], fw_ref[...])


def _mlp(h, nw, w1, w2, fw, tm, tf):
    m, d = h.shape
    dff = w1.shape[1]
    return pl.pallas_call(
        _mlp_body,
        grid=(m // tm, dff // tf),
        in_specs=[
            pl.BlockSpec((tm, d), lambda i, f: (i, 0)),
            pl.BlockSpec((1, d), lambda i, f: (0, 0)),
            pl.BlockSpec((d, tf), lambda i, f: (0, f)),
            pl.BlockSpec((tf, d), lambda i, f: (f, 0)),
            pl.BlockSpec((1, d), lambda i, f: (0, 0)),
        ],
        out_specs=pl.BlockSpec((tm, d), lambda i, f: (i, 0)),
        out_shape=jax.ShapeDtypeStruct((m, d), F32),
        scratch_shapes=[pltpu.VMEM((tm, d), BF16)],
        compiler_params=_params("parallel", "arbitrary"),
        name="mlp",
    )(h, nw, w1, w2, fw)


def _rope_tables(seq, dk):
    half = dk // 2
    pos = jnp.arange(seq, dtype=F32)
    inv = jnp.exp(-math.log(ROPE_BASE) * jnp.arange(half, dtype=F32) / half)
    ang = pos[:, None] * inv[None, :]
    cos, sin = jnp.cos(ang), jnp.sin(ang)
    return jnp.concatenate([cos, cos], axis=1), jnp.concatenate([-sin, sin], axis=1)


def kernel(x, mem, norm_mix_w, w_in, conv_w, conv_b, dt_bias, a_log, d_skip, ret_norm_w, ssd_norm_w, w_ret_out, w_ssd_out, w_mix_out, norm_xa_w, mem_norm_w, w_xq, w_xkv, w_xo, norm_ff_w, w_ff1, w_ff2, final_norm_w):
    bsz, seq, d = x.shape
    mem_len = mem.shape[1]
    depth = w_in.shape[0]
    n_tok = bsz * seq
    ret_qk = RET_HEADS * RET_DK
    ret_v = RET_HEADS * RET_DV
    inner = w_ssd_out.shape[1]
    conv_dim = conv_w.shape[2]
    n_heads = inner // SSD_HEAD_DIM
    ret_w = 2 * ret_qk + 2 * ret_v
    off_z = ret_w
    off_xbc = off_z + inner
    off_dt = off_xbc + conv_dim
    off_gate = off_dt + n_heads
    assert w_in.shape[2] == off_gate + 2 * d

    cos2, sin2 = _rope_tables(seq, RET_DK)
    row = lambda v: v.reshape(1, -1)
    pad_heads = lambda v: jnp.pad(v.reshape(1, -1), ((0, 0), (0, LANES - n_heads)))

    h = x.reshape(n_tok, d)
    for l in range(depth):
        w_l = w_in[l]
        nw = row(norm_mix_w[l])
        p_ret = _norm_matmul(h, nw, w_l, 0, ret_w, BF16, 1024, 1024, "in_proj_ret")
        p_z = _norm_matmul(h, nw, w_l, off_z, inner, BF16, 1024, 1024, "in_proj_z")
        p_xbc = _norm_matmul(h, nw, w_l, off_xbc, conv_dim, BF16, 1024, 1024, "in_proj_xbc")
        w_dt = jnp.pad(w_l[:, off_dt:off_gate], ((0, 0), (0, LANES - n_heads)))
        dt_raw = _norm_matmul(h, nw, w_dt, 0, LANES, F32, 1024, LANES, "in_proj_dt")
        gates = _norm_matmul(h, nw, w_l[:, off_gate:], 0, 2 * d, BF16, 1024, 1024, "in_proj_gates")
        y_ret = _retention(p_ret, cos2, sin2, row(ret_norm_w[l]), bsz, seq)
        y_ssd = _ssd(p_z, p_xbc, dt_raw, conv_w[l], row(conv_b[l]), pad_heads(dt_bias[l]),
                     pad_heads(a_log[l]), row(jnp.repeat(d_skip[l], SSD_HEAD_DIM)),
                     row(ssd_norm_w[l]), bsz, seq)
        merged = _merge(y_ret, y_ssd, w_ret_out[l], w_ssd_out[l], gates, 1024, 256)
        h = _matmul_res(merged, w_mix_out[l], h, 1024, 512, "mix_out")
        q = _norm_matmul(h, row(norm_xa_w[l]), w_xq[l], 0, d, BF16, 1024, 1024, "xa_q")
        kv = _norm_matmul(mem.reshape(bsz * mem_len, d), row(mem_norm_w[l]), w_xkv[l], 0, 2 * d,
                          BF16, bsz * mem_len, 1024, "xa_kv")
        o = _xattn(q, kv, bsz, seq, mem_len, 512)
        h = _matmul_res(o, w_xo[l], h, 1024, 512, "xa_out")
        assert depth == 1
        h = _mlp(h, row(norm_ff_w[l]), w_ff1[l], w_ff2[l], row(final_norm_w), 512, 512)
    return h.reshape(bsz, seq, d)
```

```python
import functools
import math

import numpy as np
import jax
import jax.numpy as jnp
from jax import lax
from jax.experimental import pallas as pl
from jax.experimental.pallas import tpu as pltpu

F32 = jnp.float32
BF16 = jnp.bfloat16
EPS = 1e-6

RET_HEADS = 8
RET_DK = 128
RET_DV = 256
CHUNK = 128
ROPE_BASE = 10000.0
SSD_HEAD_DIM = 64
SSD_GROUPS = 8
SSD_STATE = 128
SSD_CONV = 4
XA_HEADS = 4

V7X_VMEM_BYTES = 64 * 1024 * 1024
VMEM_LIMIT = V7X_VMEM_BYTES - 4 * 1024 * 1024
LANES = 128
ROW_SPLIT = 512
LOG2_E = math.log2(math.e)

_LOG_G = np.log(1.0 - np.exp2(-5.0 - np.arange(RET_HEADS, dtype=np.float32))).astype(np.float32)
_RET_CHUNK_DECAY = np.exp(np.float32(CHUNK) * _LOG_G).astype(np.float32)


def _params(*sem):
    return pltpu.CompilerParams(dimension_semantics=sem, vmem_limit_bytes=VMEM_LIMIT)


def _rms(x, w):
    ms = jnp.mean(x * x, axis=-1, keepdims=True)
    return x * lax.rsqrt(ms + EPS) * w


def _sigmoid(x):
    return 0.5 + 0.5 * jnp.tanh(0.5 * x)


def _silu(x):
    h = 0.5 * x
    return h + h * jnp.tanh(h)


def _dot(a, b):
    return jnp.dot(a, b, preferred_element_type=F32)


def _dot_nt(a, b):
    return lax.dot_general(a, b, (((1,), (1,)), ((), ())), preferred_element_type=F32)


def _row_blocks(n_rows):
    step = min(ROW_SPLIT, n_rows)
    return [slice(r, r + step) for r in range(0, n_rows, step)]


def _prenorm_body(x_ref, nw_ref, o_ref):
    o_ref[...] = _rms(x_ref[...], nw_ref[...]).astype(o_ref.dtype)


def _prenorm(x, nw, tm):
    m, k = x.shape
    return pl.pallas_call(
        _prenorm_body,
        grid=(m // tm,),
        in_specs=[pl.BlockSpec((tm, k), lambda i: (i, 0)), pl.BlockSpec((1, k), lambda i: (0, 0))],
        out_specs=pl.BlockSpec((tm, k), lambda i: (i, 0)),
        out_shape=jax.ShapeDtypeStruct((m, k), BF16),
        compiler_params=_params("parallel"),
        name="prenorm",
    )(x, nw)


def _matmul_body(a_ref, w_ref, o_ref):
    wb = w_ref[...].astype(BF16)
    for rows in _row_blocks(a_ref.shape[0]):
        o_ref[rows, :] = _dot(a_ref[rows, :], wb).astype(o_ref.dtype)


def _in_proj_main(a, w3, layer, n_cols, shift, tm, tn):
    m, k = a.shape
    nb = n_cols // tn
    assert n_cols % tn == 0 and m % tm == 0
    return pl.pallas_call(
        _matmul_body,
        grid=(m // tm, nb),
        in_specs=[
            pl.BlockSpec((tm, k), lambda i, j: (i, 0)),
            pl.BlockSpec((None, k, tn), lambda i, j: (layer, 0, j)),
        ],
        out_specs=pl.BlockSpec((tm, tn), lambda i, j: (i, (j + shift) % nb)),
        out_shape=jax.ShapeDtypeStruct((m, n_cols), BF16),
        compiler_params=_params("parallel", "arbitrary"),
        name="in_proj_main",
    )(a, w3)


def _in_proj_tail_body(a_ref, w_ref, g_ref, dt_ref):
    last = pl.program_id(1) == pl.num_programs(1) - 1
    tn = w_ref.shape[1]
    for rows in _row_blocks(a_ref.shape[0]):
        res = _dot(a_ref[rows, :], w_ref[...])
        g_ref[rows, :] = res.astype(g_ref.dtype)

        @pl.when(last)
        def _():
            dt_ref[rows, :] = res[:, tn - LANES:]


def _in_proj_tail(a, w_tail, tm, tn):
    m, k = a.shape
    n = w_tail.shape[1]
    assert n % tn == 0 and tn % LANES == 0
    return pl.pallas_call(
        _in_proj_tail_body,
        grid=(m // tm, n // tn),
        in_specs=[
            pl.BlockSpec((tm, k), lambda i, j: (i, 0)),
            pl.BlockSpec((k, tn), lambda i, j: (0, j)),
        ],
        out_specs=[
            pl.BlockSpec((tm, tn), lambda i, j: (i, j)),
            pl.BlockSpec((tm, LANES), lambda i, j: (i, 0)),
        ],
        out_shape=[jax.ShapeDtypeStruct((m, n), BF16), jax.ShapeDtypeStruct((m, LANES), F32)],
        compiler_params=_params("parallel", "arbitrary"),
        name="in_proj_tail",
    )(a, w_tail)


def _norm_matmul_body(x_ref, nw_ref, w_ref, o_ref, xn_ref):
    @pl.when(pl.program_id(1) == 0)
    def _():
        xn_ref[...] = _rms(x_ref[...], nw_ref[...]).astype(BF16)

    o_ref[...] = _dot(xn_ref[...], w_ref[...].astype(BF16)).astype(o_ref.dtype)


def _norm_matmul(x, nw, w, out_dtype, tm, tn, name):
    m, k = x.shape
    n = w.shape[1]
    return pl.pallas_call(
        _norm_matmul_body,
        grid=(m // tm, n // tn),
        in_specs=[
            pl.BlockSpec((tm, k), lambda i, j: (i, 0)),
            pl.BlockSpec((1, k), lambda i, j: (0, 0)),
            pl.BlockSpec((k, tn), lambda i, j: (0, j)),
        ],
        out_specs=pl.BlockSpec((tm, tn), lambda i, j: (i, j)),
        out_shape=jax.ShapeDtypeStruct((m, n), out_dtype),
        scratch_shapes=[pltpu.VMEM((tm, k), BF16)],
        compiler_params=_params("parallel", "arbitrary"),
        name=name,
    )(x, nw, w)


def _retention_body(q_ref, k_ref, v_ref, g_ref, cos_ref, sin_ref, nw_ref, o_ref, state_ref):
    @pl.when(pl.program_id(1) == 0)
    def _():
        state_ref[...] = jnp.zeros_like(state_ref)

    c = CHUNK
    cos = cos_ref[...]
    sin = sin_ref[...]
    row = lax.broadcasted_iota(jnp.int32, (c, c), 0)
    col = lax.broadcasted_iota(jnp.int32, (c, c), 1)
    rel = (row - col).astype(F32)
    rowf = row.astype(F32)
    scale = RET_DK ** -0.5
    for h in range(RET_HEADS):
        lg = float(_LOG_G[h])
        q = q_ref[:, h * RET_DK:(h + 1) * RET_DK].astype(F32)
        k = k_ref[:, h * RET_DK:(h + 1) * RET_DK].astype(F32)
        v = v_ref[:, h * RET_DV:(h + 1) * RET_DV]
        qr = (q * cos + pltpu.roll(q, RET_DK // 2, 1) * sin) * scale
        kr = k * cos + pltpu.roll(k, RET_DK // 2, 1) * sin
        decay = jnp.where(rel >= 0, jnp.exp(jnp.maximum(rel, 0.0) * lg), 0.0)
        s = _dot_nt(qr.astype(BF16), kr.astype(BF16)) * decay
        xi = jnp.exp((rowf + 1.0) * lg)
        zeta = jnp.exp((c - 1.0 - rowf) * lg)
        st = state_ref[h]
        lhs = jnp.concatenate([s.astype(BF16), (qr * xi).astype(BF16)], axis=1)
        rhs = jnp.concatenate([v, st.astype(BF16)], axis=0)
        y = _dot(lhs, rhs)
        kz_t = (kr * zeta).T.astype(BF16)
        state_ref[h] = float(_RET_CHUNK_DECAY[h]) * st + _dot(kz_t, v)
        mu = jnp.mean(y, axis=-1, keepdims=True)
        yc = y - mu
        var = jnp.mean(yc * yc, axis=-1, keepdims=True)
        yn = yc * lax.rsqrt(var + EPS)
        g = g_ref[:, h * RET_DV:(h + 1) * RET_DV].astype(F32)
        o_ref[:, h * RET_DV:(h + 1) * RET_DV] = (
            yn * nw_ref[:, h * RET_DV:(h + 1) * RET_DV] * _silu(g)).astype(BF16)


def _retention(proj, col_off, cos2, sin2, nw, bsz, seq):
    nc = seq // CHUNK
    qk = RET_HEADS * RET_DK
    vd = RET_HEADS * RET_DV
    assert vd == 2 * qk and col_off % vd == 0
    oq = col_off // qk
    ov = col_off // vd
    return pl.pallas_call(
        _retention_body,
        grid=(bsz, nc),
        in_specs=[
            pl.BlockSpec((CHUNK, qk), lambda b, c: (b * nc + c, oq)),
            pl.BlockSpec((CHUNK, qk), lambda b, c: (b * nc + c, oq + 1)),
            pl.BlockSpec((CHUNK, vd), lambda b, c: (b * nc + c, ov + 1)),
            pl.BlockSpec((CHUNK, vd), lambda b, c: (b * nc + c, ov + 2)),
            pl.BlockSpec((CHUNK, RET_DK), lambda b, c: (c, 0)),
            pl.BlockSpec((CHUNK, RET_DK), lambda b, c: (c, 0)),
            pl.BlockSpec((1, vd), lambda b, c: (0, 0)),
        ],
        out_specs=pl.BlockSpec((CHUNK, vd), lambda b, c: (b * nc + c, 0)),
        out_shape=jax.ShapeDtypeStruct((bsz * seq, vd), BF16),
        scratch_shapes=[pltpu.VMEM((RET_HEADS, RET_DK, RET_DV), F32)],
        compiler_params=_params("parallel", "arbitrary"),
        name="retention",
    )(proj, proj, proj, proj, cos2, sin2, nw)


def _ssd_body(z_ref, xbc_ref, dt_ref, cw_ref, cb_ref, dtb_ref, alog_ref, dsk_ref, nw_ref,
              o_ref, ext_ref, xc_ref, prev_ref, *, inner, heads_per_group):
    c = CHUNK
    conv_dim = xbc_ref.shape[1]
    gw = heads_per_group * SSD_HEAD_DIM
    first = pl.program_id(1) == 0

    @pl.when(first)
    def _():
        ext_ref[0:c, :] = jnp.zeros((c, conv_dim), BF16)
        prev_ref[...] = jnp.zeros_like(prev_ref)

    @pl.when(jnp.logical_not(first))
    def _():
        ext_ref[0:c, :] = ext_ref[c:2 * c, :]

    ext_ref[c:2 * c, :] = xbc_ref[...]

    srow = lax.broadcasted_iota(jnp.int32, (c, 2 * c), 0)
    scol = lax.broadcasted_iota(jnp.int32, (c, 2 * c), 1)
    shift_mat = jnp.concatenate(
        [jnp.where(scol - srow == c - (SSD_CONV - 1 - k), 1.0, 0.0).astype(BF16)
         for k in range(SSD_CONV - 1)], axis=0)
    cblk = 512
    for j in range(conv_dim // cblk):
        sl = slice(j * cblk, (j + 1) * cblk)
        shifted = _dot(shift_mat, ext_ref[:, sl])
        acc = cb_ref[:, sl] + cw_ref[SSD_CONV - 1:SSD_CONV, sl] * xbc_ref[:, sl].astype(F32)
        for k in range(SSD_CONV - 1):
            acc = acc + cw_ref[k:k + 1, sl] * shifted[k * c:(k + 1) * c, :]
        xc_ref[:, sl] = _silu(acc)

    dtr = dt_ref[...] + dtb_ref[...]
    dt = jnp.maximum(dtr, 0.0) + jnp.log1p(jnp.exp(-jnp.abs(dtr)))
    a = -jnp.exp(alog_ref[...])
    da = dt * a
    row = lax.broadcasted_iota(jnp.int32, (c, c), 0)
    col = lax.broadcasted_iota(jnp.int32, (c, c), 1)
    tri = row >= col
    a_cs = jnp.dot(tri.astype(F32), da, precision=lax.Precision.HIGHEST,
                   preferred_element_type=F32)
    a_cs = a_cs * LOG2_E
    a_last = a_cs[c - 1:c, :]
    w1 = jnp.exp2(a_last - a_cs) * dt
    a_cs_t = a_cs.T
    dt_t = dt.T
    w1_t = w1.T
    lane_lo = lax.broadcasted_iota(jnp.int32, (c, LANES), 1) < SSD_HEAD_DIM
    neg_inf = jnp.float32(-jnp.inf)

    for g in range(SSD_GROUPS):
        b_g = xc_ref[:, inner + g * SSD_STATE: inner + (g + 1) * SSD_STATE]
        c_g = xc_ref[:, inner + (SSD_GROUPS + g) * SSD_STATE: inner + (SSD_GROUPS + g + 1) * SSD_STATE]
        cb = _dot_nt(c_g.astype(BF16), b_g.astype(BF16))
        b_gt = b_g.T
        y_pairs = []
        for pr in range(heads_per_group // 2):
            h0 = g * heads_per_group + 2 * pr
            psl = slice(h0 * SSD_HEAD_DIM, (h0 + 2) * SSD_HEAD_DIM)
            xs_b = xc_ref[:, psl].astype(BF16)
            prev = prev_ref[:, psl]
            rhs = jnp.concatenate([xs_b, prev.astype(BF16)], axis=0)
            ys, sts, cds = [], [], []
            for h in (h0, h0 + 1):
                colb = jnp.broadcast_to(a_cs[:, h:h + 1], (c, c))
                rowb = jnp.broadcast_to(a_cs_t[h:h + 1, :], (c, c))
                lmat = jnp.exp2(jnp.where(tri, colb - rowb, neg_inf))
                m = cb * lmat * jnp.broadcast_to(dt_t[h:h + 1, :], (c, c))
                e = jnp.exp2(colb)
                lhs = jnp.concatenate([m.astype(BF16), (e * c_g).astype(BF16)], axis=1)
                ys.append(_dot(lhs, rhs))
                bw = (b_gt * jnp.broadcast_to(w1_t[h:h + 1, :], (c, c))).astype(BF16)
                sts.append(_dot(bw, xs_b))
                cds.append(e[c - 1:c, :])
            y_pairs.append(jnp.where(lane_lo, ys[0], ys[1]))
            cd = jnp.where(lane_lo[0:1, :], cds[0], cds[1])
            prev_ref[:, psl] = cd * prev + jnp.where(lane_lo, sts[0], sts[1])
        gsl = slice(g * gw, (g + 1) * gw)
        y = jnp.concatenate(y_pairs, axis=1) + dsk_ref[:, gsl] * xc_ref[:, gsl]
        gy = y * _silu(z_ref[:, gsl].astype(F32))
        ms = jnp.mean(gy * gy, axis=-1, keepdims=True)
        o_ref[:, gsl] = (gy * lax.rsqrt(ms + EPS) * nw_ref[:, gsl]).astype(BF16)


def _ssd(proj, z_off, xbc_off, inner, conv_dim, dt_raw, conv_w, conv_b, dtb, alog, dskip, nw, bsz, seq):
    nc = seq // CHUNK
    n_heads = inner // SSD_HEAD_DIM
    heads_per_group = n_heads // SSD_GROUPS
    assert conv_dim == inner + 2 * SSD_GROUPS * SSD_STATE and SSD_STATE == CHUNK
    assert n_heads <= LANES and heads_per_group % 2 == 0 and 2 * SSD_HEAD_DIM == LANES
    assert z_off % inner == 0 and xbc_off % conv_dim == 0
    zb = z_off // inner
    xb = xbc_off // conv_dim
    body = functools.partial(_ssd_body, inner=inner, heads_per_group=heads_per_group)
    row = lambda b, c: (b * nc + c, 0)
    fixed = lambda b, c: (0, 0)
    return pl.pallas_call(
        body,
        grid=(bsz, nc),
        in_specs=[
            pl.BlockSpec((CHUNK, inner), lambda b, c: (b * nc + c, zb)),
            pl.BlockSpec((CHUNK, conv_dim), lambda b, c: (b * nc + c, xb)),
            pl.BlockSpec((CHUNK, LANES), row),
            pl.BlockSpec((SSD_CONV, conv_dim), fixed),
            pl.BlockSpec((1, conv_dim), fixed),
            pl.BlockSpec((1, LANES), fixed),
            pl.BlockSpec((1, LANES), fixed),
            pl.BlockSpec((1, inner), fixed),
            pl.BlockSpec((1, inner), fixed),
        ],
        out_specs=pl.BlockSpec((CHUNK, inner), row),
        out_shape=jax.ShapeDtypeStruct((bsz * seq, inner), BF16),
        scratch_shapes=[
            pltpu.VMEM((2 * CHUNK, conv_dim), BF16),
            pltpu.VMEM((CHUNK, conv_dim), F32),
            pltpu.VMEM((SSD_STATE, inner), F32),
        ],
        compiler_params=_params("parallel", "arbitrary"),
        name="ssd",
    )(proj, proj, dt_raw, conv_w, conv_b, dtb, alog, dskip, nw)


def _merge_body(yr_ref, ys_ref, wr_ref, ws_ref, ga_ref, gb_ref, o_ref):
    for rows in _row_blocks(yr_ref.shape[0]):
        br = _dot(yr_ref[rows, :], wr_ref[...])
        bs = _dot(ys_ref[rows, :], ws_ref[...])
        ga = _sigmoid(ga_ref[rows, :].astype(F32))
        gb = _sigmoid(gb_ref[rows, :].astype(F32))
        o_ref[rows, :] = (ga * br + gb * bs).astype(o_ref.dtype)


def _merge(y_ret, y_ssd, w_ret, w_ssd, gates, tm, tn):
    m, kr = y_ret.shape
    ks = y_ssd.shape[1]
    n = w_ret.shape[1]
    nb = n // tn
    return pl.pallas_call(
        _merge_body,
        grid=(m // tm, nb),
        in_specs=[
            pl.BlockSpec((tm, kr), lambda i, j: (i, 0)),
            pl.BlockSpec((tm, ks), lambda i, j: (i, 0)),
            pl.BlockSpec((kr, tn), lambda i, j: (0, j)),
            pl.BlockSpec((ks, tn), lambda i, j: (0, j)),
            pl.BlockSpec((tm, tn), lambda i, j: (i, j)),
            pl.BlockSpec((tm, tn), lambda i, j: (i, j + nb)),
        ],
        out_specs=pl.BlockSpec((tm, tn), lambda i, j: (i, j)),
        out_shape=jax.ShapeDtypeStruct((m, n), BF16),
        compiler_params=_params("parallel", "arbitrary"),
        name="merge",
    )(y_ret, y_ssd, w_ret, w_ssd, gates, gates)


def _mix_xattn_body(m_ref, x_ref, k_ref, v_ref, wmix_ref, nw_ref, wq_ref, wo_ref, o_ref, xn_ref):
    d = x_ref.shape[1]
    hd = d // XA_HEADS
    scale = hd ** -0.5
    o_ref[...] = x_ref[...] + _dot(m_ref[...], wmix_ref[...])
    xn_ref[...] = _rms(o_ref[...], nw_ref[...]).astype(BF16)
    for h in range(XA_HEADS):
        sl = slice(h * hd, (h + 1) * hd)
        q = _dot(xn_ref[...], wq_ref[:, sl]).astype(BF16)
        s = _dot_nt(q, k_ref[:, sl]) * scale
        s = s - jnp.max(s, axis=-1, keepdims=True)
        p = jnp.exp(s)
        p = p / jnp.sum(p, axis=-1, keepdims=True)
        o_h = _dot(p.astype(BF16), v_ref[:, sl]).astype(BF16)
        o_ref[...] += _dot(o_h, wo_ref[sl, :])


def _mix_xattn(merged, x, kv, w_mix, nw, w_q, w_o, bsz, seq, mem_len, tq):
    d = x.shape[1]
    nq = seq // tq
    tile = lambda b, i: (b * nq + i, 0)
    fixed = lambda b, i: (0, 0)
    resident = pl.Buffered(1)
    return pl.pallas_call(
        _mix_xattn_body,
        grid=(bsz, nq),
        in_specs=[
            pl.BlockSpec((tq, d), tile),
            pl.BlockSpec((tq, d), tile),
            pl.BlockSpec((mem_len, d), lambda b, i: (b, 0)),
            pl.BlockSpec((mem_len, d), lambda b, i: (b, 1)),
            pl.BlockSpec((d, d), fixed, pipeline_mode=resident),
            pl.BlockSpec((1, d), fixed),
            pl.BlockSpec((d, d), fixed, pipeline_mode=resident),
            pl.BlockSpec((d, d), fixed, pipeline_mode=resident),
        ],
        out_specs=pl.BlockSpec((tq, d), tile),
        out_shape=jax.ShapeDtypeStruct((bsz * seq, d), F32),
        scratch_shapes=[pltpu.VMEM((tq, d), BF16)],
        compiler_params=_params("parallel", "arbitrary"),
        name="mix_xattn",
    )(merged, x, kv, kv, w_mix, nw, w_q, w_o)


def _mlp_body(h_ref, nw_ref, w1_ref, w2_ref, fw_ref, o_ref, xn_ref):
    f = pl.program_id(1)
    blocks = _row_blocks(h_ref.shape[0])

    @pl.when(f == 0)
    def _():
        for rows in blocks:
            hh = h_ref[rows, :]
            xn_ref[rows, :] = _rms(hh, nw_ref[...]).astype(BF16)
            o_ref[rows, :] = hh

    w1b = w1_ref[...].astype(BF16)
    w2b = w2_ref[...].astype(BF16)
    for rows in blocks:
        u = _dot(xn_ref[rows, :], w1b)
        u = jnp.square(jnp.maximum(u, 0.0)).astype(BF16)
        o_ref[rows, :] += _dot(u, w2b)

    @pl.when(f == pl.num_programs(1) - 1)
    def _():
        for rows in blocks:
            o_ref[rows, :] = _rms(o_ref[rows, :], fw_ref[...])


def _mlp(h, nw, w1, w2, fw, tm, tf):
    m, d = h.shape
    dff = w1.shape[1]
    once = pl.Buffered(1)
    return pl.pallas_call(
        _mlp_body,
        grid=(m // tm, dff // tf),
        in_specs=[
            pl.BlockSpec((tm, d), lambda i, f: (i, 0), pipeline_mode=once),
            pl.BlockSpec((1, d), lambda i, f: (0, 0)),
            pl.BlockSpec((d, tf), lambda i, f: (0, f)),
            pl.BlockSpec((tf, d), lambda i, f: (f, 0)),
            pl.BlockSpec((1, d), lambda i, f: (0, 0)),
        ],
        out_specs=pl.BlockSpec((tm, d), lambda i, f: (i, 0)),
        out_shape=jax.ShapeDtypeStruct((m, d), F32),
        scratch_shapes=[pltpu.VMEM((tm, d), BF16)],
        compiler_params=_params("parallel", "arbitrary"),
        name="mlp",
    )(h, nw, w1, w2, fw)


def _rope_tables(seq, dk):
    half = dk // 2
    pos = jnp.arange(seq, dtype=F32)
    inv = jnp.exp(-math.log(ROPE_BASE) * jnp.arange(half, dtype=F32) / half)
    ang = pos[:, None] * inv[None, :]
    cos, sin = jnp.cos(ang), jnp.sin(ang)
    return jnp.concatenate([cos, cos], axis=1), jnp.concatenate([-sin, sin], axis=1)


def kernel(x, mem, norm_mix_w, w_in, conv_w, conv_b, dt_bias, a_log, d_skip, ret_norm_w, ssd_norm_w, w_ret_out, w_ssd_out, w_mix_out, norm_xa_w, mem_norm_w, w_xq, w_xkv, w_xo, norm_ff_w, w_ff1, w_ff2, final_norm_w):
    bsz, seq, d = x.shape
    mem_len = mem.shape[1]
    depth = w_in.shape[0]
    n_tok = bsz * seq
    ret_qk = RET_HEADS * RET_DK
    ret_v = RET_HEADS * RET_DV
    inner = w_ssd_out.shape[1]
    conv_dim = conv_w.shape[2]
    n_heads = inner // SSD_HEAD_DIM
    ret_w = 2 * ret_qk + 2 * ret_v
    off_dt = ret_w + inner + conv_dim
    off_gate = off_dt + n_heads
    assert w_in.shape[2] == off_gate + 2 * d and depth == 1

    tn_main = 1024
    assert conv_dim % tn_main == 0 and ret_w % tn_main == 0 and inner % tn_main == 0
    shift = conv_dim // tn_main
    xbc_off, ret_off, z_off = 0, conv_dim, conv_dim + ret_w

    cos2, sin2 = _rope_tables(seq, RET_DK)
    row = lambda v: v.reshape(1, -1)
    pad_heads = lambda v: jnp.pad(v.reshape(1, -1), ((0, 0), (0, LANES - n_heads)))

    h = x.reshape(n_tok, d)
    for l in range(depth):
        xn = _prenorm(h, row(norm_mix_w[l]), 512)
        proj = _in_proj_main(xn, w_in, l, off_dt, shift, 2048, tn_main)
        w_tail = jnp.concatenate(
            [w_in[l, :, off_gate:], w_in[l, :, off_dt:off_gate],
             jnp.zeros((d, LANES - n_heads), F32)], axis=1).astype(BF16)
        gates, dt_raw = _in_proj_tail(xn, w_tail, 1024, w_tail.shape[1] // 3)
        y_ret = _retention(proj, ret_off, cos2, sin2, row(ret_norm_w[l]), bsz, seq)
        y_ssd = _ssd(proj, z_off, xbc_off, inner, conv_dim, dt_raw, conv_w[l], row(conv_b[l]),
                     pad_heads(dt_bias[l]), pad_heads(a_log[l]),
                     row(jnp.repeat(d_skip[l], SSD_HEAD_DIM)), row(ssd_norm_w[l]), bsz, seq)
        merged = _merge(y_ret, y_ssd, w_ret_out[l].astype(BF16), w_ssd_out[l].astype(BF16),
                        gates, 1024, 512)
        kv = _norm_matmul(mem.reshape(bsz * mem_len, d), row(mem_norm_w[l]), w_xkv[l],
                          BF16, bsz * mem_len, 1024, "xa_kv")
        h = _mix_xattn(merged, h, kv, w_mix_out[l].astype(BF16), row(norm_xa_w[l]),
                       w_xq[l].astype(BF16), w_xo[l].astype(BF16), bsz, seq, mem_len, 512)
        h = _mlp(h, row(norm_ff_w[l]), w_ff1[l], w_ff2[l], row(final_norm_w), 1024, 512)
    return h.reshape(bsz, seq, d)
```

```python
import functools
import math

import numpy as np
import jax
import jax.numpy as jnp
from jax import lax
from jax.experimental import pallas as pl
from jax.experimental.pallas import tpu as pltpu

F32 = jnp.float32
BF16 = jnp.bfloat16
EPS = 1e-6

RET_HEADS = 8
RET_DK = 128
RET_DV = 256
CHUNK = 128
ROPE_BASE = 10000.0
SSD_HEAD_DIM = 64
SSD_GROUPS = 8
SSD_STATE = 128
SSD_CONV = 4
XA_HEADS = 4

V7X_VMEM_BYTES = 64 * 1024 * 1024
VMEM_LIMIT = V7X_VMEM_BYTES - 4 * 1024 * 1024
LANES = 128
ROW_SPLIT = 512
LOG2_E = math.log2(math.e)

_LOG_G = np.log(1.0 - np.exp2(-5.0 - np.arange(RET_HEADS, dtype=np.float32))).astype(np.float32)
_RET_CHUNK_DECAY = np.exp(np.float32(CHUNK) * _LOG_G).astype(np.float32)


def _params(*sem):
    return pltpu.CompilerParams(dimension_semantics=sem, vmem_limit_bytes=VMEM_LIMIT)


def _rms(x, w):
    ms = jnp.mean(x * x, axis=-1, keepdims=True)
    return x * lax.rsqrt(ms + EPS) * w


def _sigmoid(x):
    return 0.5 + 0.5 * jnp.tanh(0.5 * x)


def _silu(x):
    h = 0.5 * x
    return h + h * jnp.tanh(h)


def _dot(a, b):
    return jnp.dot(a, b, preferred_element_type=F32)


def _dot_nt(a, b):
    return lax.dot_general(a, b, (((1,), (1,)), ((), ())), preferred_element_type=F32)


def _row_blocks(n_rows):
    step = min(ROW_SPLIT, n_rows)
    return [slice(r, r + step) for r in range(0, n_rows, step)]


def _prenorm_body(x_ref, nw_ref, o_ref):
    o_ref[...] = _rms(x_ref[...], nw_ref[...]).astype(o_ref.dtype)


def _prenorm(x, nw, tm):
    m, k = x.shape
    return pl.pallas_call(
        _prenorm_body,
        grid=(m // tm,),
        in_specs=[pl.BlockSpec((tm, k), lambda i: (i, 0)), pl.BlockSpec((1, k), lambda i: (0, 0))],
        out_specs=pl.BlockSpec((tm, k), lambda i: (i, 0)),
        out_shape=jax.ShapeDtypeStruct((m, k), BF16),
        compiler_params=_params("parallel"),
        name="prenorm",
    )(x, nw)


def _matmul_nt_body(a_ref, wt_ref, o_ref):
    wb = wt_ref[...].astype(BF16)
    for rows in _row_blocks(a_ref.shape[0]):
        o_ref[rows, :] = _dot_nt(a_ref[rows, :], wb).astype(o_ref.dtype)


def _in_proj_main(a, w3, layer, n_cols, shift, tm, tn):
    m, k = a.shape
    nb = n_cols // tn
    assert n_cols % tn == 0 and m % tm == 0
    return pl.pallas_call(
        _matmul_nt_body,
        grid=(m // tm, nb),
        in_specs=[
            pl.BlockSpec((tm, k), lambda i, j: (i, 0)),
            pl.BlockSpec((None, tn, k), lambda i, j: (layer, j, 0)),
        ],
        out_specs=pl.BlockSpec((tm, tn), lambda i, j: (i, (j + shift) % nb)),
        out_shape=jax.ShapeDtypeStruct((m, n_cols), BF16),
        compiler_params=_params("parallel", "arbitrary"),
        name="in_proj_main",
    )(a, w3)


def _in_proj_tail_body(a_ref, wt_ref, g_ref, dt_ref):
    last = pl.program_id(1) == pl.num_programs(1) - 1
    tn = wt_ref.shape[0]
    wb = wt_ref[...].astype(BF16)
    for rows in _row_blocks(a_ref.shape[0]):
        res = _dot_nt(a_ref[rows, :], wb)
        g_ref[rows, :] = res.astype(g_ref.dtype)

        @pl.when(last)
        def _():
            dt_ref[rows, :] = res[:, tn - LANES:]


def _in_proj_tail(a, wt_tail, tm, tn):
    m, k = a.shape
    n = wt_tail.shape[0]
    assert n % tn == 0 and tn % LANES == 0
    return pl.pallas_call(
        _in_proj_tail_body,
        grid=(m // tm, n // tn),
        in_specs=[
            pl.BlockSpec((tm, k), lambda i, j: (i, 0)),
            pl.BlockSpec((tn, k), lambda i, j: (j, 0)),
        ],
        out_specs=[
            pl.BlockSpec((tm, tn), lambda i, j: (i, j)),
            pl.BlockSpec((tm, LANES), lambda i, j: (i, 0)),
        ],
        out_shape=[jax.ShapeDtypeStruct((m, n), BF16), jax.ShapeDtypeStruct((m, LANES), F32)],
        compiler_params=_params("parallel", "arbitrary"),
        name="in_proj_tail",
    )(a, wt_tail)


def _norm_matmul_body(x_ref, nw_ref, w_ref, o_ref, xn_ref):
    @pl.when(pl.program_id(1) == 0)
    def _():
        xn_ref[...] = _rms(x_ref[...], nw_ref[...]).astype(BF16)

    o_ref[...] = _dot(xn_ref[...], w_ref[...].astype(BF16)).astype(o_ref.dtype)


def _norm_matmul(x, nw, w, out_dtype, tm, tn, name):
    m, k = x.shape
    n = w.shape[1]
    return pl.pallas_call(
        _norm_matmul_body,
        grid=(m // tm, n // tn),
        in_specs=[
            pl.BlockSpec((tm, k), lambda i, j: (i, 0)),
            pl.BlockSpec((1, k), lambda i, j: (0, 0)),
            pl.BlockSpec((k, tn), lambda i, j: (0, j)),
        ],
        out_specs=pl.BlockSpec((tm, tn), lambda i, j: (i, j)),
        out_shape=jax.ShapeDtypeStruct((m, n), out_dtype),
        scratch_shapes=[pltpu.VMEM((tm, k), BF16)],
        compiler_params=_params("parallel", "arbitrary"),
        name=name,
    )(x, nw, w)


def _retention_body(q_ref, k_ref, v_ref, g_ref, cos_ref, sin_ref, nw_ref, o_ref, state_ref):
    @pl.when(pl.program_id(1) == 0)
    def _():
        state_ref[...] = jnp.zeros_like(state_ref)

    c = CHUNK
    cos = cos_ref[...]
    sin = sin_ref[...]
    row = lax.broadcasted_iota(jnp.int32, (c, c), 0)
    col = lax.broadcasted_iota(jnp.int32, (c, c), 1)
    rel = (row - col).astype(F32)
    rowf = row.astype(F32)
    scale = RET_DK ** -0.5
    for h in range(RET_HEADS):
        lg = float(_LOG_G[h])
        q = q_ref[:, h * RET_DK:(h + 1) * RET_DK].astype(F32)
        k = k_ref[:, h * RET_DK:(h + 1) * RET_DK].astype(F32)
        v = v_ref[:, h * RET_DV:(h + 1) * RET_DV]
        qr = (q * cos + pltpu.roll(q, RET_DK // 2, 1) * sin) * scale
        kr = k * cos + pltpu.roll(k, RET_DK // 2, 1) * sin
        decay = jnp.where(rel >= 0, jnp.exp(jnp.maximum(rel, 0.0) * lg), 0.0)
        s = _dot_nt(qr.astype(BF16), kr.astype(BF16)) * decay
        xi = jnp.exp((rowf + 1.0) * lg)
        zeta = jnp.exp((c - 1.0 - rowf) * lg)
        st = state_ref[h]
        lhs = jnp.concatenate([s.astype(BF16), (qr * xi).astype(BF16)], axis=1)
        rhs = jnp.concatenate([v, st.astype(BF16)], axis=0)
        y = _dot(lhs, rhs)
        kz_t = (kr * zeta).T.astype(BF16)
        state_ref[h] = float(_RET_CHUNK_DECAY[h]) * st + _dot(kz_t, v)
        mu = jnp.mean(y, axis=-1, keepdims=True)
        yc = y - mu
        var = jnp.mean(yc * yc, axis=-1, keepdims=True)
        yn = yc * lax.rsqrt(var + EPS)
        g = g_ref[:, h * RET_DV:(h + 1) * RET_DV].astype(F32)
        o_ref[:, h * RET_DV:(h + 1) * RET_DV] = (
            yn * nw_ref[:, h * RET_DV:(h + 1) * RET_DV] * _silu(g)).astype(BF16)


def _retention(proj, col_off, cos2, sin2, nw, bsz, seq):
    nc = seq // CHUNK
    qk = RET_HEADS * RET_DK
    vd = RET_HEADS * RET_DV
    assert vd == 2 * qk and col_off % vd == 0
    oq = col_off // qk
    ov = col_off // vd
    return pl.pallas_call(
        _retention_body,
        grid=(bsz, nc),
        in_specs=[
            pl.BlockSpec((CHUNK, qk), lambda b, c: (b * nc + c, oq)),
            pl.BlockSpec((CHUNK, qk), lambda b, c: (b * nc + c, oq + 1)),
            pl.BlockSpec((CHUNK, vd), lambda b, c: (b * nc + c, ov + 1)),
            pl.BlockSpec((CHUNK, vd), lambda b, c: (b * nc + c, ov + 2)),
            pl.BlockSpec((CHUNK, RET_DK), lambda b, c: (c, 0)),
            pl.BlockSpec((CHUNK, RET_DK), lambda b, c: (c, 0)),
            pl.BlockSpec((1, vd), lambda b, c: (0, 0)),
        ],
        out_specs=pl.BlockSpec((CHUNK, vd), lambda b, c: (b * nc + c, 0)),
        out_shape=jax.ShapeDtypeStruct((bsz * seq, vd), BF16),
        scratch_shapes=[pltpu.VMEM((RET_HEADS, RET_DK, RET_DV), F32)],
        compiler_params=_params("parallel", "arbitrary"),
        name="retention",
    )(proj, proj, proj, proj, cos2, sin2, nw)


def _ssd_body(z_ref, xbc_ref, dt_ref, cw_ref, cb_ref, dtb_ref, alog_ref, dsk_ref, nw_ref,
              o_ref, ext_ref, xc_ref, prev_ref, *, inner, heads_per_group):
    c = CHUNK
    conv_dim = xbc_ref.shape[1]
    gw = heads_per_group * SSD_HEAD_DIM
    first = pl.program_id(1) == 0

    @pl.when(first)
    def _():
        ext_ref[0:c, :] = jnp.zeros((c, conv_dim), BF16)
        prev_ref[...] = jnp.zeros_like(prev_ref)

    @pl.when(jnp.logical_not(first))
    def _():
        ext_ref[0:c, :] = ext_ref[c:2 * c, :]

    ext_ref[c:2 * c, :] = xbc_ref[...]

    srow = lax.broadcasted_iota(jnp.int32, (c, 2 * c), 0)
    scol = lax.broadcasted_iota(jnp.int32, (c, 2 * c), 1)
    shift_mat = jnp.concatenate(
        [jnp.where(scol - srow == c - (SSD_CONV - 1 - k), 1.0, 0.0).astype(BF16)
         for k in range(SSD_CONV - 1)], axis=0)
    cblk = 512
    for j in range(conv_dim // cblk):
        sl = slice(j * cblk, (j + 1) * cblk)
        shifted = _dot(shift_mat, ext_ref[:, sl])
        acc = cb_ref[:, sl] + cw_ref[SSD_CONV - 1:SSD_CONV, sl] * xbc_ref[:, sl].astype(F32)
        for k in range(SSD_CONV - 1):
            acc = acc + cw_ref[k:k + 1, sl] * shifted[k * c:(k + 1) * c, :]
        xc_ref[:, sl] = _silu(acc)

    dtr = dt_ref[...] + dtb_ref[...]
    dt = jnp.maximum(dtr, 0.0) + jnp.log1p(jnp.exp(-jnp.abs(dtr)))
    a = -jnp.exp(alog_ref[...])
    da = dt * a
    row = lax.broadcasted_iota(jnp.int32, (c, c), 0)
    col = lax.broadcasted_iota(jnp.int32, (c, c), 1)
    tri = row >= col
    a_cs = jnp.dot(tri.astype(F32), da, precision=lax.Precision.HIGHEST,
                   preferred_element_type=F32)
    a_cs = a_cs * LOG2_E
    a_last = a_cs[c - 1:c, :]
    w1 = jnp.exp2(a_last - a_cs) * dt
    a_cs_t = a_cs.T
    dt_t = dt.T
    w1_t = w1.T
    lane_lo = lax.broadcasted_iota(jnp.int32, (c, LANES), 1) < SSD_HEAD_DIM
    neg_inf = jnp.float32(-jnp.inf)

    for g in range(SSD_GROUPS):
        b_g = xc_ref[:, inner + g * SSD_STATE: inner + (g + 1) * SSD_STATE]
        c_g = xc_ref[:, inner + (SSD_GROUPS + g) * SSD_STATE: inner + (SSD_GROUPS + g + 1) * SSD_STATE]
        cb = _dot_nt(c_g.astype(BF16), b_g.astype(BF16))
        b_gt = b_g.T
        y_pairs = []
        for pr in range(heads_per_group // 2):
            h0 = g * heads_per_group + 2 * pr
            psl = slice(h0 * SSD_HEAD_DIM, (h0 + 2) * SSD_HEAD_DIM)
            xs_b = xc_ref[:, psl].astype(BF16)
            prev = prev_ref[:, psl]
            rhs = jnp.concatenate([xs_b, prev.astype(BF16)], axis=0)
            ys, sts, cds = [], [], []
            for h in (h0, h0 + 1):
                colb = jnp.broadcast_to(a_cs[:, h:h + 1], (c, c))
                rowb = jnp.broadcast_to(a_cs_t[h:h + 1, :], (c, c))
                lmat = jnp.exp2(jnp.where(tri, colb - rowb, neg_inf))
                m = cb * lmat * jnp.broadcast_to(dt_t[h:h + 1, :], (c, c))
                e = jnp.exp2(colb)
                lhs = jnp.concatenate([m.astype(BF16), (e * c_g).astype(BF16)], axis=1)
                ys.append(_dot(lhs, rhs))
                bw = (b_gt * jnp.broadcast_to(w1_t[h:h + 1, :], (c, c))).astype(BF16)
                sts.append(_dot(bw, xs_b))
                cds.append(e[c - 1:c, :])
            y_pairs.append(jnp.where(lane_lo, ys[0], ys[1]))
            cd = jnp.where(lane_lo[0:1, :], cds[0], cds[1])
            prev_ref[:, psl] = cd * prev + jnp.where(lane_lo, sts[0], sts[1])
        gsl = slice(g * gw, (g + 1) * gw)
        y = jnp.concatenate(y_pairs, axis=1) + dsk_ref[:, gsl] * xc_ref[:, gsl]
        gy = y * _silu(z_ref[:, gsl].astype(F32))
        ms = jnp.mean(gy * gy, axis=-1, keepdims=True)
        o_ref[:, gsl] = (gy * lax.rsqrt(ms + EPS) * nw_ref[:, gsl]).astype(BF16)


def _ssd(proj, z_off, xbc_off, inner, conv_dim, dt_raw, conv_w, conv_b, dtb, alog, dskip, nw, bsz, seq):
    nc = seq // CHUNK
    n_heads = inner // SSD_HEAD_DIM
    heads_per_group = n_heads // SSD_GROUPS
    assert conv_dim == inner + 2 * SSD_GROUPS * SSD_STATE and SSD_STATE == CHUNK
    assert n_heads <= LANES and heads_per_group % 2 == 0 and 2 * SSD_HEAD_DIM == LANES
    assert z_off % inner == 0 and xbc_off % conv_dim == 0
    zb = z_off // inner
    xb = xbc_off // conv_dim
    body = functools.partial(_ssd_body, inner=inner, heads_per_group=heads_per_group)
    row = lambda b, c: (b * nc + c, 0)
    fixed = lambda b, c: (0, 0)
    return pl.pallas_call(
        body,
        grid=(bsz, nc),
        in_specs=[
            pl.BlockSpec((CHUNK, inner), lambda b, c: (b * nc + c, zb)),
            pl.BlockSpec((CHUNK, conv_dim), lambda b, c: (b * nc + c, xb)),
            pl.BlockSpec((CHUNK, LANES), row),
            pl.BlockSpec((SSD_CONV, conv_dim), fixed),
            pl.BlockSpec((1, conv_dim), fixed),
            pl.BlockSpec((1, LANES), fixed),
            pl.BlockSpec((1, LANES), fixed),
            pl.BlockSpec((1, inner), fixed),
            pl.BlockSpec((1, inner), fixed),
        ],
        out_specs=pl.BlockSpec((CHUNK, inner), row),
        out_shape=jax.ShapeDtypeStruct((bsz * seq, inner), BF16),
        scratch_shapes=[
            pltpu.VMEM((2 * CHUNK, conv_dim), BF16),
            pltpu.VMEM((CHUNK, conv_dim), F32),
            pltpu.VMEM((SSD_STATE, inner), F32),
        ],
        compiler_params=_params("parallel", "arbitrary"),
        name="ssd",
    )(proj, proj, dt_raw, conv_w, conv_b, dtb, alog, dskip, nw)


def _merge_body(yr_ref, ys_ref, wr_ref, ws_ref, ga_ref, gb_ref, o_ref):
    for rows in _row_blocks(yr_ref.shape[0]):
        br = _dot(yr_ref[rows, :], wr_ref[...])
        bs = _dot(ys_ref[rows, :], ws_ref[...])
        ga = _sigmoid(ga_ref[rows, :].astype(F32))
        gb = _sigmoid(gb_ref[rows, :].astype(F32))
        o_ref[rows, :] = (ga * br + gb * bs).astype(o_ref.dtype)


def _merge(y_ret, y_ssd, w_ret, w_ssd, gates, tm, tn):
    m, kr = y_ret.shape
    ks = y_ssd.shape[1]
    n = w_ret.shape[1]
    nb = n // tn
    return pl.pallas_call(
        _merge_body,
        grid=(m // tm, nb),
        in_specs=[
            pl.BlockSpec((tm, kr), lambda i, j: (i, 0)),
            pl.BlockSpec((tm, ks), lambda i, j: (i, 0)),
            pl.BlockSpec((kr, tn), lambda i, j: (0, j)),
            pl.BlockSpec((ks, tn), lambda i, j: (0, j)),
            pl.BlockSpec((tm, tn), lambda i, j: (i, j)),
            pl.BlockSpec((tm, tn), lambda i, j: (i, j + nb)),
        ],
        out_specs=pl.BlockSpec((tm, tn), lambda i, j: (i, j)),
        out_shape=jax.ShapeDtypeStruct((m, n), BF16),
        compiler_params=_params("parallel", "arbitrary"),
        name="merge",
    )(y_ret, y_ssd, w_ret, w_ssd, gates, gates)


def _mix_xattn_body(m_ref, x_ref, k_ref, v_ref, wmix_ref, nw_ref, wq_ref, wo_ref, o_ref, xn_ref):
    d = x_ref.shape[1]
    hd = d // XA_HEADS
    scale = hd ** -0.5
    o_ref[...] = x_ref[...] + _dot(m_ref[...], wmix_ref[...])
    xn_ref[...] = _rms(o_ref[...], nw_ref[...]).astype(BF16)
    for h in range(XA_HEADS):
        sl = slice(h * hd, (h + 1) * hd)
        q = _dot(xn_ref[...], wq_ref[:, sl]).astype(BF16)
        s = _dot_nt(q, k_ref[:, sl]) * scale
        s = s - jnp.max(s, axis=-1, keepdims=True)
        p = jnp.exp(s)
        p = p / jnp.sum(p, axis=-1, keepdims=True)
        o_h = _dot(p.astype(BF16), v_ref[:, sl]).astype(BF16)
        o_ref[...] += _dot(o_h, wo_ref[sl, :])


def _mix_xattn(merged, x, kv, w_mix, nw, w_q, w_o, bsz, seq, mem_len, tq):
    d = x.shape[1]
    nq = seq // tq
    tile = lambda b, i: (b * nq + i, 0)
    fixed = lambda b, i: (0, 0)
    resident = pl.Buffered(1)
    return pl.pallas_call(
        _mix_xattn_body,
        grid=(bsz, nq),
        in_specs=[
            pl.BlockSpec((tq, d), tile),
            pl.BlockSpec((tq, d), tile),
            pl.BlockSpec((mem_len, d), lambda b, i: (b, 0)),
            pl.BlockSpec((mem_len, d), lambda b, i: (b, 1)),
            pl.BlockSpec((d, d), fixed, pipeline_mode=resident),
            pl.BlockSpec((1, d), fixed),
            pl.BlockSpec((d, d), fixed, pipeline_mode=resident),
            pl.BlockSpec((d, d), fixed, pipeline_mode=resident),
        ],
        out_specs=pl.BlockSpec((tq, d), tile),
        out_shape=jax.ShapeDtypeStruct((bsz * seq, d), F32),
        scratch_shapes=[pltpu.VMEM((tq, d), BF16)],
        compiler_params=_params("parallel", "arbitrary"),
        name="mix_xattn",
    )(merged, x, kv, kv, w_mix, nw, w_q, w_o)


def _mlp_body(h_ref, nw_ref, w1_ref, w2_ref, fw_ref, o_ref, xn_ref):
    f = pl.program_id(1)
    blocks = _row_blocks(h_ref.shape[0])

    @pl.when(f == 0)
    def _():
        for rows in blocks:
            hh = h_ref[rows, :]
            xn_ref[rows, :] = _rms(hh, nw_ref[...]).astype(BF16)
            o_ref[rows, :] = hh

    w1b = w1_ref[...].astype(BF16)
    w2b = w2_ref[...].astype(BF16)
    for rows in blocks:
        u = _dot(xn_ref[rows, :], w1b)
        u = jnp.square(jnp.maximum(u, 0.0)).astype(BF16)
        o_ref[rows, :] += _dot(u, w2b)

    @pl.when(f == pl.num_programs(1) - 1)
    def _():
        for rows in blocks:
            o_ref[rows, :] = _rms(o_ref[rows, :], fw_ref[...])


def _mlp(h, nw, w1, w2, fw, tm, tf):
    m, d = h.shape
    dff = w1.shape[1]
    once = pl.Buffered(1)
    return pl.pallas_call(
        _mlp_body,
        grid=(m // tm, dff // tf),
        in_specs=[
            pl.BlockSpec((tm, d), lambda i, f: (i, 0), pipeline_mode=once),
            pl.BlockSpec((1, d), lambda i, f: (0, 0)),
            pl.BlockSpec((d, tf), lambda i, f: (0, f)),
            pl.BlockSpec((tf, d), lambda i, f: (f, 0)),
            pl.BlockSpec((1, d), lambda i, f: (0, 0)),
        ],
        out_specs=pl.BlockSpec((tm, d), lambda i, f: (i, 0)),
        out_shape=jax.ShapeDtypeStruct((m, d), F32),
        scratch_shapes=[pltpu.VMEM((tm, d), BF16)],
        compiler_params=_params("parallel", "arbitrary"),
        name="mlp",
    )(h, nw, w1, w2, fw)


def _rope_tables(seq, dk):
    half = dk // 2
    pos = jnp.arange(seq, dtype=F32)
    inv = jnp.exp(-math.log(ROPE_BASE) * jnp.arange(half, dtype=F32) / half)
    ang = pos[:, None] * inv[None, :]
    cos, sin = jnp.cos(ang), jnp.sin(ang)
    return jnp.concatenate([cos, cos], axis=1), jnp.concatenate([-sin, sin], axis=1)


def kernel(x, mem, norm_mix_w, w_in, conv_w, conv_b, dt_bias, a_log, d_skip, ret_norm_w, ssd_norm_w, w_ret_out, w_ssd_out, w_mix_out, norm_xa_w, mem_norm_w, w_xq, w_xkv, w_xo, norm_ff_w, w_ff1, w_ff2, final_norm_w):
    bsz, seq, d = x.shape
    mem_len = mem.shape[1]
    depth = w_in.shape[0]
    n_tok = bsz * seq
    ret_qk = RET_HEADS * RET_DK
    ret_v = RET_HEADS * RET_DV
    inner = w_ssd_out.shape[1]
    conv_dim = conv_w.shape[2]
    n_heads = inner // SSD_HEAD_DIM
    ret_w = 2 * ret_qk + 2 * ret_v
    off_dt = ret_w + inner + conv_dim
    off_gate = off_dt + n_heads
    assert w_in.shape[2] == off_gate + 2 * d and depth == 1

    tn_main = 1024
    assert conv_dim % tn_main == 0 and ret_w % tn_main == 0 and inner % tn_main == 0
    shift = conv_dim // tn_main
    xbc_off, ret_off, z_off = 0, conv_dim, conv_dim + ret_w

    cos2, sin2 = _rope_tables(seq, RET_DK)
    row = lambda v: v.reshape(1, -1)
    pad_heads = lambda v: jnp.pad(v.reshape(1, -1), ((0, 0), (0, LANES - n_heads)))

    w_in_t = jnp.swapaxes(w_in, 1, 2)

    h = x.reshape(n_tok, d)
    for l in range(depth):
        xn = _prenorm(h, row(norm_mix_w[l]), 512)
        proj = _in_proj_main(xn, w_in_t, l, off_dt, shift, 2048, tn_main)
        wt_tail = jnp.concatenate(
            [w_in_t[l, off_gate:], w_in_t[l, off_dt:off_gate],
             jnp.zeros((LANES - n_heads, d), F32)], axis=0)
        gates, dt_raw = _in_proj_tail(xn, wt_tail, 1024, wt_tail.shape[0] // 3)
        y_ret = _retention(proj, ret_off, cos2, sin2, row(ret_norm_w[l]), bsz, seq)
        y_ssd = _ssd(proj, z_off, xbc_off, inner, conv_dim, dt_raw, conv_w[l], row(conv_b[l]),
                     pad_heads(dt_bias[l]), pad_heads(a_log[l]),
                     row(jnp.repeat(d_skip[l], SSD_HEAD_DIM)), row(ssd_norm_w[l]), bsz, seq)
        merged = _merge(y_ret, y_ssd, w_ret_out[l].astype(BF16), w_ssd_out[l].astype(BF16),
                        gates, 1024, 512)
        kv = _norm_matmul(mem.reshape(bsz * mem_len, d), row(mem_norm_w[l]), w_xkv[l],
                          BF16, bsz * mem_len, 1024, "xa_kv")
        h = _mix_xattn(merged, h, kv, w_mix_out[l].astype(BF16), row(norm_xa_w[l]),
                       w_xq[l].astype(BF16), w_xo[l].astype(BF16), bsz, seq, mem_len, 512)
        h = _mlp(h, row(norm_ff_w[l]), w_ff1[l], w_ff2[l], row(final_norm_w), 1024, 512)
    return h.reshape(bsz, seq, d)
```

```python
import functools
import math

import numpy as np
import jax
import jax.numpy as jnp
from jax import lax
from jax.experimental import pallas as pl
from jax.experimental.pallas import tpu as pltpu

F32 = jnp.float32
BF16 = jnp.bfloat16
EPS = 1e-6

RET_HEADS = 8
RET_DK = 128
RET_DV = 256
CHUNK = 128
ROPE_BASE = 10000.0
SSD_HEAD_DIM = 64
SSD_GROUPS = 8
SSD_STATE = 128
SSD_CONV = 4
XA_HEADS = 4

V7X_VMEM_BYTES = 64 * 1024 * 1024
VMEM_LIMIT = V7X_VMEM_BYTES - 4 * 1024 * 1024
LANES = 128
ROW_SPLIT = 512
LOG2_E = math.log2(math.e)

_LOG_G = np.log(1.0 - np.exp2(-5.0 - np.arange(RET_HEADS, dtype=np.float32))).astype(np.float32)
_RET_CHUNK_DECAY = np.exp(np.float32(CHUNK) * _LOG_G).astype(np.float32)


def _params(*sem):
    return pltpu.CompilerParams(dimension_semantics=sem, vmem_limit_bytes=VMEM_LIMIT)


def _rms(x, w):
    ms = jnp.mean(x * x, axis=-1, keepdims=True)
    return x * lax.rsqrt(ms + EPS) * w


def _sigmoid(x):
    return 0.5 + 0.5 * jnp.tanh(0.5 * x)


def _silu(x):
    h = 0.5 * x
    return h + h * jnp.tanh(h)


def _dot(a, b):
    return jnp.dot(a, b, preferred_element_type=F32)


def _dot_nt(a, b):
    return lax.dot_general(a, b, (((1,), (1,)), ((), ())), preferred_element_type=F32)


def _row_blocks(n_rows):
    step = min(ROW_SPLIT, n_rows)
    return [slice(r, r + step) for r in range(0, n_rows, step)]


def _prenorm_body(x_ref, nw_ref, o_ref):
    o_ref[...] = _rms(x_ref[...], nw_ref[...]).astype(o_ref.dtype)


def _prenorm(x, nw, tm):
    m, k = x.shape
    return pl.pallas_call(
        _prenorm_body,
        grid=(m // tm,),
        in_specs=[pl.BlockSpec((tm, k), lambda i: (i, 0)), pl.BlockSpec((1, k), lambda i: (0, 0))],
        out_specs=pl.BlockSpec((tm, k), lambda i: (i, 0)),
        out_shape=jax.ShapeDtypeStruct((m, k), BF16),
        compiler_params=_params("parallel"),
        name="prenorm",
    )(x, nw)


def _matmul_nt_body(a_ref, wt_ref, o_ref):
    wb = wt_ref[...].astype(BF16)
    for rows in _row_blocks(a_ref.shape[0]):
        o_ref[rows, :] = _dot_nt(a_ref[rows, :], wb).astype(o_ref.dtype)


def _in_proj_main(a, w3, layer, n_cols, shift, tm, tn):
    m, k = a.shape
    nb = n_cols // tn
    assert n_cols % tn == 0 and m % tm == 0
    return pl.pallas_call(
        _matmul_nt_body,
        grid=(m // tm, nb),
        in_specs=[
            pl.BlockSpec((tm, k), lambda i, j: (i, 0)),
            pl.BlockSpec((None, tn, k), lambda i, j: (layer, j, 0)),
        ],
        out_specs=pl.BlockSpec((tm, tn), lambda i, j: (i, (j + shift) % nb)),
        out_shape=jax.ShapeDtypeStruct((m, n_cols), BF16),
        compiler_params=_params("parallel", "arbitrary"),
        name="in_proj_main",
    )(a, w3)


def _in_proj_tail_body(a_ref, wt_ref, g_ref, dt_ref):
    tn = wt_ref.shape[0]
    for rows in _row_blocks(a_ref.shape[0]):
        res = _dot_nt(a_ref[rows, :], wt_ref[...])
        g_ref[rows, :] = res.astype(g_ref.dtype)
        dt_ref[rows, :] = res[:, tn - LANES:]


def _in_proj_tail(a, wt_tail, tm, tn):
    m, k = a.shape
    n = wt_tail.shape[0]
    assert n % tn == 0 and tn % LANES == 0
    return pl.pallas_call(
        _in_proj_tail_body,
        grid=(m // tm, n // tn),
        in_specs=[
            pl.BlockSpec((tm, k), lambda i, j: (i, 0)),
            pl.BlockSpec((tn, k), lambda i, j: (j, 0)),
        ],
        out_specs=[
            pl.BlockSpec((tm, tn), lambda i, j: (i, j)),
            pl.BlockSpec((tm, LANES), lambda i, j: (i, 0)),
        ],
        out_shape=[jax.ShapeDtypeStruct((m, n), BF16), jax.ShapeDtypeStruct((m, LANES), F32)],
        compiler_params=_params("parallel", "arbitrary"),
        name="in_proj_tail",
    )(a, wt_tail)


def _norm_matmul_body(x_ref, nw_ref, w_ref, o_ref, xn_ref):
    @pl.when(pl.program_id(1) == 0)
    def _():
        xn_ref[...] = _rms(x_ref[...], nw_ref[...]).astype(BF16)

    o_ref[...] = _dot(xn_ref[...], w_ref[...].astype(BF16)).astype(o_ref.dtype)


def _norm_matmul(x, nw, w, out_dtype, tm, tn, name):
    m, k = x.shape
    n = w.shape[1]
    return pl.pallas_call(
        _norm_matmul_body,
        grid=(m // tm, n // tn),
        in_specs=[
            pl.BlockSpec((tm, k), lambda i, j: (i, 0)),
            pl.BlockSpec((1, k), lambda i, j: (0, 0)),
            pl.BlockSpec((k, tn), lambda i, j: (0, j)),
        ],
        out_specs=pl.BlockSpec((tm, tn), lambda i, j: (i, j)),
        out_shape=jax.ShapeDtypeStruct((m, n), out_dtype),
        scratch_shapes=[pltpu.VMEM((tm, k), BF16)],
        compiler_params=_params("parallel", "arbitrary"),
        name=name,
    )(x, nw, w)


def _retention_tables():
    idx = np.arange(CHUNK, dtype=np.float32)
    rel = idx[:, None] - idx[None, :]
    lg = _LOG_G[:, None, None]
    scale = np.float32(RET_DK ** -0.5)
    decay = np.where(rel[None] >= 0, np.exp(np.maximum(rel, 0.0)[None] * lg), 0.0) * scale
    xi = np.exp((idx + 1.0)[None, :, None] * lg) * scale
    zeta = np.exp((CHUNK - 1.0 - idx)[None, :, None] * lg)
    wide = (RET_HEADS, CHUNK, RET_DK)
    return (decay.astype(np.float32), np.broadcast_to(xi, wide).astype(np.float32),
            np.broadcast_to(zeta, wide).astype(np.float32))


def _retention_body(q_ref, k_ref, v_ref, g_ref, cos_ref, sin_ref, dec_ref, xi_ref, zeta_ref,
                    nw_ref, o_ref, state_ref):
    @pl.when(pl.program_id(1) == 0)
    def _():
        state_ref[...] = jnp.zeros_like(state_ref)

    cos = cos_ref[...]
    sin = sin_ref[...]
    for h in range(RET_HEADS):
        q = q_ref[:, h * RET_DK:(h + 1) * RET_DK].astype(F32)
        k = k_ref[:, h * RET_DK:(h + 1) * RET_DK].astype(F32)
        v = v_ref[:, h * RET_DV:(h + 1) * RET_DV]
        qr = q * cos + pltpu.roll(q, RET_DK // 2, 1) * sin
        kr = k * cos + pltpu.roll(k, RET_DK // 2, 1) * sin
        s = _dot_nt(qr.astype(BF16), kr.astype(BF16)) * dec_ref[h]
        st = state_ref[h]
        lhs = jnp.concatenate([s.astype(BF16), (qr * xi_ref[h]).astype(BF16)], axis=1)
        rhs = jnp.concatenate([v, st.astype(BF16)], axis=0)
        y = _dot(lhs, rhs)
        kz_t = (kr * zeta_ref[h]).T.astype(BF16)
        state_ref[h] = float(_RET_CHUNK_DECAY[h]) * st + _dot(kz_t, v)
        mu = jnp.mean(y, axis=-1, keepdims=True)
        yc = y - mu
        var = jnp.mean(yc * yc, axis=-1, keepdims=True)
        yn = yc * lax.rsqrt(var + EPS)
        hsl = slice(h * RET_DV, (h + 1) * RET_DV)
        o_ref[:, hsl] = (yn * nw_ref[:, hsl] * _silu(g_ref[:, hsl].astype(F32))).astype(BF16)


def _retention(proj, col_off, cos2, sin2, nw, bsz, seq):
    nc = seq // CHUNK
    qk = RET_HEADS * RET_DK
    vd = RET_HEADS * RET_DV
    assert vd == 2 * qk and col_off % vd == 0
    oq = col_off // qk
    ov = col_off // vd
    decay, xi, zeta = _retention_tables()
    table = lambda n: pl.BlockSpec((RET_HEADS, CHUNK, n), lambda b, c: (0, 0, 0))
    return pl.pallas_call(
        _retention_body,
        grid=(bsz, nc),
        in_specs=[
            pl.BlockSpec((CHUNK, qk), lambda b, c: (b * nc + c, oq)),
            pl.BlockSpec((CHUNK, qk), lambda b, c: (b * nc + c, oq + 1)),
            pl.BlockSpec((CHUNK, vd), lambda b, c: (b * nc + c, ov + 1)),
            pl.BlockSpec((CHUNK, vd), lambda b, c: (b * nc + c, ov + 2)),
            pl.BlockSpec((CHUNK, RET_DK), lambda b, c: (c, 0)),
            pl.BlockSpec((CHUNK, RET_DK), lambda b, c: (c, 0)),
            table(CHUNK), table(RET_DK), table(RET_DK),
            pl.BlockSpec((1, vd), lambda b, c: (0, 0)),
        ],
        out_specs=pl.BlockSpec((CHUNK, vd), lambda b, c: (b * nc + c, 0)),
        out_shape=jax.ShapeDtypeStruct((bsz * seq, vd), BF16),
        scratch_shapes=[pltpu.VMEM((RET_HEADS, RET_DK, RET_DV), F32)],
        compiler_params=_params("parallel", "arbitrary"),
        name="retention",
    )(proj, proj, proj, proj, cos2, sin2, decay, xi, zeta, nw)


def _ssd_body(z_ref, xbc_ref, dt_ref, cw_ref, cb_ref, dtb_ref, alog_ref, dsk_ref, nw_ref,
              o_ref, ext_ref, xc_ref, prev_ref, *, inner, heads_per_group):
    c = CHUNK
    conv_dim = xbc_ref.shape[1]
    gw = heads_per_group * SSD_HEAD_DIM
    first = pl.program_id(1) == 0

    @pl.when(first)
    def _():
        ext_ref[0:c, :] = jnp.zeros((c, conv_dim), BF16)
        prev_ref[...] = jnp.zeros_like(prev_ref)

    @pl.when(jnp.logical_not(first))
    def _():
        ext_ref[0:c, :] = ext_ref[c:2 * c, :]

    ext_ref[c:2 * c, :] = xbc_ref[...]

    srow = lax.broadcasted_iota(jnp.int32, (c, 2 * c), 0)
    scol = lax.broadcasted_iota(jnp.int32, (c, 2 * c), 1)
    shift_mat = jnp.concatenate(
        [jnp.where(scol - srow == c - (SSD_CONV - 1 - k), 1.0, 0.0).astype(BF16)
         for k in range(SSD_CONV - 1)], axis=0)
    cblk = 512
    for j in range(conv_dim // cblk):
        sl = slice(j * cblk, (j + 1) * cblk)
        shifted = _dot(shift_mat, ext_ref[:, sl])
        cwh = 0.5 * cw_ref[:, sl]
        acc = 0.5 * cb_ref[:, sl] + cwh[SSD_CONV - 1:SSD_CONV, :] * xbc_ref[:, sl].astype(F32)
        for k in range(SSD_CONV - 1):
            acc = acc + cwh[k:k + 1, :] * shifted[k * c:(k + 1) * c, :]
        xc_ref[:, sl] = acc + acc * jnp.tanh(acc)

    dtr = dt_ref[...] + dtb_ref[...]
    dt = jnp.maximum(dtr, 0.0) + jnp.log1p(jnp.exp(-jnp.abs(dtr)))
    a = -jnp.exp(alog_ref[...])
    da = dt * a
    row = lax.broadcasted_iota(jnp.int32, (c, c), 0)
    col = lax.broadcasted_iota(jnp.int32, (c, c), 1)
    tri = row >= col
    a_cs = jnp.dot(tri.astype(F32), da, precision=lax.Precision.HIGHEST,
                   preferred_element_type=F32)
    a_cs = a_cs * LOG2_E
    a_last = a_cs[c - 1:c, :]
    w1 = jnp.exp2(a_last - a_cs) * dt
    src_t = (a_cs - jnp.log2(dt)).T
    w1_t = w1.T
    lane_lo = lax.broadcasted_iota(jnp.int32, (c, LANES), 1) < SSD_HEAD_DIM
    neg_inf = jnp.float32(-jnp.inf)

    for g in range(SSD_GROUPS):
        b_g = xc_ref[:, inner + g * SSD_STATE: inner + (g + 1) * SSD_STATE]
        c_g = xc_ref[:, inner + (SSD_GROUPS + g) * SSD_STATE: inner + (SSD_GROUPS + g + 1) * SSD_STATE]
        cb = _dot_nt(c_g.astype(BF16), b_g.astype(BF16))
        b_gt = b_g.T
        y_pairs = []
        for pr in range(heads_per_group // 2):
            h0 = g * heads_per_group + 2 * pr
            psl = slice(h0 * SSD_HEAD_DIM, (h0 + 2) * SSD_HEAD_DIM)
            xs_b = xc_ref[:, psl].astype(BF16)
            prev = prev_ref[:, psl]
            rhs = jnp.concatenate([xs_b, prev.astype(BF16)], axis=0)
            lhs, bws, cds = [], [], []
            for h in (h0, h0 + 1):
                colb = jnp.broadcast_to(a_cs[:, h:h + 1], (c, c))
                rowb = jnp.broadcast_to(src_t[h:h + 1, :], (c, c))
                m = cb * jnp.exp2(jnp.where(tri, colb - rowb, neg_inf))
                e = jnp.exp2(colb)
                lhs.append(jnp.concatenate([m.astype(BF16), (e * c_g).astype(BF16)], axis=1))
                bws.append((b_gt * jnp.broadcast_to(w1_t[h:h + 1, :], (c, c))).astype(BF16))
                cds.append(e[c - 1:c, :])
            ys = _dot(jnp.concatenate(lhs, axis=0), rhs)
            sts = _dot(jnp.concatenate(bws, axis=0), xs_b)
            y_pairs.append(jnp.where(lane_lo, ys[0:c, :], ys[c:2 * c, :]))
            cd = jnp.where(lane_lo[0:1, :], cds[0], cds[1])
            prev_ref[:, psl] = cd * prev + jnp.where(lane_lo, sts[0:c, :], sts[c:2 * c, :])
        gsl = slice(g * gw, (g + 1) * gw)
        y = jnp.concatenate(y_pairs, axis=1) + dsk_ref[:, gsl] * xc_ref[:, gsl]
        gy = y * _silu(z_ref[:, gsl].astype(F32))
        ms = jnp.mean(gy * gy, axis=-1, keepdims=True)
        o_ref[:, gsl] = (gy * lax.rsqrt(ms + EPS) * nw_ref[:, gsl]).astype(BF16)


def _ssd(proj, z_off, xbc_off, inner, conv_dim, dt_raw, conv_w, conv_b, dtb, alog, dskip, nw, bsz, seq):
    nc = seq // CHUNK
    n_heads = inner // SSD_HEAD_DIM
    heads_per_group = n_heads // SSD_GROUPS
    assert conv_dim == inner + 2 * SSD_GROUPS * SSD_STATE and SSD_STATE == CHUNK
    assert n_heads <= LANES and heads_per_group % 2 == 0 and 2 * SSD_HEAD_DIM == LANES
    assert z_off % inner == 0 and xbc_off % conv_dim == 0
    zb = z_off // inner
    xb = xbc_off // conv_dim
    body = functools.partial(_ssd_body, inner=inner, heads_per_group=heads_per_group)
    row = lambda b, c: (b * nc + c, 0)
    fixed = lambda b, c: (0, 0)
    return pl.pallas_call(
        body,
        grid=(bsz, nc),
        in_specs=[
            pl.BlockSpec((CHUNK, inner), lambda b, c: (b * nc + c, zb)),
            pl.BlockSpec((CHUNK, conv_dim), lambda b, c: (b * nc + c, xb)),
            pl.BlockSpec((CHUNK, LANES), row),
            pl.BlockSpec((SSD_CONV, conv_dim), fixed),
            pl.BlockSpec((1, conv_dim), fixed),
            pl.BlockSpec((1, LANES), fixed),
            pl.BlockSpec((1, LANES), fixed),
            pl.BlockSpec((1, inner), fixed),
            pl.BlockSpec((1, inner), fixed),
        ],
        out_specs=pl.BlockSpec((CHUNK, inner), row),
        out_shape=jax.ShapeDtypeStruct((bsz * seq, inner), BF16),
        scratch_shapes=[
            pltpu.VMEM((2 * CHUNK, conv_dim), BF16),
            pltpu.VMEM((CHUNK, conv_dim), F32),
            pltpu.VMEM((SSD_STATE, inner), F32),
        ],
        compiler_params=_params("parallel", "arbitrary"),
        name="ssd",
    )(proj, proj, dt_raw, conv_w, conv_b, dtb, alog, dskip, nw)


def _merge_body(yr_ref, ys_ref, wr_ref, ws_ref, ga_ref, gb_ref, o_ref):
    for rows in _row_blocks(yr_ref.shape[0]):
        br = _dot(yr_ref[rows, :], wr_ref[...])
        bs = _dot(ys_ref[rows, :], ws_ref[...])
        ga = _sigmoid(ga_ref[rows, :].astype(F32))
        gb = _sigmoid(gb_ref[rows, :].astype(F32))
        o_ref[rows, :] = (ga * br + gb * bs).astype(o_ref.dtype)


def _merge(y_ret, y_ssd, w_ret, w_ssd, gates, tm, tn):
    m, kr = y_ret.shape
    ks = y_ssd.shape[1]
    n = w_ret.shape[1]
    nb = n // tn
    return pl.pallas_call(
        _merge_body,
        grid=(m // tm, nb),
        in_specs=[
            pl.BlockSpec((tm, kr), lambda i, j: (i, 0)),
            pl.BlockSpec((tm, ks), lambda i, j: (i, 0)),
            pl.BlockSpec((kr, tn), lambda i, j: (0, j)),
            pl.BlockSpec((ks, tn), lambda i, j: (0, j)),
            pl.BlockSpec((tm, tn), lambda i, j: (i, j)),
            pl.BlockSpec((tm, tn), lambda i, j: (i, j + nb)),
        ],
        out_specs=pl.BlockSpec((tm, tn), lambda i, j: (i, j)),
        out_shape=jax.ShapeDtypeStruct((m, n), BF16),
        compiler_params=_params("parallel", "arbitrary"),
        name="merge",
    )(y_ret, y_ssd, w_ret, w_ssd, gates, gates)


def _mix_xattn_body(m_ref, x_ref, k_ref, v_ref, wmix_ref, nw_ref, wq_ref, wo_ref, o_ref, xn_ref):
    d = x_ref.shape[1]
    hd = d // XA_HEADS
    scale = hd ** -0.5
    o_ref[...] = x_ref[...] + _dot(m_ref[...], wmix_ref[...])
    xn_ref[...] = _rms(o_ref[...], nw_ref[...]).astype(BF16)
    for h in range(XA_HEADS):
        sl = slice(h * hd, (h + 1) * hd)
        q = _dot(xn_ref[...], wq_ref[:, sl]).astype(BF16)
        s = _dot_nt(q, k_ref[:, sl]) * scale
        s = s - jnp.max(s, axis=-1, keepdims=True)
        p = jnp.exp(s)
        p = p / jnp.sum(p, axis=-1, keepdims=True)
        o_h = _dot(p.astype(BF16), v_ref[:, sl]).astype(BF16)
        o_ref[...] += _dot(o_h, wo_ref[sl, :])


def _mix_xattn(merged, x, kv, w_mix, nw, w_q, w_o, bsz, seq, mem_len, tq):
    d = x.shape[1]
    nq = seq // tq
    tile = lambda b, i: (b * nq + i, 0)
    fixed = lambda b, i: (0, 0)
    resident = pl.Buffered(1)
    return pl.pallas_call(
        _mix_xattn_body,
        grid=(bsz, nq),
        in_specs=[
            pl.BlockSpec((tq, d), tile),
            pl.BlockSpec((tq, d), tile),
            pl.BlockSpec((mem_len, d), lambda b, i: (b, 0)),
            pl.BlockSpec((mem_len, d), lambda b, i: (b, 1)),
            pl.BlockSpec((d, d), fixed, pipeline_mode=resident),
            pl.BlockSpec((1, d), fixed),
            pl.BlockSpec((d, d), fixed, pipeline_mode=resident),
            pl.BlockSpec((d, d), fixed, pipeline_mode=resident),
        ],
        out_specs=pl.BlockSpec((tq, d), tile),
        out_shape=jax.ShapeDtypeStruct((bsz * seq, d), F32),
        scratch_shapes=[pltpu.VMEM((tq, d), BF16)],
        compiler_params=_params("parallel", "arbitrary"),
        name="mix_xattn",
    )(merged, x, kv, kv, w_mix, nw, w_q, w_o)


def _mlp_body(h_ref, nw_ref, w1_ref, w2_ref, fw_ref, o_ref, xn_ref):
    f = pl.program_id(1)
    blocks = _row_blocks(h_ref.shape[0])

    @pl.when(f == 0)
    def _():
        for rows in blocks:
            hh = h_ref[rows, :]
            xn_ref[rows, :] = _rms(hh, nw_ref[...]).astype(BF16)
            o_ref[rows, :] = hh

    w1b = w1_ref[...].astype(BF16)
    w2b = w2_ref[...].astype(BF16)
    for rows in blocks:
        u = _dot(xn_ref[rows, :], w1b)
        u = jnp.square(jnp.maximum(u, 0.0)).astype(BF16)
        o_ref[rows, :] += _dot(u, w2b)

    @pl.when(f == pl.num_programs(1) - 1)
    def _():
        for rows in blocks:
            o_ref[rows, :] = _rms(o_ref[rows, :], fw_ref[...])


def _mlp(h, nw, w1, w2, fw, tm, tf):
    m, d = h.shape
    dff = w1.shape[1]
    once = pl.Buffered(1)
    return pl.pallas_call(
        _mlp_body,
        grid=(m // tm, dff // tf),
        in_specs=[
            pl.BlockSpec((tm, d), lambda i, f: (i, 0), pipeline_mode=once),
            pl.BlockSpec((1, d), lambda i, f: (0, 0)),
            pl.BlockSpec((d, tf), lambda i, f: (0, f)),
            pl.BlockSpec((tf, d), lambda i, f: (f, 0)),
            pl.BlockSpec((1, d), lambda i, f: (0, 0)),
        ],
        out_specs=pl.BlockSpec((tm, d), lambda i, f: (i, 0)),
        out_shape=jax.ShapeDtypeStruct((m, d), F32),
        scratch_shapes=[pltpu.VMEM((tm, d), BF16)],
        compiler_params=_params("parallel", "arbitrary"),
        name="mlp",
    )(h, nw, w1, w2, fw)


def _rope_tables(seq, dk):
    half = dk // 2
    pos = jnp.arange(seq, dtype=F32)
    inv = jnp.exp(-math.log(ROPE_BASE) * jnp.arange(half, dtype=F32) / half)
    ang = pos[:, None] * inv[None, :]
    cos, sin = jnp.cos(ang), jnp.sin(ang)
    return jnp.concatenate([cos, cos], axis=1), jnp.concatenate([-sin, sin], axis=1)


def kernel(x, mem, norm_mix_w, w_in, conv_w, conv_b, dt_bias, a_log, d_skip, ret_norm_w, ssd_norm_w, w_ret_out, w_ssd_out, w_mix_out, norm_xa_w, mem_norm_w, w_xq, w_xkv, w_xo, norm_ff_w, w_ff1, w_ff2, final_norm_w):
    bsz, seq, d = x.shape
    mem_len = mem.shape[1]
    depth = w_in.shape[0]
    n_tok = bsz * seq
    ret_qk = RET_HEADS * RET_DK
    ret_v = RET_HEADS * RET_DV
    inner = w_ssd_out.shape[1]
    conv_dim = conv_w.shape[2]
    n_heads = inner // SSD_HEAD_DIM
    ret_w = 2 * ret_qk + 2 * ret_v
    off_dt = ret_w + inner + conv_dim
    off_gate = off_dt + n_heads
    assert w_in.shape[2] == off_gate + 2 * d and depth == 1

    tn_main = 1024
    assert conv_dim % tn_main == 0 and ret_w % tn_main == 0 and inner % tn_main == 0
    shift = conv_dim // tn_main
    xbc_off, ret_off, z_off = 0, conv_dim, conv_dim + ret_w

    cos2, sin2 = _rope_tables(seq, RET_DK)
    row = lambda v: v.reshape(1, -1)
    pad_heads = lambda v: jnp.pad(v.reshape(1, -1), ((0, 0), (0, LANES - n_heads)))

    w_in_t = jnp.swapaxes(w_in, 1, 2)

    h = x.reshape(n_tok, d)
    for l in range(depth):
        xn = _prenorm(h, row(norm_mix_w[l]), 512)
        proj = _in_proj_main(xn, w_in_t, l, off_dt, shift, 2048, tn_main)
        tail_rows = lax.optimization_barrier((w_in_t[l, off_gate:], w_in_t[l, off_dt:off_gate]))
        wt_tail = jnp.concatenate(
            [tail_rows[0].astype(BF16), tail_rows[1].astype(BF16),
             jnp.zeros((LANES - n_heads, d), BF16)], axis=0)
        gates, dt_raw = _in_proj_tail(xn, wt_tail, 2048, wt_tail.shape[0] // 3)
        y_ret = _retention(proj, ret_off, cos2, sin2, row(ret_norm_w[l]), bsz, seq)
        y_ssd = _ssd(proj, z_off, xbc_off, inner, conv_dim, dt_raw, conv_w[l], row(conv_b[l]),
                     pad_heads(dt_bias[l]), pad_heads(a_log[l]),
                     row(jnp.repeat(d_skip[l], SSD_HEAD_DIM)), row(ssd_norm_w[l]), bsz, seq)
        merged = _merge(y_ret, y_ssd, w_ret_out[l].astype(BF16), w_ssd_out[l].astype(BF16),
                        gates, 1024, 512)
        kv = _norm_matmul(mem.reshape(bsz * mem_len, d), row(mem_norm_w[l]), w_xkv[l],
                          BF16, bsz * mem_len, 1024, "xa_kv")
        h = _mix_xattn(merged, h, kv, w_mix_out[l].astype(BF16), row(norm_xa_w[l]),
                       w_xq[l].astype(BF16), w_xo[l].astype(BF16), bsz, seq, mem_len, 512)
        h = _mlp(h, row(norm_ff_w[l]), w_ff1[l], w_ff2[l], row(final_norm_w), 1024, 512)
    return h.reshape(bsz, seq, d)
```

```python
import functools
import itertools
import math

import numpy as np
import jax
import jax.numpy as jnp
from jax import lax
from jax.experimental import pallas as pl
from jax.experimental.pallas import tpu as pltpu

F32 = jnp.float32
BF16 = jnp.bfloat16
EPS = 1e-6

RET_HEADS = 8
RET_DK = 128
RET_DV = 256
CHUNK = 128
ROPE_BASE = 10000.0
SSD_HEAD_DIM = 64
SSD_GROUPS = 8
SSD_STATE = 128
SSD_CONV = 4
XA_HEADS = 4

V7X_VMEM_BYTES = 64 * 1024 * 1024
VMEM_LIMIT = V7X_VMEM_BYTES - 4 * 1024 * 1024
LANES = 128
ROW_SPLIT = 512
LOG2_E = math.log2(math.e)
RET_STEP_CHUNKS = 4
SSD_STEP_CHUNKS = 2

_LOG_G = np.log(1.0 - np.exp2(-5.0 - np.arange(RET_HEADS, dtype=np.float32))).astype(np.float32)
_RET_CHUNK_DECAY = np.exp(np.float32(CHUNK) * _LOG_G).astype(np.float32)


def _params(*sem):
    return pltpu.CompilerParams(dimension_semantics=sem, vmem_limit_bytes=VMEM_LIMIT)


def _rms(x, w):
    ms = jnp.mean(x * x, axis=-1, keepdims=True)
    return x * lax.rsqrt(ms + EPS) * w


def _sigmoid(x):
    return 0.5 + 0.5 * jnp.tanh(0.5 * x)


def _silu(x):
    h = 0.5 * x
    return h + h * jnp.tanh(h)


def _dot(a, b):
    return jnp.dot(a, b, preferred_element_type=F32)


def _dot_nt(a, b):
    return lax.dot_general(a, b, (((1,), (1,)), ((), ())), preferred_element_type=F32)


def _row_blocks(n_rows):
    step = min(ROW_SPLIT, n_rows)
    return [slice(r, r + step) for r in range(0, n_rows, step)]


def _prenorm_body(x_ref, nw_ref, o_ref):
    o_ref[...] = _rms(x_ref[...], nw_ref[...]).astype(o_ref.dtype)


def _prenorm(x, nw, tm):
    m, k = x.shape
    return pl.pallas_call(
        _prenorm_body,
        grid=(m // tm,),
        in_specs=[pl.BlockSpec((tm, k), lambda i: (i, 0)), pl.BlockSpec((1, k), lambda i: (0, 0))],
        out_specs=pl.BlockSpec((tm, k), lambda i: (i, 0)),
        out_shape=jax.ShapeDtypeStruct((m, k), BF16),
        compiler_params=_params("parallel"),
        name="prenorm",
    )(x, nw)


def _matmul_nt_body(a_ref, wt_ref, o_ref):
    wb = wt_ref[...].astype(BF16)
    for rows in _row_blocks(a_ref.shape[0]):
        o_ref[rows, :] = _dot_nt(a_ref[rows, :], wb).astype(o_ref.dtype)


def _in_proj_main(a, w3, layer, n_cols, shift, tm, tn):
    m, k = a.shape
    nb = n_cols // tn
    assert n_cols % tn == 0 and m % tm == 0
    return pl.pallas_call(
        _matmul_nt_body,
        grid=(m // tm, nb),
        in_specs=[
            pl.BlockSpec((tm, k), lambda i, j: (i, 0)),
            pl.BlockSpec((None, tn, k), lambda i, j: (layer, j, 0)),
        ],
        out_specs=pl.BlockSpec((tm, tn), lambda i, j: (i, (j + shift) % nb)),
        out_shape=jax.ShapeDtypeStruct((m, n_cols), BF16),
        compiler_params=_params("parallel", "arbitrary"),
        name="in_proj_main",
    )(a, w3)


def _in_proj_tail_body(a_ref, wt_ref, g_ref, dt_ref):
    tn = wt_ref.shape[0]
    for rows in _row_blocks(a_ref.shape[0]):
        res = _dot_nt(a_ref[rows, :], wt_ref[...])
        g_ref[rows, :] = res.astype(g_ref.dtype)
        dt_ref[rows, :] = res[:, tn - LANES:]


def _in_proj_tail(a, wt_tail, tm, tn):
    m, k = a.shape
    n = wt_tail.shape[0]
    assert n % tn == 0 and tn % LANES == 0
    return pl.pallas_call(
        _in_proj_tail_body,
        grid=(m // tm, n // tn),
        in_specs=[
            pl.BlockSpec((tm, k), lambda i, j: (i, 0)),
            pl.BlockSpec((tn, k), lambda i, j: (j, 0)),
        ],
        out_specs=[
            pl.BlockSpec((tm, tn), lambda i, j: (i, j)),
            pl.BlockSpec((tm, LANES), lambda i, j: (i, 0)),
        ],
        out_shape=[jax.ShapeDtypeStruct((m, n), BF16), jax.ShapeDtypeStruct((m, LANES), F32)],
        compiler_params=_params("parallel", "arbitrary"),
        name="in_proj_tail",
    )(a, wt_tail)


def _norm_matmul_body(x_ref, nw_ref, w_ref, o_ref, xn_ref):
    @pl.when(pl.program_id(1) == 0)
    def _():
        xn_ref[...] = _rms(x_ref[...], nw_ref[...]).astype(BF16)

    o_ref[...] = _dot(xn_ref[...], w_ref[...].astype(BF16)).astype(o_ref.dtype)


def _norm_matmul(x, nw, w, out_dtype, tm, tn, name):
    m, k = x.shape
    n = w.shape[1]
    return pl.pallas_call(
        _norm_matmul_body,
        grid=(m // tm, n // tn),
        in_specs=[
            pl.BlockSpec((tm, k), lambda i, j: (i, 0)),
            pl.BlockSpec((1, k), lambda i, j: (0, 0)),
            pl.BlockSpec((k, tn), lambda i, j: (0, j)),
        ],
        out_specs=pl.BlockSpec((tm, tn), lambda i, j: (i, j)),
        out_shape=jax.ShapeDtypeStruct((m, n), out_dtype),
        scratch_shapes=[pltpu.VMEM((tm, k), BF16)],
        compiler_params=_params("parallel", "arbitrary"),
        name=name,
    )(x, nw, w)


def _retention_tables():
    idx = np.arange(CHUNK, dtype=np.float32)
    rel = idx[:, None] - idx[None, :]
    lg = _LOG_G[:, None, None]
    scale = np.float32(RET_DK ** -0.5)
    decay = np.where(rel[None] >= 0, np.exp(np.maximum(rel, 0.0)[None] * lg), 0.0) * scale
    xi = np.exp((idx + 1.0)[None, :, None] * lg) * scale
    zeta = np.exp((CHUNK - 1.0 - idx)[None, :, None] * lg)
    wide = (RET_HEADS, CHUNK, RET_DK)
    return (decay.astype(np.float32), np.broadcast_to(xi, wide).astype(np.float32),
            np.broadcast_to(zeta, wide).astype(np.float32))


def _retention_body(q_ref, k_ref, v_ref, g_ref, cos_ref, sin_ref, dec_ref, xi_ref, zeta_ref,
                    nw_ref, o_ref, state_ref):
    @pl.when(pl.program_id(1) == 0)
    def _():
        state_ref[...] = jnp.zeros_like(state_ref)

    for cc, h in itertools.product(range(RET_STEP_CHUNKS), range(RET_HEADS)):
        rows = slice(cc * CHUNK, (cc + 1) * CHUNK)
        cos = cos_ref[rows, :]
        sin = sin_ref[rows, :]
        q = q_ref[rows, h * RET_DK:(h + 1) * RET_DK].astype(F32)
        k = k_ref[rows, h * RET_DK:(h + 1) * RET_DK].astype(F32)
        v = v_ref[rows, h * RET_DV:(h + 1) * RET_DV]
        qr = q * cos + pltpu.roll(q, RET_DK // 2, 1) * sin
        kr = k * cos + pltpu.roll(k, RET_DK // 2, 1) * sin
        s = _dot_nt(qr.astype(BF16), kr.astype(BF16)) * dec_ref[h]
        st = state_ref[h]
        lhs = jnp.concatenate([s.astype(BF16), (qr * xi_ref[h]).astype(BF16)], axis=1)
        rhs = jnp.concatenate([v, st.astype(BF16)], axis=0)
        y = _dot(lhs, rhs)
        kz_t = (kr * zeta_ref[h]).T.astype(BF16)
        state_ref[h] = float(_RET_CHUNK_DECAY[h]) * st + _dot(kz_t, v)
        mu = jnp.mean(y, axis=-1, keepdims=True)
        yc = y - mu
        var = jnp.mean(yc * yc, axis=-1, keepdims=True)
        yn = yc * lax.rsqrt(var + EPS)
        hsl = slice(h * RET_DV, (h + 1) * RET_DV)
        o_ref[rows, hsl] = (yn * nw_ref[:, hsl] * _silu(g_ref[rows, hsl].astype(F32))).astype(BF16)


def _retention(proj, col_off, cos2, sin2, nw, bsz, seq):
    rows = RET_STEP_CHUNKS * CHUNK
    nc = seq // rows
    qk = RET_HEADS * RET_DK
    vd = RET_HEADS * RET_DV
    assert vd == 2 * qk and col_off % vd == 0 and seq % rows == 0
    oq = col_off // qk
    ov = col_off // vd
    decay, xi, zeta = _retention_tables()
    table = lambda n: pl.BlockSpec((RET_HEADS, CHUNK, n), lambda b, c: (0, 0, 0))
    return pl.pallas_call(
        _retention_body,
        grid=(bsz, nc),
        in_specs=[
            pl.BlockSpec((rows, qk), lambda b, c: (b * nc + c, oq)),
            pl.BlockSpec((rows, qk), lambda b, c: (b * nc + c, oq + 1)),
            pl.BlockSpec((rows, vd), lambda b, c: (b * nc + c, ov + 1)),
            pl.BlockSpec((rows, vd), lambda b, c: (b * nc + c, ov + 2)),
            pl.BlockSpec((rows, RET_DK), lambda b, c: (c, 0)),
            pl.BlockSpec((rows, RET_DK), lambda b, c: (c, 0)),
            table(CHUNK), table(RET_DK), table(RET_DK),
            pl.BlockSpec((1, vd), lambda b, c: (0, 0)),
        ],
        out_specs=pl.BlockSpec((rows, vd), lambda b, c: (b * nc + c, 0)),
        out_shape=jax.ShapeDtypeStruct((bsz * seq, vd), BF16),
        scratch_shapes=[pltpu.VMEM((RET_HEADS, RET_DK, RET_DV), F32)],
        compiler_params=_params("parallel", "arbitrary"),
        name="retention",
    )(proj, proj, proj, proj, cos2, sin2, decay, xi, zeta, nw)


def _ssd_body(z_ref, xbc_ref, dt_ref, cw_ref, cb_ref, dtb_ref, alog_ref, dsk_ref, nw_ref,
              o_ref, ext_ref, xc_ref, prev_ref, *, inner, heads_per_group):
    c = CHUNK
    n_sub = z_ref.shape[0] // c
    conv_dim = xbc_ref.shape[1]
    gw = heads_per_group * SSD_HEAD_DIM
    first = pl.program_id(1) == 0

    @pl.when(first)
    def _():
        ext_ref[0:c, :] = jnp.zeros((c, conv_dim), BF16)
        prev_ref[...] = jnp.zeros_like(prev_ref)

    ext_ref[c:2 * c, :] = xbc_ref[0:c, :]

    srow = lax.broadcasted_iota(jnp.int32, (c, 2 * c), 0)
    scol = lax.broadcasted_iota(jnp.int32, (c, 2 * c), 1)
    shift_mat = jnp.concatenate(
        [jnp.where(scol - srow == c - (SSD_CONV - 1 - k), 1.0, 0.0).astype(BF16)
         for k in range(SSD_CONV - 1)], axis=0)
    row = lax.broadcasted_iota(jnp.int32, (c, c), 0)
    col = lax.broadcasted_iota(jnp.int32, (c, c), 1)
    tri = row >= col
    lane_lo = lax.broadcasted_iota(jnp.int32, (c, LANES), 1) < SSD_HEAD_DIM
    neg_inf = jnp.float32(-jnp.inf)
    a = -jnp.exp(alog_ref[...])
    cblk = 512

    for cc in range(n_sub):
        rows = slice(cc * c, (cc + 1) * c)
        xcc_ref = xc_ref.at[cc]
        for j in range(conv_dim // cblk):
            sl = slice(j * cblk, (j + 1) * cblk)
            hist = ext_ref[:, sl] if cc == 0 else xbc_ref[(cc - 1) * c:(cc + 1) * c, sl]
            shifted = _dot(shift_mat, hist)
            cwh = 0.5 * cw_ref[:, sl]
            acc = 0.5 * cb_ref[:, sl] + cwh[SSD_CONV - 1:SSD_CONV, :] * xbc_ref[rows, sl].astype(F32)
            for k in range(SSD_CONV - 1):
                acc = acc + cwh[k:k + 1, :] * shifted[k * c:(k + 1) * c, :]
            xcc_ref[:, sl] = acc + acc * jnp.tanh(acc)

        dtr = dt_ref[rows, :] + dtb_ref[...]
        dt = jnp.maximum(dtr, 0.0) + jnp.log1p(jnp.exp(-jnp.abs(dtr)))
        da = dt * a
        a_cs = jnp.dot(tri.astype(F32), da, precision=lax.Precision.HIGHEST,
                       preferred_element_type=F32)
        a_cs = a_cs * LOG2_E
        a_last = a_cs[c - 1:c, :]
        w1 = jnp.exp2(a_last - a_cs) * dt
        src_t = (a_cs - jnp.log2(dt)).T
        w1_t = w1.T

        for g in range(SSD_GROUPS):
            b_g = xcc_ref[:, inner + g * SSD_STATE: inner + (g + 1) * SSD_STATE]
            c_g = xcc_ref[:, inner + (SSD_GROUPS + g) * SSD_STATE:
                          inner + (SSD_GROUPS + g + 1) * SSD_STATE]
            cb = _dot_nt(c_g.astype(BF16), b_g.astype(BF16))
            b_gt = b_g.T
            y_pairs = []
            for pr in range(heads_per_group // 2):
                h0 = g * heads_per_group + 2 * pr
                psl = slice(h0 * SSD_HEAD_DIM, (h0 + 2) * SSD_HEAD_DIM)
                xs_b = xcc_ref[:, psl].astype(BF16)
                prev = prev_ref[:, psl]
                rhs = jnp.concatenate([xs_b, prev.astype(BF16)], axis=0)
                lhs, bws, cds = [], [], []
                for h in (h0, h0 + 1):
                    colb = jnp.broadcast_to(a_cs[:, h:h + 1], (c, c))
                    rowb = jnp.broadcast_to(src_t[h:h + 1, :], (c, c))
                    m = cb * jnp.exp2(jnp.where(tri, colb - rowb, neg_inf))
                    e = jnp.exp2(colb)
                    lhs.append(jnp.concatenate([m.astype(BF16), (e * c_g).astype(BF16)], axis=1))
                    bws.append((b_gt * jnp.broadcast_to(w1_t[h:h + 1, :], (c, c))).astype(BF16))
                    cds.append(e[c - 1:c, :])
                ys = _dot(jnp.concatenate(lhs, axis=0), rhs)
                sts = _dot(jnp.concatenate(bws, axis=0), xs_b)
                y_pairs.append(jnp.where(lane_lo, ys[0:c, :], ys[c:2 * c, :]))
                cd = jnp.where(lane_lo[0:1, :], cds[0], cds[1])
                prev_ref[:, psl] = cd * prev + jnp.where(lane_lo, sts[0:c, :], sts[c:2 * c, :])
            gsl = slice(g * gw, (g + 1) * gw)
            y = jnp.concatenate(y_pairs, axis=1) + dsk_ref[:, gsl] * xcc_ref[:, gsl]
            gy = y * _silu(z_ref[rows, gsl].astype(F32))
            ms = jnp.mean(gy * gy, axis=-1, keepdims=True)
            o_ref[rows, gsl] = (gy * lax.rsqrt(ms + EPS) * nw_ref[:, gsl]).astype(BF16)

    ext_ref[0:c, :] = xbc_ref[(n_sub - 1) * c:n_sub * c, :]


def _ssd(proj, z_off, xbc_off, inner, conv_dim, dt_raw, conv_w, conv_b, dtb, alog, dskip, nw, bsz, seq):
    rows = SSD_STEP_CHUNKS * CHUNK
    nc = seq // rows
    n_heads = inner // SSD_HEAD_DIM
    heads_per_group = n_heads // SSD_GROUPS
    assert conv_dim == inner + 2 * SSD_GROUPS * SSD_STATE and SSD_STATE == CHUNK and seq % rows == 0
    assert n_heads <= LANES and heads_per_group % 2 == 0 and 2 * SSD_HEAD_DIM == LANES
    assert z_off % inner == 0 and xbc_off % conv_dim == 0
    zb = z_off // inner
    xb = xbc_off // conv_dim
    body = functools.partial(_ssd_body, inner=inner, heads_per_group=heads_per_group)
    row = lambda b, c: (b * nc + c, 0)
    fixed = lambda b, c: (0, 0)
    return pl.pallas_call(
        body,
        grid=(bsz, nc),
        in_specs=[
            pl.BlockSpec((rows, inner), lambda b, c: (b * nc + c, zb)),
            pl.BlockSpec((rows, conv_dim), lambda b, c: (b * nc + c, xb)),
            pl.BlockSpec((rows, LANES), row),
            pl.BlockSpec((SSD_CONV, conv_dim), fixed),
            pl.BlockSpec((1, conv_dim), fixed),
            pl.BlockSpec((1, LANES), fixed),
            pl.BlockSpec((1, LANES), fixed),
            pl.BlockSpec((1, inner), fixed),
            pl.BlockSpec((1, inner), fixed),
        ],
        out_specs=pl.BlockSpec((rows, inner), row),
        out_shape=jax.ShapeDtypeStruct((bsz * seq, inner), BF16),
        scratch_shapes=[
            pltpu.VMEM((2 * CHUNK, conv_dim), BF16),
            pltpu.VMEM((SSD_STEP_CHUNKS, CHUNK, conv_dim), F32),
            pltpu.VMEM((SSD_STATE, inner), F32),
        ],
        compiler_params=_params("parallel", "arbitrary"),
        name="ssd",
    )(proj, proj, dt_raw, conv_w, conv_b, dtb, alog, dskip, nw)


def _merge_body(yr_ref, ys_ref, wr_ref, ws_ref, ga_ref, gb_ref, o_ref):
    for rows in _row_blocks(yr_ref.shape[0]):
        br = _dot(yr_ref[rows, :], wr_ref[...])
        bs = _dot(ys_ref[rows, :], ws_ref[...])
        ga = _sigmoid(ga_ref[rows, :].astype(F32))
        gb = _sigmoid(gb_ref[rows, :].astype(F32))
        o_ref[rows, :] = (ga * br + gb * bs).astype(o_ref.dtype)


def _merge(y_ret, y_ssd, w_ret, w_ssd, gates, tm, tn):
    m, kr = y_ret.shape
    ks = y_ssd.shape[1]
    n = w_ret.shape[1]
    nb = n // tn
    return pl.pallas_call(
        _merge_body,
        grid=(m // tm, nb),
        in_specs=[
            pl.BlockSpec((tm, kr), lambda i, j: (i, 0)),
            pl.BlockSpec((tm, ks), lambda i, j: (i, 0)),
            pl.BlockSpec((kr, tn), lambda i, j: (0, j)),
            pl.BlockSpec((ks, tn), lambda i, j: (0, j)),
            pl.BlockSpec((tm, tn), lambda i, j: (i, j)),
            pl.BlockSpec((tm, tn), lambda i, j: (i, j + nb)),
        ],
        out_specs=pl.BlockSpec((tm, tn), lambda i, j: (i, j)),
        out_shape=jax.ShapeDtypeStruct((m, n), BF16),
        compiler_params=_params("parallel", "arbitrary"),
        name="merge",
    )(y_ret, y_ssd, w_ret, w_ssd, gates, gates)


def _mix_xattn_body(m_ref, x_ref, k_ref, v_ref, wmix_ref, nw_ref, wq_ref, wo_ref, o_ref, xn_ref):
    d = x_ref.shape[1]
    hd = d // XA_HEADS
    scale = hd ** -0.5
    o_ref[...] = x_ref[...] + _dot(m_ref[...], wmix_ref[...])
    xn_ref[...] = _rms(o_ref[...], nw_ref[...]).astype(BF16)
    for h in range(XA_HEADS):
        sl = slice(h * hd, (h + 1) * hd)
        q = _dot(xn_ref[...], wq_ref[:, sl]).astype(BF16)
        s = _dot_nt(q, k_ref[:, sl]) * scale
        s = s - jnp.max(s, axis=-1, keepdims=True)
        p = jnp.exp(s)
        p = p / jnp.sum(p, axis=-1, keepdims=True)
        o_h = _dot(p.astype(BF16), v_ref[:, sl]).astype(BF16)
        o_ref[...] += _dot(o_h, wo_ref[sl, :])


def _mix_xattn(merged, x, kv, w_mix, nw, w_q, w_o, bsz, seq, mem_len, tq):
    d = x.shape[1]
    nq = seq // tq
    tile = lambda b, i: (b * nq + i, 0)
    fixed = lambda b, i: (0, 0)
    resident = pl.Buffered(1)
    return pl.pallas_call(
        _mix_xattn_body,
        grid=(bsz, nq),
        in_specs=[
            pl.BlockSpec((tq, d), tile),
            pl.BlockSpec((tq, d), tile),
            pl.BlockSpec((mem_len, d), lambda b, i: (b, 0)),
            pl.BlockSpec((mem_len, d), lambda b, i: (b, 1)),
            pl.BlockSpec((d, d), fixed, pipeline_mode=resident),
            pl.BlockSpec((1, d), fixed),
            pl.BlockSpec((d, d), fixed, pipeline_mode=resident),
            pl.BlockSpec((d, d), fixed, pipeline_mode=resident),
        ],
        out_specs=pl.BlockSpec((tq, d), tile),
        out_shape=jax.ShapeDtypeStruct((bsz * seq, d), F32),
        scratch_shapes=[pltpu.VMEM((tq, d), BF16)],
        compiler_params=_params("parallel", "arbitrary"),
        name="mix_xattn",
    )(merged, x, kv, kv, w_mix, nw, w_q, w_o)


def _mlp_body(h_ref, nw_ref, w1_ref, w2_ref, fw_ref, o_ref, xn_ref):
    f = pl.program_id(1)
    blocks = _row_blocks(h_ref.shape[0])

    @pl.when(f == 0)
    def _():
        for rows in blocks:
            hh = h_ref[rows, :]
            xn_ref[rows, :] = _rms(hh, nw_ref[...]).astype(BF16)
            o_ref[rows, :] = hh

    w1b = w1_ref[...].astype(BF16)
    w2b = w2_ref[...].astype(BF16)
    for rows in blocks:
        u = _dot(xn_ref[rows, :], w1b)
        u = jnp.square(jnp.maximum(u, 0.0)).astype(BF16)
        o_ref[rows, :] += _dot(u, w2b)

    @pl.when(f == pl.num_programs(1) - 1)
    def _():
        for rows in blocks:
            o_ref[rows, :] = _rms(o_ref[rows, :], fw_ref[...])


def _mlp(h, nw, w1, w2, fw, tm, tf):
    m, d = h.shape
    dff = w1.shape[1]
    once = pl.Buffered(1)
    return pl.pallas_call(
        _mlp_body,
        grid=(m // tm, dff // tf),
        in_specs=[
            pl.BlockSpec((tm, d), lambda i, f: (i, 0), pipeline_mode=once),
            pl.BlockSpec((1, d), lambda i, f: (0, 0)),
            pl.BlockSpec((d, tf), lambda i, f: (0, f)),
            pl.BlockSpec((tf, d), lambda i, f: (f, 0)),
            pl.BlockSpec((1, d), lambda i, f: (0, 0)),
        ],
        out_specs=pl.BlockSpec((tm, d), lambda i, f: (i, 0)),
        out_shape=jax.ShapeDtypeStruct((m, d), F32),
        scratch_shapes=[pltpu.VMEM((tm, d), BF16)],
        compiler_params=_params("parallel", "arbitrary"),
        name="mlp",
    )(h, nw, w1, w2, fw)


def _rope_tables(seq, dk):
    half = dk // 2
    pos = jnp.arange(seq, dtype=F32)
    inv = jnp.exp(-math.log(ROPE_BASE) * jnp.arange(half, dtype=F32) / half)
    ang = pos[:, None] * inv[None, :]
    cos, sin = jnp.cos(ang), jnp.sin(ang)
    return jnp.concatenate([cos, cos], axis=1), jnp.concatenate([-sin, sin], axis=1)


def kernel(x, mem, norm_mix_w, w_in, conv_w, conv_b, dt_bias, a_log, d_skip, ret_norm_w, ssd_norm_w, w_ret_out, w_ssd_out, w_mix_out, norm_xa_w, mem_norm_w, w_xq, w_xkv, w_xo, norm_ff_w, w_ff1, w_ff2, final_norm_w):
    bsz, seq, d = x.shape
    mem_len = mem.shape[1]
    depth = w_in.shape[0]
    n_tok = bsz * seq
    ret_qk = RET_HEADS * RET_DK
    ret_v = RET_HEADS * RET_DV
    inner = w_ssd_out.shape[1]
    conv_dim = conv_w.shape[2]
    n_heads = inner // SSD_HEAD_DIM
    ret_w = 2 * ret_qk + 2 * ret_v
    off_dt = ret_w + inner + conv_dim
    off_gate = off_dt + n_heads
    assert w_in.shape[2] == off_gate + 2 * d and depth == 1

    tn_main = 1024
    assert conv_dim % tn_main == 0 and ret_w % tn_main == 0 and inner % tn_main == 0
    shift = conv_dim // tn_main
    xbc_off, ret_off, z_off = 0, conv_dim, conv_dim + ret_w

    cos2, sin2 = _rope_tables(seq, RET_DK)
    row = lambda v: v.reshape(1, -1)
    pad_heads = lambda v: jnp.pad(v.reshape(1, -1), ((0, 0), (0, LANES - n_heads)))

    w_in_t = jnp.swapaxes(w_in, 1, 2)

    h = x.reshape(n_tok, d)
    for l in range(depth):
        xn = _prenorm(h, row(norm_mix_w[l]), 512)
        proj = _in_proj_main(xn, w_in_t, l, off_dt, shift, 2048, tn_main)
        tail_rows = lax.optimization_barrier((w_in_t[l, off_gate:], w_in_t[l, off_dt:off_gate]))
        wt_tail = jnp.concatenate(
            [tail_rows[0].astype(BF16), tail_rows[1].astype(BF16),
             jnp.zeros((LANES - n_heads, d), BF16)], axis=0)
        gates, dt_raw = _in_proj_tail(xn, wt_tail, 2048, wt_tail.shape[0] // 3)
        y_ret = _retention(proj, ret_off, cos2, sin2, row(ret_norm_w[l]), bsz, seq)
        y_ssd = _ssd(proj, z_off, xbc_off, inner, conv_dim, dt_raw, conv_w[l], row(conv_b[l]),
                     pad_heads(dt_bias[l]), pad_heads(a_log[l]),
                     row(jnp.repeat(d_skip[l], SSD_HEAD_DIM)), row(ssd_norm_w[l]), bsz, seq)
        merged = _merge(y_ret, y_ssd, w_ret_out[l].astype(BF16), w_ssd_out[l].astype(BF16),
                        gates, 1024, 512)
        kv = _norm_matmul(mem.reshape(bsz * mem_len, d), row(mem_norm_w[l]), w_xkv[l],
                          BF16, bsz * mem_len, 1024, "xa_kv")
        h = _mix_xattn(merged, h, kv, w_mix_out[l].astype(BF16), row(norm_xa_w[l]),
                       w_xq[l].astype(BF16), w_xo[l].astype(BF16), bsz, seq, mem_len, 512)
        h = _mlp(h, row(norm_ff_w[l]), w_ff1[l], w_ff2[l], row(final_norm_w), 1024, 512)
    return h.reshape(bsz, seq, d)
```

```python
import functools
import itertools
import math

import numpy as np
import jax
import jax.numpy as jnp
from jax import lax
from jax.experimental import pallas as pl
from jax.experimental.pallas import tpu as pltpu

F32 = jnp.float32
BF16 = jnp.bfloat16
EPS = 1e-6

RET_HEADS = 8
RET_DK = 128
RET_DV = 256
CHUNK = 128
ROPE_BASE = 10000.0
SSD_HEAD_DIM = 64
SSD_GROUPS = 8
SSD_STATE = 128
SSD_CONV = 4
XA_HEADS = 4

V7X_VMEM_BYTES = 64 * 1024 * 1024
VMEM_LIMIT = V7X_VMEM_BYTES - 4 * 1024 * 1024
LANES = 128
ROW_SPLIT = 512
LOG2_E = math.log2(math.e)
RET_STEP_CHUNKS = 8
SSD_STEP_CHUNKS = 2

_LOG_G = np.log(1.0 - np.exp2(-5.0 - np.arange(RET_HEADS, dtype=np.float32))).astype(np.float32)
_RET_CHUNK_DECAY = np.exp(np.float32(CHUNK) * _LOG_G).astype(np.float32)


def _params(*sem):
    return pltpu.CompilerParams(dimension_semantics=sem, vmem_limit_bytes=VMEM_LIMIT)


def _rms(x, w):
    ms = jnp.mean(x * x, axis=-1, keepdims=True)
    return x * lax.rsqrt(ms + EPS) * w


def _sigmoid(x):
    return 0.5 + 0.5 * jnp.tanh(0.5 * x)


def _silu(x):
    h = 0.5 * x
    return h + h * jnp.tanh(h)


def _dot(a, b):
    return jnp.dot(a, b, preferred_element_type=F32)


def _dot_nt(a, b):
    return lax.dot_general(a, b, (((1,), (1,)), ((), ())), preferred_element_type=F32)


def _row_blocks(n_rows):
    step = min(ROW_SPLIT, n_rows)
    return [slice(r, r + step) for r in range(0, n_rows, step)]


def _prenorm_body(x_ref, nw_ref, o_ref):
    o_ref[...] = _rms(x_ref[...], nw_ref[...]).astype(o_ref.dtype)


def _prenorm(x, nw, tm):
    m, k = x.shape
    return pl.pallas_call(
        _prenorm_body,
        grid=(m // tm,),
        in_specs=[pl.BlockSpec((tm, k), lambda i: (i, 0)), pl.BlockSpec((1, k), lambda i: (0, 0))],
        out_specs=pl.BlockSpec((tm, k), lambda i: (i, 0)),
        out_shape=jax.ShapeDtypeStruct((m, k), BF16),
        compiler_params=_params("parallel"),
        name="prenorm",
    )(x, nw)


def _matmul_nt_body(a_ref, wt_ref, o_ref):
    wb = wt_ref[...].astype(BF16)
    for rows in _row_blocks(a_ref.shape[0]):
        o_ref[rows, :] = _dot_nt(a_ref[rows, :], wb).astype(o_ref.dtype)


def _in_proj_main(a, w3, layer, n_cols, shift, tm, tn):
    m, k = a.shape
    nb = n_cols // tn
    assert n_cols % tn == 0 and m % tm == 0
    return pl.pallas_call(
        _matmul_nt_body,
        grid=(m // tm, nb),
        in_specs=[
            pl.BlockSpec((tm, k), lambda i, j: (i, 0)),
            pl.BlockSpec((None, tn, k), lambda i, j: (layer, j, 0)),
        ],
        out_specs=pl.BlockSpec((tm, tn), lambda i, j: (i, (j + shift) % nb)),
        out_shape=jax.ShapeDtypeStruct((m, n_cols), BF16),
        compiler_params=_params("parallel", "arbitrary"),
        name="in_proj_main",
    )(a, w3)


def _in_proj_tail_body(a_ref, wt_ref, g_ref, dt_ref):
    tn = wt_ref.shape[0]
    for rows in _row_blocks(a_ref.shape[0]):
        res = _dot_nt(a_ref[rows, :], wt_ref[...])
        g_ref[rows, :] = res.astype(g_ref.dtype)
        dt_ref[rows, :] = res[:, tn - LANES:]


def _in_proj_tail(a, wt_tail, tm, tn):
    m, k = a.shape
    n = wt_tail.shape[0]
    assert n % tn == 0 and tn % LANES == 0
    return pl.pallas_call(
        _in_proj_tail_body,
        grid=(m // tm, n // tn),
        in_specs=[
            pl.BlockSpec((tm, k), lambda i, j: (i, 0)),
            pl.BlockSpec((tn, k), lambda i, j: (j, 0)),
        ],
        out_specs=[
            pl.BlockSpec((tm, tn), lambda i, j: (i, j)),
            pl.BlockSpec((tm, LANES), lambda i, j: (i, 0)),
        ],
        out_shape=[jax.ShapeDtypeStruct((m, n), BF16), jax.ShapeDtypeStruct((m, LANES), F32)],
        compiler_params=_params("parallel", "arbitrary"),
        name="in_proj_tail",
    )(a, wt_tail)


def _norm_matmul_body(x_ref, nw_ref, w_ref, o_ref, xn_ref):
    @pl.when(pl.program_id(1) == 0)
    def _():
        xn_ref[...] = _rms(x_ref[...], nw_ref[...]).astype(BF16)

    o_ref[...] = _dot(xn_ref[...], w_ref[...].astype(BF16)).astype(o_ref.dtype)


def _norm_matmul(x, nw, w, out_dtype, tm, tn, name):
    m, k = x.shape
    n = w.shape[1]
    return pl.pallas_call(
        _norm_matmul_body,
        grid=(m // tm, n // tn),
        in_specs=[
            pl.BlockSpec((tm, k), lambda i, j: (i, 0)),
            pl.BlockSpec((1, k), lambda i, j: (0, 0)),
            pl.BlockSpec((k, tn), lambda i, j: (0, j)),
        ],
        out_specs=pl.BlockSpec((tm, tn), lambda i, j: (i, j)),
        out_shape=jax.ShapeDtypeStruct((m, n), out_dtype),
        scratch_shapes=[pltpu.VMEM((tm, k), BF16)],
        compiler_params=_params("parallel", "arbitrary"),
        name=name,
    )(x, nw, w)


def _retention_tables():
    idx = np.arange(CHUNK, dtype=np.float32)
    rel = idx[:, None] - idx[None, :]
    lg = _LOG_G[:, None, None]
    scale = np.float32(RET_DK ** -0.5)
    decay = np.where(rel[None] >= 0, np.exp(np.maximum(rel, 0.0)[None] * lg), 0.0) * scale
    xi = np.exp((idx + 1.0)[None, :, None] * lg) * scale
    zeta = np.exp((CHUNK - 1.0 - idx)[None, :, None] * lg)
    wide = (RET_HEADS, CHUNK, RET_DK)
    return (decay.astype(np.float32), np.broadcast_to(xi, wide).astype(np.float32),
            np.broadcast_to(zeta, wide).astype(np.float32))


def _retention_body(q_ref, k_ref, v_ref, g_ref, cos_ref, sin_ref, dec_ref, xi_ref, zeta_ref,
                    nw_ref, o_ref, state_ref):
    @pl.when(pl.program_id(1) == 0)
    def _():
        state_ref[...] = jnp.zeros_like(state_ref)

    for cc, h in itertools.product(range(RET_STEP_CHUNKS), range(RET_HEADS)):
        rows = slice(cc * CHUNK, (cc + 1) * CHUNK)
        cos = cos_ref[rows, :]
        sin = sin_ref[rows, :]
        q = q_ref[rows, h * RET_DK:(h + 1) * RET_DK].astype(F32)
        k = k_ref[rows, h * RET_DK:(h + 1) * RET_DK].astype(F32)
        v = v_ref[rows, h * RET_DV:(h + 1) * RET_DV]
        qr = q * cos + pltpu.roll(q, RET_DK // 2, 1) * sin
        kr = k * cos + pltpu.roll(k, RET_DK // 2, 1) * sin
        s = _dot_nt(qr.astype(BF16), kr.astype(BF16)) * dec_ref[h]
        st = state_ref[h]
        lhs = jnp.concatenate([s.astype(BF16), (qr * xi_ref[h]).astype(BF16)], axis=1)
        rhs = jnp.concatenate([v, st.astype(BF16)], axis=0)
        y = _dot(lhs, rhs)
        kz_t = (kr * zeta_ref[h]).T.astype(BF16)
        state_ref[h] = float(_RET_CHUNK_DECAY[h]) * st + _dot(kz_t, v)
        mu = jnp.mean(y, axis=-1, keepdims=True)
        yc = y - mu
        var = jnp.mean(yc * yc, axis=-1, keepdims=True)
        yn = yc * lax.rsqrt(var + EPS)
        hsl = slice(h * RET_DV, (h + 1) * RET_DV)
        o_ref[rows, hsl] = (yn * nw_ref[:, hsl] * _silu(g_ref[rows, hsl].astype(F32))).astype(BF16)


def _retention(proj, col_off, cos2, sin2, nw, bsz, seq):
    rows = RET_STEP_CHUNKS * CHUNK
    nc = seq // rows
    qk = RET_HEADS * RET_DK
    vd = RET_HEADS * RET_DV
    assert vd == 2 * qk and col_off % vd == 0 and seq % rows == 0
    oq = col_off // qk
    ov = col_off // vd
    decay, xi, zeta = _retention_tables()
    table = lambda n: pl.BlockSpec((RET_HEADS, CHUNK, n), lambda b, c: (0, 0, 0))
    return pl.pallas_call(
        _retention_body,
        grid=(bsz, nc),
        in_specs=[
            pl.BlockSpec((rows, qk), lambda b, c: (b * nc + c, oq)),
            pl.BlockSpec((rows, qk), lambda b, c: (b * nc + c, oq + 1)),
            pl.BlockSpec((rows, vd), lambda b, c: (b * nc + c, ov + 1)),
            pl.BlockSpec((rows, vd), lambda b, c: (b * nc + c, ov + 2)),
            pl.BlockSpec((rows, RET_DK), lambda b, c: (c, 0)),
            pl.BlockSpec((rows, RET_DK), lambda b, c: (c, 0)),
            table(CHUNK), table(RET_DK), table(RET_DK),
            pl.BlockSpec((1, vd), lambda b, c: (0, 0)),
        ],
        out_specs=pl.BlockSpec((rows, vd), lambda b, c: (b * nc + c, 0)),
        out_shape=jax.ShapeDtypeStruct((bsz * seq, vd), BF16),
        scratch_shapes=[pltpu.VMEM((RET_HEADS, RET_DK, RET_DV), F32)],
        compiler_params=_params("parallel", "arbitrary"),
        name="retention",
    )(proj, proj, proj, proj, cos2, sin2, decay, xi, zeta, nw)


def _ssd_body(z_ref, xbc_ref, dt_ref, cw_ref, cb_ref, dtb_ref, alog_ref, dsk_ref, nw_ref,
              o_ref, ext_ref, xc_ref, prev_ref, *, inner, heads_per_group):
    c = CHUNK
    n_sub = z_ref.shape[0] // c
    conv_dim = xbc_ref.shape[1]
    gw = heads_per_group * SSD_HEAD_DIM
    first = pl.program_id(1) == 0

    @pl.when(first)
    def _():
        ext_ref[0:c, :] = jnp.zeros((c, conv_dim), BF16)
        prev_ref[...] = jnp.zeros_like(prev_ref)

    ext_ref[c:2 * c, :] = xbc_ref[0:c, :]

    srow = lax.broadcasted_iota(jnp.int32, (c, 2 * c), 0)
    scol = lax.broadcasted_iota(jnp.int32, (c, 2 * c), 1)
    shift_mat = jnp.concatenate(
        [jnp.where(scol - srow == c - (SSD_CONV - 1 - k), 1.0, 0.0).astype(BF16)
         for k in range(SSD_CONV - 1)], axis=0)
    row = lax.broadcasted_iota(jnp.int32, (c, c), 0)
    col = lax.broadcasted_iota(jnp.int32, (c, c), 1)
    tri = row >= col
    lane_lo = lax.broadcasted_iota(jnp.int32, (c, LANES), 1) < SSD_HEAD_DIM
    neg_inf = jnp.float32(-jnp.inf)
    a = -jnp.exp(alog_ref[...])
    cblk = 512

    for cc in range(n_sub):
        rows = slice(cc * c, (cc + 1) * c)
        xcc_ref = xc_ref.at[cc]
        for j in range(conv_dim // cblk):
            sl = slice(j * cblk, (j + 1) * cblk)
            hist = ext_ref[:, sl] if cc == 0 else xbc_ref[(cc - 1) * c:(cc + 1) * c, sl]
            shifted = _dot(shift_mat, hist)
            cwh = 0.5 * cw_ref[:, sl]
            acc = 0.5 * cb_ref[:, sl] + cwh[SSD_CONV - 1:SSD_CONV, :] * xbc_ref[rows, sl].astype(F32)
            for k in range(SSD_CONV - 1):
                acc = acc + cwh[k:k + 1, :] * shifted[k * c:(k + 1) * c, :]
            xcc_ref[:, sl] = acc + acc * jnp.tanh(acc)

        dtr = dt_ref[rows, :] + dtb_ref[...]
        dt = jnp.maximum(dtr, 0.0) + jnp.log1p(jnp.exp(-jnp.abs(dtr)))
        da = dt * a
        a_cs = jnp.dot(tri.astype(F32), da, precision=lax.Precision.HIGHEST,
                       preferred_element_type=F32)
        a_cs = a_cs * LOG2_E
        a_last = a_cs[c - 1:c, :]
        w1 = jnp.exp2(a_last - a_cs) * dt
        src_t = (a_cs - jnp.log2(dt)).T
        w1_t = w1.T

        for g in range(SSD_GROUPS):
            b_g = xcc_ref[:, inner + g * SSD_STATE: inner + (g + 1) * SSD_STATE]
            c_g = xcc_ref[:, inner + (SSD_GROUPS + g) * SSD_STATE:
                          inner + (SSD_GROUPS + g + 1) * SSD_STATE]
            cb = _dot_nt(c_g.astype(BF16), b_g.astype(BF16))
            b_gt = b_g.T
            y_pairs = []
            for pr in range(heads_per_group // 2):
                h0 = g * heads_per_group + 2 * pr
                psl = slice(h0 * SSD_HEAD_DIM, (h0 + 2) * SSD_HEAD_DIM)
                xs_b = xcc_ref[:, psl].astype(BF16)
                prev = prev_ref[:, psl]
                rhs = jnp.concatenate([xs_b, prev.astype(BF16)], axis=0)
                lhs, bws, cds = [], [], []
                for h in (h0, h0 + 1):
                    colb = jnp.broadcast_to(a_cs[:, h:h + 1], (c, c))
                    rowb = jnp.broadcast_to(src_t[h:h + 1, :], (c, c))
                    m = cb * jnp.exp2(jnp.where(tri, colb - rowb, neg_inf))
                    e = jnp.exp2(colb)
                    lhs.append(jnp.concatenate([m.astype(BF16), (e * c_g).astype(BF16)], axis=1))
                    bws.append((b_gt * jnp.broadcast_to(w1_t[h:h + 1, :], (c, c))).astype(BF16))
                    cds.append(e[c - 1:c, :])
                ys = _dot(jnp.concatenate(lhs, axis=0), rhs)
                sts = _dot(jnp.concatenate(bws, axis=0), xs_b)
                y_pairs.append(jnp.where(lane_lo, ys[0:c, :], ys[c:2 * c, :]))
                cd = jnp.where(lane_lo[0:1, :], cds[0], cds[1])
                prev_ref[:, psl] = cd * prev + jnp.where(lane_lo, sts[0:c, :], sts[c:2 * c, :])
            gsl = slice(g * gw, (g + 1) * gw)
            y = jnp.concatenate(y_pairs, axis=1) + dsk_ref[:, gsl] * xcc_ref[:, gsl]
            gy = y * _silu(z_ref[rows, gsl].astype(F32))
            ms = jnp.mean(gy * gy, axis=-1, keepdims=True)
            o_ref[rows, gsl] = (gy * lax.rsqrt(ms + EPS) * nw_ref[:, gsl]).astype(BF16)

    ext_ref[0:c, :] = xbc_ref[(n_sub - 1) * c:n_sub * c, :]


def _ssd(proj, z_off, xbc_off, inner, conv_dim, dt_raw, conv_w, conv_b, dtb, alog, dskip, nw, bsz, seq):
    rows = SSD_STEP_CHUNKS * CHUNK
    nc = seq // rows
    n_heads = inner // SSD_HEAD_DIM
    heads_per_group = n_heads // SSD_GROUPS
    assert conv_dim == inner + 2 * SSD_GROUPS * SSD_STATE and SSD_STATE == CHUNK and seq % rows == 0
    assert n_heads <= LANES and heads_per_group % 2 == 0 and 2 * SSD_HEAD_DIM == LANES
    assert z_off % inner == 0 and xbc_off % conv_dim == 0
    zb = z_off // inner
    xb = xbc_off // conv_dim
    body = functools.partial(_ssd_body, inner=inner, heads_per_group=heads_per_group)
    row = lambda b, c: (b * nc + c, 0)
    fixed = lambda b, c: (0, 0)
    return pl.pallas_call(
        body,
        grid=(bsz, nc),
        in_specs=[
            pl.BlockSpec((rows, inner), lambda b, c: (b * nc + c, zb)),
            pl.BlockSpec((rows, conv_dim), lambda b, c: (b * nc + c, xb)),
            pl.BlockSpec((rows, LANES), row),
            pl.BlockSpec((SSD_CONV, conv_dim), fixed),
            pl.BlockSpec((1, conv_dim), fixed),
            pl.BlockSpec((1, LANES), fixed),
            pl.BlockSpec((1, LANES), fixed),
            pl.BlockSpec((1, inner), fixed),
            pl.BlockSpec((1, inner), fixed),
        ],
        out_specs=pl.BlockSpec((rows, inner), row),
        out_shape=jax.ShapeDtypeStruct((bsz * seq, inner), BF16),
        scratch_shapes=[
            pltpu.VMEM((2 * CHUNK, conv_dim), BF16),
            pltpu.VMEM((SSD_STEP_CHUNKS, CHUNK, conv_dim), F32),
            pltpu.VMEM((SSD_STATE, inner), F32),
        ],
        compiler_params=_params("parallel", "arbitrary"),
        name="ssd",
    )(proj, proj, dt_raw, conv_w, conv_b, dtb, alog, dskip, nw)


def _merge_body(yr_ref, ys_ref, wr_ref, ws_ref, ga_ref, gb_ref, o_ref):
    for rows in _row_blocks(yr_ref.shape[0]):
        br = _dot(yr_ref[rows, :], wr_ref[...])
        bs = _dot(ys_ref[rows, :], ws_ref[...])
        ga = _sigmoid(ga_ref[rows, :].astype(F32))
        gb = _sigmoid(gb_ref[rows, :].astype(F32))
        o_ref[rows, :] = (ga * br + gb * bs).astype(o_ref.dtype)


def _merge(y_ret, y_ssd, w_ret, w_ssd, gates, tm, tn):
    m, kr = y_ret.shape
    ks = y_ssd.shape[1]
    n = w_ret.shape[1]
    nb = n // tn
    return pl.pallas_call(
        _merge_body,
        grid=(m // tm, nb),
        in_specs=[
            pl.BlockSpec((tm, kr), lambda i, j: (i, 0)),
            pl.BlockSpec((tm, ks), lambda i, j: (i, 0)),
            pl.BlockSpec((kr, tn), lambda i, j: (0, j)),
            pl.BlockSpec((ks, tn), lambda i, j: (0, j)),
            pl.BlockSpec((tm, tn), lambda i, j: (i, j)),
            pl.BlockSpec((tm, tn), lambda i, j: (i, j + nb)),
        ],
        out_specs=pl.BlockSpec((tm, tn), lambda i, j: (i, j)),
        out_shape=jax.ShapeDtypeStruct((m, n), BF16),
        compiler_params=_params("parallel", "arbitrary"),
        name="merge",
    )(y_ret, y_ssd, w_ret, w_ssd, gates, gates)


def _mix_xattn_body(m_ref, x_ref, k_ref, v_ref, wmix_ref, nw_ref, wq_ref, wo_ref, o_ref, xn_ref):
    d = x_ref.shape[1]
    hd = d // XA_HEADS
    scale = hd ** -0.5
    o_ref[...] = x_ref[...] + _dot(m_ref[...], wmix_ref[...])
    xn_ref[...] = _rms(o_ref[...], nw_ref[...]).astype(BF16)
    for h in range(XA_HEADS):
        sl = slice(h * hd, (h + 1) * hd)
        q = _dot(xn_ref[...], wq_ref[:, sl]).astype(BF16)
        s = _dot_nt(q, k_ref[:, sl]) * scale
        s = s - jnp.max(s, axis=-1, keepdims=True)
        p = jnp.exp(s)
        p = p / jnp.sum(p, axis=-1, keepdims=True)
        o_h = _dot(p.astype(BF16), v_ref[:, sl]).astype(BF16)
        o_ref[...] += _dot(o_h, wo_ref[sl, :])


def _mix_xattn(merged, x, kv, w_mix, nw, w_q, w_o, bsz, seq, mem_len, tq):
    d = x.shape[1]
    nq = seq // tq
    tile = lambda b, i: (b * nq + i, 0)
    fixed = lambda b, i: (0, 0)
    resident = pl.Buffered(1)
    return pl.pallas_call(
        _mix_xattn_body,
        grid=(bsz, nq),
        in_specs=[
            pl.BlockSpec((tq, d), tile),
            pl.BlockSpec((tq, d), tile),
            pl.BlockSpec((mem_len, d), lambda b, i: (b, 0)),
            pl.BlockSpec((mem_len, d), lambda b, i: (b, 1)),
            pl.BlockSpec((d, d), fixed, pipeline_mode=resident),
            pl.BlockSpec((1, d), fixed),
            pl.BlockSpec((d, d), fixed, pipeline_mode=resident),
            pl.BlockSpec((d, d), fixed, pipeline_mode=resident),
        ],
        out_specs=pl.BlockSpec((tq, d), tile),
        out_shape=jax.ShapeDtypeStruct((bsz * seq, d), F32),
        scratch_shapes=[pltpu.VMEM((tq, d), BF16)],
        compiler_params=_params("parallel", "arbitrary"),
        name="mix_xattn",
    )(merged, x, kv, kv, w_mix, nw, w_q, w_o)


def _mlp_body(h_ref, nw_ref, w1_ref, w2_ref, fw_ref, o_ref, xn_ref):
    f = pl.program_id(1)
    blocks = _row_blocks(h_ref.shape[0])

    @pl.when(f == 0)
    def _():
        for rows in blocks:
            hh = h_ref[rows, :]
            xn_ref[rows, :] = _rms(hh, nw_ref[...]).astype(BF16)
            o_ref[rows, :] = hh

    w1b = w1_ref[...].astype(BF16)
    w2b = w2_ref[...].astype(BF16)
    for rows in blocks:
        u = _dot(xn_ref[rows, :], w1b)
        u = jnp.square(jnp.maximum(u, 0.0)).astype(BF16)
        o_ref[rows, :] += _dot(u, w2b)

    @pl.when(f == pl.num_programs(1) - 1)
    def _():
        for rows in blocks:
            o_ref[rows, :] = _rms(o_ref[rows, :], fw_ref[...])


def _mlp(h, nw, w1, w2, fw, tm, tf):
    m, d = h.shape
    dff = w1.shape[1]
    once = pl.Buffered(1)
    return pl.pallas_call(
        _mlp_body,
        grid=(m // tm, dff // tf),
        in_specs=[
            pl.BlockSpec((tm, d), lambda i, f: (i, 0)),
            pl.BlockSpec((1, d), lambda i, f: (0, 0)),
            pl.BlockSpec((d, tf), lambda i, f: (0, f)),
            pl.BlockSpec((tf, d), lambda i, f: (f, 0)),
            pl.BlockSpec((1, d), lambda i, f: (0, 0)),
        ],
        out_specs=pl.BlockSpec((tm, d), lambda i, f: (i, 0)),
        out_shape=jax.ShapeDtypeStruct((m, d), F32),
        scratch_shapes=[pltpu.VMEM((tm, d), BF16)],
        compiler_params=_params("parallel", "arbitrary"),
        name="mlp",
    )(h, nw, w1, w2, fw)


def _rope_tables(seq, dk):
    half = dk // 2
    pos = jnp.arange(seq, dtype=F32)
    inv = jnp.exp(-math.log(ROPE_BASE) * jnp.arange(half, dtype=F32) / half)
    ang = pos[:, None] * inv[None, :]
    cos, sin = jnp.cos(ang), jnp.sin(ang)
    return jnp.concatenate([cos, cos], axis=1), jnp.concatenate([-sin, sin], axis=1)


def kernel(x, mem, norm_mix_w, w_in, conv_w, conv_b, dt_bias, a_log, d_skip, ret_norm_w, ssd_norm_w, w_ret_out, w_ssd_out, w_mix_out, norm_xa_w, mem_norm_w, w_xq, w_xkv, w_xo, norm_ff_w, w_ff1, w_ff2, final_norm_w):
    bsz, seq, d = x.shape
    mem_len = mem.shape[1]
    depth = w_in.shape[0]
    n_tok = bsz * seq
    ret_qk = RET_HEADS * RET_DK
    ret_v = RET_HEADS * RET_DV
    inner = w_ssd_out.shape[1]
    conv_dim = conv_w.shape[2]
    n_heads = inner // SSD_HEAD_DIM
    ret_w = 2 * ret_qk + 2 * ret_v
    off_dt = ret_w + inner + conv_dim
    off_gate = off_dt + n_heads
    assert w_in.shape[2] == off_gate + 2 * d and depth == 1

    tn_main = 1024
    assert conv_dim % tn_main == 0 and ret_w % tn_main == 0 and inner % tn_main == 0
    shift = conv_dim // tn_main
    xbc_off, ret_off, z_off = 0, conv_dim, conv_dim + ret_w

    cos2, sin2 = _rope_tables(seq, RET_DK)
    row = lambda v: v.reshape(1, -1)
    pad_heads = lambda v: jnp.pad(v.reshape(1, -1), ((0, 0), (0, LANES - n_heads)))

    w_in_t = jnp.swapaxes(w_in, 1, 2)

    h = x.reshape(n_tok, d)
    for l in range(depth):
        xn = _prenorm(h, row(norm_mix_w[l]), 512)
        proj = _in_proj_main(xn, w_in_t, l, off_dt, shift, 2048, tn_main)
        tail_rows = lax.optimization_barrier((w_in_t[l, off_gate:], w_in_t[l, off_dt:off_gate]))
        wt_tail = jnp.concatenate(
            [tail_rows[0].astype(BF16), tail_rows[1].astype(BF16),
             jnp.zeros((LANES - n_heads, d), BF16)], axis=0)
        gates, dt_raw = _in_proj_tail(xn, wt_tail, 2048, wt_tail.shape[0] // 3)
        y_ret = _retention(proj, ret_off, cos2, sin2, row(ret_norm_w[l]), bsz, seq)
        y_ssd = _ssd(proj, z_off, xbc_off, inner, conv_dim, dt_raw, conv_w[l], row(conv_b[l]),
                     pad_heads(dt_bias[l]), pad_heads(a_log[l]),
                     row(jnp.repeat(d_skip[l], SSD_HEAD_DIM)), row(ssd_norm_w[l]), bsz, seq)
        merged = _merge(y_ret, y_ssd, w_ret_out[l].astype(BF16), w_ssd_out[l].astype(BF16),
                        gates, 1024, 512)
        kv = _norm_matmul(mem.reshape(bsz * mem_len, d), row(mem_norm_w[l]), w_xkv[l],
                          BF16, bsz * mem_len, 1024, "xa_kv")
        h = _mix_xattn(merged, h, kv, w_mix_out[l].astype(BF16), row(norm_xa_w[l]),
                       w_xq[l].astype(BF16), w_xo[l].astype(BF16), bsz, seq, mem_len, 512)
        h = _mlp(h, row(norm_ff_w[l]), w_ff1[l], w_ff2[l], row(final_norm_w), 1024, 512)
    return h.reshape(bsz, seq, d)
```

```python
import functools
import itertools
import math

import numpy as np
import jax
import jax.numpy as jnp
from jax import lax
from jax.experimental import pallas as pl
from jax.experimental.pallas import tpu as pltpu

F32 = jnp.float32
BF16 = jnp.bfloat16
EPS = 1e-6

RET_HEADS = 8
RET_DK = 128
RET_DV = 256
CHUNK = 128
ROPE_BASE = 10000.0
SSD_HEAD_DIM = 64
SSD_GROUPS = 8
SSD_STATE = 128
SSD_CONV = 4
XA_HEADS = 4

V7X_VMEM_BYTES = 64 * 1024 * 1024
VMEM_LIMIT = V7X_VMEM_BYTES - 4 * 1024 * 1024
LANES = 128
ROW_SPLIT = 512
LOG2_E = math.log2(math.e)
RET_STEP_CHUNKS = 8
SSD_STEP_CHUNKS = 2

_LOG_G = np.log(1.0 - np.exp2(-5.0 - np.arange(RET_HEADS, dtype=np.float32))).astype(np.float32)
_RET_CHUNK_DECAY = np.exp(np.float32(CHUNK) * _LOG_G).astype(np.float32)


def _params(*sem):
    return pltpu.CompilerParams(dimension_semantics=sem, vmem_limit_bytes=VMEM_LIMIT)


def _rms(x, w):
    ms = jnp.mean(x * x, axis=-1, keepdims=True)
    return x * lax.rsqrt(ms + EPS) * w


def _sigmoid(x):
    return 0.5 + 0.5 * jnp.tanh(0.5 * x)


def _silu(x):
    h = 0.5 * x
    return h + h * jnp.tanh(h)


def _dot(a, b):
    return jnp.dot(a, b, preferred_element_type=F32)


def _dot_nt(a, b):
    return lax.dot_general(a, b, (((1,), (1,)), ((), ())), preferred_element_type=F32)


def _row_blocks(n_rows):
    step = min(ROW_SPLIT, n_rows)
    return [slice(r, r + step) for r in range(0, n_rows, step)]


def _prenorm_body(x_ref, nw_ref, o_ref):
    o_ref[...] = _rms(x_ref[...], nw_ref[...]).astype(o_ref.dtype)


def _prenorm(x, nw, tm):
    m, k = x.shape
    return pl.pallas_call(
        _prenorm_body,
        grid=(m // tm,),
        in_specs=[pl.BlockSpec((tm, k), lambda i: (i, 0)), pl.BlockSpec((1, k), lambda i: (0, 0))],
        out_specs=pl.BlockSpec((tm, k), lambda i: (i, 0)),
        out_shape=jax.ShapeDtypeStruct((m, k), BF16),
        compiler_params=_params("parallel"),
        name="prenorm",
    )(x, nw)


def _matmul_nt_body(a_ref, wt_ref, o_ref):
    wb = wt_ref[...].astype(BF16)
    for rows in _row_blocks(a_ref.shape[0]):
        o_ref[rows, :] = _dot_nt(a_ref[rows, :], wb).astype(o_ref.dtype)


def _in_proj_main(a, w3, layer, n_cols, shift, tm, tn):
    m, k = a.shape
    nb = n_cols // tn
    assert n_cols % tn == 0 and m % tm == 0
    return pl.pallas_call(
        _matmul_nt_body,
        grid=(m // tm, nb),
        in_specs=[
            pl.BlockSpec((tm, k), lambda i, j: (i, 0)),
            pl.BlockSpec((None, tn, k), lambda i, j: (layer, j, 0)),
        ],
        out_specs=pl.BlockSpec((tm, tn), lambda i, j: (i, (j + shift) % nb)),
        out_shape=jax.ShapeDtypeStruct((m, n_cols), BF16),
        compiler_params=_params("parallel", "arbitrary"),
        name="in_proj_main",
    )(a, w3)


def _in_proj_tail_body(a_ref, wt_ref, g_ref, dt_ref):
    tn = wt_ref.shape[0]
    for rows in _row_blocks(a_ref.shape[0]):
        res = _dot_nt(a_ref[rows, :], wt_ref[...])
        g_ref[rows, :] = res.astype(g_ref.dtype)
        dt_ref[rows, :] = res[:, tn - LANES:]


def _in_proj_tail(a, wt_tail, tm, tn):
    m, k = a.shape
    n = wt_tail.shape[0]
    assert n % tn == 0 and tn % LANES == 0
    return pl.pallas_call(
        _in_proj_tail_body,
        grid=(m // tm, n // tn),
        in_specs=[
            pl.BlockSpec((tm, k), lambda i, j: (i, 0)),
            pl.BlockSpec((tn, k), lambda i, j: (j, 0)),
        ],
        out_specs=[
            pl.BlockSpec((tm, tn), lambda i, j: (i, j)),
            pl.BlockSpec((tm, LANES), lambda i, j: (i, 0)),
        ],
        out_shape=[jax.ShapeDtypeStruct((m, n), BF16), jax.ShapeDtypeStruct((m, LANES), F32)],
        compiler_params=_params("parallel", "arbitrary"),
        name="in_proj_tail",
    )(a, wt_tail)


def _norm_matmul_body(x_ref, nw_ref, w_ref, o_ref, xn_ref):
    @pl.when(pl.program_id(1) == 0)
    def _():
        xn_ref[...] = _rms(x_ref[...], nw_ref[...]).astype(BF16)

    o_ref[...] = _dot(xn_ref[...], w_ref[...].astype(BF16)).astype(o_ref.dtype)


def _norm_matmul(x, nw, w, out_dtype, tm, tn, name):
    m, k = x.shape
    n = w.shape[1]
    return pl.pallas_call(
        _norm_matmul_body,
        grid=(m // tm, n // tn),
        in_specs=[
            pl.BlockSpec((tm, k), lambda i, j: (i, 0)),
            pl.BlockSpec((1, k), lambda i, j: (0, 0)),
            pl.BlockSpec((k, tn), lambda i, j: (0, j)),
        ],
        out_specs=pl.BlockSpec((tm, tn), lambda i, j: (i, j)),
        out_shape=jax.ShapeDtypeStruct((m, n), out_dtype),
        scratch_shapes=[pltpu.VMEM((tm, k), BF16)],
        compiler_params=_params("parallel", "arbitrary"),
        name=name,
    )(x, nw, w)


def _retention_tables():
    idx = np.arange(CHUNK, dtype=np.float32)
    rel = idx[:, None] - idx[None, :]
    lg = _LOG_G[:, None, None]
    scale = np.float32(RET_DK ** -0.5)
    decay = np.where(rel[None] >= 0, np.exp(np.maximum(rel, 0.0)[None] * lg), 0.0) * scale
    xi = np.exp((idx + 1.0)[None, :, None] * lg) * scale
    zeta = np.exp((CHUNK - 1.0 - idx)[None, :, None] * lg)
    wide = (RET_HEADS, CHUNK, RET_DK)
    return (decay.astype(np.float32), np.broadcast_to(xi, wide).astype(np.float32),
            np.broadcast_to(zeta, wide).astype(np.float32))


def _retention_body(q_ref, k_ref, v_ref, g_ref, cos_ref, sin_ref, dec_ref, xi_ref, zeta_ref,
                    nw_ref, o_ref, state_ref):
    @pl.when(pl.program_id(1) == 0)
    def _():
        state_ref[...] = jnp.zeros_like(state_ref)

    for cc, h in itertools.product(range(RET_STEP_CHUNKS), range(RET_HEADS)):
        rows = slice(cc * CHUNK, (cc + 1) * CHUNK)
        cos = cos_ref[rows, :]
        sin = sin_ref[rows, :]
        q = q_ref[rows, h * RET_DK:(h + 1) * RET_DK].astype(F32)
        k = k_ref[rows, h * RET_DK:(h + 1) * RET_DK].astype(F32)
        v = v_ref[rows, h * RET_DV:(h + 1) * RET_DV]
        qr = q * cos + pltpu.roll(q, RET_DK // 2, 1) * sin
        kr = k * cos + pltpu.roll(k, RET_DK // 2, 1) * sin
        s = _dot_nt(qr.astype(BF16), kr.astype(BF16)) * dec_ref[h]
        st = state_ref[h]
        lhs = jnp.concatenate([s.astype(BF16), (qr * xi_ref[h]).astype(BF16)], axis=1)
        rhs = jnp.concatenate([v, st.astype(BF16)], axis=0)
        y = _dot(lhs, rhs)
        kz_t = (kr * zeta_ref[h]).T.astype(BF16)
        state_ref[h] = float(_RET_CHUNK_DECAY[h]) * st + _dot(kz_t, v)
        mu = jnp.mean(y, axis=-1, keepdims=True)
        yc = y - mu
        var = jnp.mean(yc * yc, axis=-1, keepdims=True)
        yn = yc * lax.rsqrt(var + EPS)
        hsl = slice(h * RET_DV, (h + 1) * RET_DV)
        o_ref[rows, hsl] = (yn * nw_ref[:, hsl] * _silu(g_ref[rows, hsl].astype(F32))).astype(BF16)


def _retention(proj, col_off, cos2, sin2, nw, bsz, seq):
    rows = RET_STEP_CHUNKS * CHUNK
    nc = seq // rows
    qk = RET_HEADS * RET_DK
    vd = RET_HEADS * RET_DV
    assert vd == 2 * qk and col_off % vd == 0 and seq % rows == 0
    oq = col_off // qk
    ov = col_off // vd
    decay, xi, zeta = _retention_tables()
    table = lambda n: pl.BlockSpec((RET_HEADS, CHUNK, n), lambda b, c: (0, 0, 0))
    return pl.pallas_call(
        _retention_body,
        grid=(bsz, nc),
        in_specs=[
            pl.BlockSpec((rows, qk), lambda b, c: (b * nc + c, oq)),
            pl.BlockSpec((rows, qk), lambda b, c: (b * nc + c, oq + 1)),
            pl.BlockSpec((rows, vd), lambda b, c: (b * nc + c, ov + 1)),
            pl.BlockSpec((rows, vd), lambda b, c: (b * nc + c, ov + 2)),
            pl.BlockSpec((rows, RET_DK), lambda b, c: (c, 0)),
            pl.BlockSpec((rows, RET_DK), lambda b, c: (c, 0)),
            table(CHUNK), table(RET_DK), table(RET_DK),
            pl.BlockSpec((1, vd), lambda b, c: (0, 0)),
        ],
        out_specs=pl.BlockSpec((rows, vd), lambda b, c: (b * nc + c, 0)),
        out_shape=jax.ShapeDtypeStruct((bsz * seq, vd), BF16),
        scratch_shapes=[pltpu.VMEM((RET_HEADS, RET_DK, RET_DV), F32)],
        compiler_params=_params("parallel", "arbitrary"),
        name="retention",
    )(proj, proj, proj, proj, cos2, sin2, decay, xi, zeta, nw)


def _ssd_body(z_ref, xbc_ref, dt_ref, cw_ref, cb_ref, dtb_ref, alog_ref, dsk_ref, nw_ref,
              o_ref, ext_ref, xc_ref, prev_ref, *, inner, heads_per_group):
    c = CHUNK
    n_sub = z_ref.shape[0] // c
    conv_dim = xbc_ref.shape[1]
    gw = heads_per_group * SSD_HEAD_DIM
    first = pl.program_id(1) == 0

    @pl.when(first)
    def _():
        ext_ref[0:c, :] = jnp.zeros((c, conv_dim), BF16)
        prev_ref[...] = jnp.zeros_like(prev_ref)

    ext_ref[c:2 * c, :] = xbc_ref[0:c, :]

    srow = lax.broadcasted_iota(jnp.int32, (c, 2 * c), 0)
    scol = lax.broadcasted_iota(jnp.int32, (c, 2 * c), 1)
    shift_mat = jnp.concatenate(
        [jnp.where(scol - srow == c - (SSD_CONV - 1 - k), 1.0, 0.0).astype(BF16)
         for k in range(SSD_CONV - 1)], axis=0)
    row = lax.broadcasted_iota(jnp.int32, (c, c), 0)
    col = lax.broadcasted_iota(jnp.int32, (c, c), 1)
    tri = row >= col
    lane_lo = lax.broadcasted_iota(jnp.int32, (c, LANES), 1) < SSD_HEAD_DIM
    neg_inf = jnp.float32(-jnp.inf)
    a = -jnp.exp(alog_ref[...])
    cblk = 512

    for cc in range(n_sub):
        rows = slice(cc * c, (cc + 1) * c)
        xcc_ref = xc_ref.at[cc]
        for j in range(conv_dim // cblk):
            sl = slice(j * cblk, (j + 1) * cblk)
            hist = ext_ref[:, sl] if cc == 0 else xbc_ref[(cc - 1) * c:(cc + 1) * c, sl]
            shifted = _dot(shift_mat, hist)
            cwh = 0.5 * cw_ref[:, sl]
            acc = 0.5 * cb_ref[:, sl] + cwh[SSD_CONV - 1:SSD_CONV, :] * xbc_ref[rows, sl].astype(F32)
            for k in range(SSD_CONV - 1):
                acc = acc + cwh[k:k + 1, :] * shifted[k * c:(k + 1) * c, :]
            xcc_ref[:, sl] = acc + acc * jnp.tanh(acc)

        dtr = dt_ref[rows, :] + dtb_ref[...]
        dt = jnp.maximum(dtr, 0.0) + jnp.log1p(jnp.exp(-jnp.abs(dtr)))
        da = dt * a
        a_cs = jnp.dot(tri.astype(F32), da, precision=lax.Precision.HIGHEST,
                       preferred_element_type=F32)
        a_cs = a_cs * LOG2_E
        a_last = a_cs[c - 1:c, :]
        w1 = jnp.exp2(a_last - a_cs) * dt
        src_t = (a_cs - jnp.log2(dt)).T
        w1_t = w1.T

        for g in range(SSD_GROUPS):
            b_g = xcc_ref[:, inner + g * SSD_STATE: inner + (g + 1) * SSD_STATE]
            c_g = xcc_ref[:, inner + (SSD_GROUPS + g) * SSD_STATE:
                          inner + (SSD_GROUPS + g + 1) * SSD_STATE]
            cb = _dot_nt(c_g.astype(BF16), b_g.astype(BF16))
            b_gt = b_g.T
            y_pairs = []
            for pr in range(heads_per_group // 2):
                h0 = g * heads_per_group + 2 * pr
                psl = slice(h0 * SSD_HEAD_DIM, (h0 + 2) * SSD_HEAD_DIM)
                xs_b = xcc_ref[:, psl].astype(BF16)
                prev = prev_ref[:, psl]
                rhs = jnp.concatenate([xs_b, prev.astype(BF16)], axis=0)
                lhs, bws, cds = [], [], []
                for h in (h0, h0 + 1):
                    colb = jnp.broadcast_to(a_cs[:, h:h + 1], (c, c))
                    rowb = jnp.broadcast_to(src_t[h:h + 1, :], (c, c))
                    m = cb * jnp.exp2(jnp.where(tri, colb - rowb, neg_inf))
                    e = jnp.exp2(colb)
                    lhs.append(jnp.concatenate([m.astype(BF16), (e * c_g).astype(BF16)], axis=1))
                    bws.append((b_gt * jnp.broadcast_to(w1_t[h:h + 1, :], (c, c))).astype(BF16))
                    cds.append(e[c - 1:c, :])
                ys = _dot(jnp.concatenate(lhs, axis=0), rhs)
                sts = _dot(jnp.concatenate(bws, axis=0), xs_b)
                y_pairs.append(jnp.where(lane_lo, ys[0:c, :], ys[c:2 * c, :]))
                cd = jnp.where(lane_lo[0:1, :], cds[0], cds[1])
                prev_ref[:, psl] = cd * prev + jnp.where(lane_lo, sts[0:c, :], sts[c:2 * c, :])
            gsl = slice(g * gw, (g + 1) * gw)
            y = jnp.concatenate(y_pairs, axis=1) + dsk_ref[:, gsl] * xcc_ref[:, gsl]
            gy = y * _silu(z_ref[rows, gsl].astype(F32))
            ms = jnp.mean(gy * gy, axis=-1, keepdims=True)
            o_ref[rows, gsl] = (gy * lax.rsqrt(ms + EPS) * nw_ref[:, gsl]).astype(BF16)

    ext_ref[0:c, :] = xbc_ref[(n_sub - 1) * c:n_sub * c, :]


def _ssd(proj, z_off, xbc_off, inner, conv_dim, dt_raw, conv_w, conv_b, dtb, alog, dskip, nw, bsz, seq):
    rows = SSD_STEP_CHUNKS * CHUNK
    nc = seq // rows
    n_heads = inner // SSD_HEAD_DIM
    heads_per_group = n_heads // SSD_GROUPS
    assert conv_dim == inner + 2 * SSD_GROUPS * SSD_STATE and SSD_STATE == CHUNK and seq % rows == 0
    assert n_heads <= LANES and heads_per_group % 2 == 0 and 2 * SSD_HEAD_DIM == LANES
    assert z_off % inner == 0 and xbc_off % conv_dim == 0
    zb = z_off // inner
    xb = xbc_off // conv_dim
    body = functools.partial(_ssd_body, inner=inner, heads_per_group=heads_per_group)
    row = lambda b, c: (b * nc + c, 0)
    fixed = lambda b, c: (0, 0)
    return pl.pallas_call(
        body,
        grid=(bsz, nc),
        in_specs=[
            pl.BlockSpec((rows, inner), lambda b, c: (b * nc + c, zb)),
            pl.BlockSpec((rows, conv_dim), lambda b, c: (b * nc + c, xb)),
            pl.BlockSpec((rows, LANES), row),
            pl.BlockSpec((SSD_CONV, conv_dim), fixed),
            pl.BlockSpec((1, conv_dim), fixed),
            pl.BlockSpec((1, LANES), fixed),
            pl.BlockSpec((1, LANES), fixed),
            pl.BlockSpec((1, inner), fixed),
            pl.BlockSpec((1, inner), fixed),
        ],
        out_specs=pl.BlockSpec((rows, inner), row),
        out_shape=jax.ShapeDtypeStruct((bsz * seq, inner), BF16),
        scratch_shapes=[
            pltpu.VMEM((2 * CHUNK, conv_dim), BF16),
            pltpu.VMEM((SSD_STEP_CHUNKS, CHUNK, conv_dim), F32),
            pltpu.VMEM((SSD_STATE, inner), F32),
        ],
        compiler_params=_params("parallel", "arbitrary"),
        name="ssd",
    )(proj, proj, dt_raw, conv_w, conv_b, dtb, alog, dskip, nw)


def _merge_body(yr_ref, ys_ref, wr_ref, ws_ref, ga_ref, gb_ref, o_ref):
    for rows in _row_blocks(yr_ref.shape[0]):
        br = _dot(yr_ref[rows, :], wr_ref[...])
        bs = _dot(ys_ref[rows, :], ws_ref[...])
        ga = _sigmoid(ga_ref[rows, :].astype(F32))
        gb = _sigmoid(gb_ref[rows, :].astype(F32))
        o_ref[rows, :] = (ga * br + gb * bs).astype(o_ref.dtype)


def _merge(y_ret, y_ssd, w_ret, w_ssd, gates, tm, tn):
    m, kr = y_ret.shape
    ks = y_ssd.shape[1]
    n = w_ret.shape[1]
    nb = n // tn
    return pl.pallas_call(
        _merge_body,
        grid=(m // tm, nb),
        in_specs=[
            pl.BlockSpec((tm, kr), lambda i, j: (i, 0)),
            pl.BlockSpec((tm, ks), lambda i, j: (i, 0)),
            pl.BlockSpec((kr, tn), lambda i, j: (0, j)),
            pl.BlockSpec((ks, tn), lambda i, j: (0, j)),
            pl.BlockSpec((tm, tn), lambda i, j: (i, j)),
            pl.BlockSpec((tm, tn), lambda i, j: (i, j + nb)),
        ],
        out_specs=pl.BlockSpec((tm, tn), lambda i, j: (i, j)),
        out_shape=jax.ShapeDtypeStruct((m, n), BF16),
        compiler_params=_params("parallel", "arbitrary"),
        name="merge",
    )(y_ret, y_ssd, w_ret, w_ssd, gates, gates)


def _mix_xattn_body(m_ref, x_ref, k_ref, v_ref, wmix_hbm, nw_ref, wq_hbm, wo_hbm, o_ref,
                    xn_ref, wmix_ref, wq_ref, wo_ref):
    d = x_ref.shape[1]
    hd = d // XA_HEADS
    scale = hd ** -0.5

    @pl.when(jnp.logical_and(pl.program_id(0) == 0, pl.program_id(1) == 0))
    def _():
        pltpu.sync_copy(wmix_hbm, wmix_ref)
        pltpu.sync_copy(wq_hbm, wq_ref)
        pltpu.sync_copy(wo_hbm, wo_ref)

    o_ref[...] = x_ref[...] + _dot(m_ref[...], wmix_ref[...])
    xn_ref[...] = _rms(o_ref[...], nw_ref[...]).astype(BF16)
    for h in range(XA_HEADS):
        sl = slice(h * hd, (h + 1) * hd)
        q = _dot(xn_ref[...], wq_ref[:, sl]).astype(BF16)
        s = _dot_nt(q, k_ref[:, sl]) * scale
        s = s - jnp.max(s, axis=-1, keepdims=True)
        p = jnp.exp(s)
        p = p / jnp.sum(p, axis=-1, keepdims=True)
        o_h = _dot(p.astype(BF16), v_ref[:, sl]).astype(BF16)
        o_ref[...] += _dot(o_h, wo_ref[sl, :])


def _mix_xattn(merged, x, kv, w_mix, nw, w_q, w_o, bsz, seq, mem_len, tq):
    d = x.shape[1]
    nq = seq // tq
    tile = lambda b, i: (b * nq + i, 0)
    in_hbm = pl.BlockSpec(memory_space=pl.ANY)
    return pl.pallas_call(
        _mix_xattn_body,
        grid=(bsz, nq),
        in_specs=[
            pl.BlockSpec((tq, d), tile),
            pl.BlockSpec((tq, d), tile),
            pl.BlockSpec((mem_len, d), lambda b, i: (b, 0)),
            pl.BlockSpec((mem_len, d), lambda b, i: (b, 1)),
            in_hbm,
            pl.BlockSpec((1, d), lambda b, i: (0, 0)),
            in_hbm,
            in_hbm,
        ],
        out_specs=pl.BlockSpec((tq, d), tile),
        out_shape=jax.ShapeDtypeStruct((bsz * seq, d), F32),
        scratch_shapes=[pltpu.VMEM((tq, d), BF16)] + [pltpu.VMEM((d, d), BF16)] * 3,
        compiler_params=_params("arbitrary", "arbitrary"),
        name="mix_xattn",
    )(merged, x, kv, kv, w_mix, nw, w_q, w_o)


def _mlp_body(h_ref, nw_ref, w1_ref, w2_ref, fw_ref, o_ref, xn_ref):
    f = pl.program_id(1)
    blocks = _row_blocks(h_ref.shape[0])

    @pl.when(f == 0)
    def _():
        for rows in blocks:
            hh = h_ref[rows, :]
            xn_ref[rows, :] = _rms(hh, nw_ref[...]).astype(BF16)
            o_ref[rows, :] = hh

    w1b = w1_ref[...].astype(BF16)
    w2b = w2_ref[...].astype(BF16)
    for rows in blocks:
        u = _dot(xn_ref[rows, :], w1b)
        u = jnp.square(jnp.maximum(u, 0.0)).astype(BF16)
        o_ref[rows, :] += _dot(u, w2b)

    @pl.when(f == pl.num_programs(1) - 1)
    def _():
        for rows in blocks:
            o_ref[rows, :] = _rms(o_ref[rows, :], fw_ref[...])


def _mlp(h, nw, w1, w2, fw, tm, tf):
    m, d = h.shape
    dff = w1.shape[1]
    once = pl.Buffered(1)
    return pl.pallas_call(
        _mlp_body,
        grid=(m // tm, dff // tf),
        in_specs=[
            pl.BlockSpec((tm, d), lambda i, f: (i, 0)),
            pl.BlockSpec((1, d), lambda i, f: (0, 0)),
            pl.BlockSpec((d, tf), lambda i, f: (0, f)),
            pl.BlockSpec((tf, d), lambda i, f: (f, 0)),
            pl.BlockSpec((1, d), lambda i, f: (0, 0)),
        ],
        out_specs=pl.BlockSpec((tm, d), lambda i, f: (i, 0)),
        out_shape=jax.ShapeDtypeStruct((m, d), F32),
        scratch_shapes=[pltpu.VMEM((tm, d), BF16)],
        compiler_params=_params("parallel", "arbitrary"),
        name="mlp",
    )(h, nw, w1, w2, fw)


def _rope_tables(seq, dk):
    half = dk // 2
    pos = jnp.arange(seq, dtype=F32)
    inv = jnp.exp(-math.log(ROPE_BASE) * jnp.arange(half, dtype=F32) / half)
    ang = pos[:, None] * inv[None, :]
    cos, sin = jnp.cos(ang), jnp.sin(ang)
    return jnp.concatenate([cos, cos], axis=1), jnp.concatenate([-sin, sin], axis=1)


def kernel(x, mem, norm_mix_w, w_in, conv_w, conv_b, dt_bias, a_log, d_skip, ret_norm_w, ssd_norm_w, w_ret_out, w_ssd_out, w_mix_out, norm_xa_w, mem_norm_w, w_xq, w_xkv, w_xo, norm_ff_w, w_ff1, w_ff2, final_norm_w):
    bsz, seq, d = x.shape
    mem_len = mem.shape[1]
    depth = w_in.shape[0]
    n_tok = bsz * seq
    ret_qk = RET_HEADS * RET_DK
    ret_v = RET_HEADS * RET_DV
    inner = w_ssd_out.shape[1]
    conv_dim = conv_w.shape[2]
    n_heads = inner // SSD_HEAD_DIM
    ret_w = 2 * ret_qk + 2 * ret_v
    off_dt = ret_w + inner + conv_dim
    off_gate = off_dt + n_heads
    assert w_in.shape[2] == off_gate + 2 * d and depth == 1

    tn_main = 1024
    assert conv_dim % tn_main == 0 and ret_w % tn_main == 0 and inner % tn_main == 0
    shift = conv_dim // tn_main
    xbc_off, ret_off, z_off = 0, conv_dim, conv_dim + ret_w

    cos2, sin2 = _rope_tables(seq, RET_DK)
    row = lambda v: v.reshape(1, -1)
    pad_heads = lambda v: jnp.pad(v.reshape(1, -1), ((0, 0), (0, LANES - n_heads)))

    w_in_t = jnp.swapaxes(w_in, 1, 2)

    h = x.reshape(n_tok, d)
    for l in range(depth):
        xn = _prenorm(h, row(norm_mix_w[l]), 512)
        proj = _in_proj_main(xn, w_in_t, l, off_dt, shift, 2048, tn_main)
        tail_rows = lax.optimization_barrier((w_in_t[l, off_gate:], w_in_t[l, off_dt:off_gate]))
        wt_tail = jnp.concatenate(
            [tail_rows[0].astype(BF16), tail_rows[1].astype(BF16),
             jnp.zeros((LANES - n_heads, d), BF16)], axis=0)
        gates, dt_raw = _in_proj_tail(xn, wt_tail, 2048, wt_tail.shape[0] // 3)
        y_ret = _retention(proj, ret_off, cos2, sin2, row(ret_norm_w[l]), bsz, seq)
        y_ssd = _ssd(proj, z_off, xbc_off, inner, conv_dim, dt_raw, conv_w[l], row(conv_b[l]),
                     pad_heads(dt_bias[l]), pad_heads(a_log[l]),
                     row(jnp.repeat(d_skip[l], SSD_HEAD_DIM)), row(ssd_norm_w[l]), bsz, seq)
        merged = _merge(y_ret, y_ssd, w_ret_out[l].astype(BF16), w_ssd_out[l].astype(BF16),
                        gates, 1024, 512)
        kv = _norm_matmul(mem.reshape(bsz * mem_len, d), row(mem_norm_w[l]), w_xkv[l],
                          BF16, bsz * mem_len, 1024, "xa_kv")
        h = _mix_xattn(merged, h, kv, w_mix_out[l].astype(BF16), row(norm_xa_w[l]),
                       w_xq[l].astype(BF16), w_xo[l].astype(BF16), bsz, seq, mem_len, 512)
        h = _mlp(h, row(norm_ff_w[l]), w_ff1[l], w_ff2[l], row(final_norm_w), 1024, 512)
    return h.reshape(bsz, seq, d)
```

```python
import functools
import itertools
import math

import numpy as np
import jax
import jax.numpy as jnp
from jax import lax
from jax.experimental import pallas as pl
from jax.experimental.pallas import tpu as pltpu

F32 = jnp.float32
BF16 = jnp.bfloat16
EPS = 1e-6

RET_HEADS = 8
RET_DK = 128
RET_DV = 256
CHUNK = 128
ROPE_BASE = 10000.0
SSD_HEAD_DIM = 64
SSD_GROUPS = 8
SSD_STATE = 128
SSD_CONV = 4
CONV_HIST = 8
XA_HEADS = 4

V7X_VMEM_BYTES = 64 * 1024 * 1024
VMEM_LIMIT = V7X_VMEM_BYTES - 4 * 1024 * 1024
LANES = 128
ROW_SPLIT = 512
CONV_ROW_SPLIT = 256
LOG2_E = math.log2(math.e)
RET_STEP_CHUNKS = 8
SSD_STEP_CHUNKS = 2

_LOG_G = np.log(1.0 - np.exp2(-5.0 - np.arange(RET_HEADS, dtype=np.float32))).astype(np.float32)
_RET_CHUNK_DECAY = np.exp(np.float32(CHUNK) * _LOG_G).astype(np.float32)


def _params(*sem):
    return pltpu.CompilerParams(dimension_semantics=sem, vmem_limit_bytes=VMEM_LIMIT)


def _rms(x, w):
    ms = jnp.mean(x * x, axis=-1, keepdims=True)
    return x * lax.rsqrt(ms + EPS) * w


def _sigmoid(x):
    return 0.5 + 0.5 * jnp.tanh(0.5 * x)


def _silu(x):
    h = 0.5 * x
    return h + h * jnp.tanh(h)


def _dot(a, b):
    return jnp.dot(a, b, preferred_element_type=F32)


def _dot_nt(a, b):
    return lax.dot_general(a, b, (((1,), (1,)), ((), ())), preferred_element_type=F32)


def _row_blocks(n_rows, step=ROW_SPLIT):
    step = min(step, n_rows)
    return [slice(r, r + step) for r in range(0, n_rows, step)]


def _prenorm_body(x_ref, nw_ref, o_ref):
    o_ref[...] = _rms(x_ref[...], nw_ref[...]).astype(o_ref.dtype)


def _prenorm(x, nw, tm):
    m, k = x.shape
    return pl.pallas_call(
        _prenorm_body,
        grid=(m // tm,),
        in_specs=[pl.BlockSpec((tm, k), lambda i: (i, 0)), pl.BlockSpec((1, k), lambda i: (0, 0))],
        out_specs=pl.BlockSpec((tm, k), lambda i: (i, 0)),
        out_shape=jax.ShapeDtypeStruct((m, k), BF16),
        compiler_params=_params("parallel"),
        name="prenorm",
    )(x, nw)


def _matmul_nt_body(a_ref, wt_ref, o_ref):
    wb = wt_ref[...].astype(BF16)
    for rows in _row_blocks(a_ref.shape[0]):
        o_ref[rows, :] = _dot_nt(a_ref[rows, :], wb).astype(o_ref.dtype)


def _in_proj_main(a, w3, layer, row_off, n_cols, shift, tm, tn):
    m, k = a.shape
    nb = n_cols // tn
    off = row_off // tn
    assert n_cols % tn == 0 and m % tm == 0 and row_off % tn == 0
    return pl.pallas_call(
        _matmul_nt_body,
        grid=(m // tm, nb),
        in_specs=[
            pl.BlockSpec((tm, k), lambda i, j: (i, 0)),
            pl.BlockSpec((None, tn, k), lambda i, j: (layer, j + off, 0)),
        ],
        out_specs=pl.BlockSpec((tm, tn), lambda i, j: (i, (j + shift) % nb)),
        out_shape=jax.ShapeDtypeStruct((m, n_cols), BF16),
        compiler_params=_params("parallel", "arbitrary"),
        name="in_proj_main",
    )(a, w3)


def _in_proj_conv_body(a_ref, wt_ref, cw_ref, cb_ref, o_ref, carry_ref, *, tiles_per_seq):
    i = pl.program_id(0)
    j = pl.program_id(1)
    tn = wt_ref.shape[0]

    @pl.when(i % tiles_per_seq == 0)
    def _():
        carry_ref[j] = jnp.zeros((CONV_HIST, tn), F32)

    wb = wt_ref[...].astype(BF16)
    cwh = 0.5 * cw_ref[...]
    cbh = 0.5 * cb_ref[...]
    hist = carry_ref[j]
    sub = lax.broadcasted_iota(jnp.int32, (CONV_HIST, tn), 0)
    for rows in _row_blocks(a_ref.shape[0], CONV_ROW_SPLIT):
        res = _dot_nt(a_ref[rows, :], wb)
        acc = cbh + cwh[SSD_CONV - 1:SSD_CONV, :] * res
        for k in range(SSD_CONV - 1):
            dist = SSD_CONV - 1 - k
            rolled = pltpu.roll(res, dist, 0)
            head = jnp.where(sub < dist, pltpu.roll(hist, dist, 0), rolled[0:CONV_HIST, :])
            shifted = jnp.concatenate([head, rolled[CONV_HIST:, :]], axis=0)
            acc = acc + cwh[k:k + 1, :] * shifted
        o_ref[rows, :] = (acc + acc * jnp.tanh(acc)).astype(o_ref.dtype)
        hist = res[res.shape[0] - CONV_HIST:, :]
    carry_ref[j] = hist


def _in_proj_conv(a, w3, layer, row_off, n_cols, conv_w, conv_b, seq, tm, tn):
    m, k = a.shape
    nb = n_cols // tn
    off = row_off // tn
    assert n_cols % tn == 0 and m % tm == 0 and row_off % tn == 0 and seq % tm == 0
    assert SSD_CONV - 1 <= CONV_HIST
    body = functools.partial(_in_proj_conv_body, tiles_per_seq=seq // tm)
    return pl.pallas_call(
        body,
        grid=(m // tm, nb),
        in_specs=[
            pl.BlockSpec((tm, k), lambda i, j: (i, 0)),
            pl.BlockSpec((None, tn, k), lambda i, j: (layer, j + off, 0)),
            pl.BlockSpec((SSD_CONV, tn), lambda i, j: (0, j)),
            pl.BlockSpec((1, tn), lambda i, j: (0, j)),
        ],
        out_specs=pl.BlockSpec((tm, tn), lambda i, j: (i, j)),
        out_shape=jax.ShapeDtypeStruct((m, n_cols), BF16),
        scratch_shapes=[pltpu.VMEM((nb, CONV_HIST, tn), F32)],
        compiler_params=_params("arbitrary", "arbitrary"),
        name="in_proj_conv",
    )(a, w3, conv_w, conv_b)


def _in_proj_tail_body(a_ref, wt_ref, g_ref, dt_ref):
    tn = wt_ref.shape[0]
    for rows in _row_blocks(a_ref.shape[0]):
        res = _dot_nt(a_ref[rows, :], wt_ref[...])
        g_ref[rows, :] = res.astype(g_ref.dtype)
        dt_ref[rows, :] = res[:, tn - LANES:]


def _in_proj_tail(a, wt_tail, tm, tn):
    m, k = a.shape
    n = wt_tail.shape[0]
    assert n % tn == 0 and tn % LANES == 0
    return pl.pallas_call(
        _in_proj_tail_body,
        grid=(m // tm, n // tn),
        in_specs=[
            pl.BlockSpec((tm, k), lambda i, j: (i, 0)),
            pl.BlockSpec((tn, k), lambda i, j: (j, 0)),
        ],
        out_specs=[
            pl.BlockSpec((tm, tn), lambda i, j: (i, j)),
            pl.BlockSpec((tm, LANES), lambda i, j: (i, 0)),
        ],
        out_shape=[jax.ShapeDtypeStruct((m, n), BF16), jax.ShapeDtypeStruct((m, LANES), F32)],
        compiler_params=_params("parallel", "arbitrary"),
        name="in_proj_tail",
    )(a, wt_tail)


def _norm_matmul_body(x_ref, nw_ref, w_ref, o_ref, xn_ref):
    @pl.when(pl.program_id(1) == 0)
    def _():
        xn_ref[...] = _rms(x_ref[...], nw_ref[...]).astype(BF16)

    o_ref[...] = _dot(xn_ref[...], w_ref[...].astype(BF16)).astype(o_ref.dtype)


def _norm_matmul(x, nw, w, out_dtype, tm, tn, name):
    m, k = x.shape
    n = w.shape[1]
    return pl.pallas_call(
        _norm_matmul_body,
        grid=(m // tm, n // tn),
        in_specs=[
            pl.BlockSpec((tm, k), lambda i, j: (i, 0)),
            pl.BlockSpec((1, k), lambda i, j: (0, 0)),
            pl.BlockSpec((k, tn), lambda i, j: (0, j)),
        ],
        out_specs=pl.BlockSpec((tm, tn), lambda i, j: (i, j)),
        out_shape=jax.ShapeDtypeStruct((m, n), out_dtype),
        scratch_shapes=[pltpu.VMEM((tm, k), BF16)],
        compiler_params=_params("parallel", "arbitrary"),
        name=name,
    )(x, nw, w)


def _retention_tables():
    idx = np.arange(CHUNK, dtype=np.float32)
    rel = idx[:, None] - idx[None, :]
    lg = _LOG_G[:, None, None]
    scale = np.float32(RET_DK ** -0.5)
    decay = np.where(rel[None] >= 0, np.exp(np.maximum(rel, 0.0)[None] * lg), 0.0) * scale
    xi = np.exp((idx + 1.0)[None, :, None] * lg) * scale
    zeta = np.exp((CHUNK - 1.0 - idx)[None, :, None] * lg)
    wide = (RET_HEADS, CHUNK, RET_DK)
    return (decay.astype(np.float32), np.broadcast_to(xi, wide).astype(np.float32),
            np.broadcast_to(zeta, wide).astype(np.float32))


def _retention_body(q_ref, k_ref, v_ref, g_ref, cos_ref, sin_ref, dec_ref, xi_ref, zeta_ref,
                    nw_ref, o_ref, state_ref):
    @pl.when(pl.program_id(1) == 0)
    def _():
        state_ref[...] = jnp.zeros_like(state_ref)

    for cc, h in itertools.product(range(RET_STEP_CHUNKS), range(RET_HEADS)):
        rows = slice(cc * CHUNK, (cc + 1) * CHUNK)
        cos = cos_ref[rows, :]
        sin = sin_ref[rows, :]
        q = q_ref[rows, h * RET_DK:(h + 1) * RET_DK].astype(F32)
        k = k_ref[rows, h * RET_DK:(h + 1) * RET_DK].astype(F32)
        v = v_ref[rows, h * RET_DV:(h + 1) * RET_DV]
        qr = q * cos + pltpu.roll(q, RET_DK // 2, 1) * sin
        kr = k * cos + pltpu.roll(k, RET_DK // 2, 1) * sin
        s = _dot_nt(qr.astype(BF16), kr.astype(BF16)) * dec_ref[h]
        st = state_ref[h]
        lhs = jnp.concatenate([s.astype(BF16), (qr * xi_ref[h]).astype(BF16)], axis=1)
        rhs = jnp.concatenate([v, st.astype(BF16)], axis=0)
        y = _dot(lhs, rhs)
        kz_t = (kr * zeta_ref[h]).T.astype(BF16)
        state_ref[h] = float(_RET_CHUNK_DECAY[h]) * st + _dot(kz_t, v)
        mu = jnp.mean(y, axis=-1, keepdims=True)
        yc = y - mu
        var = jnp.mean(yc * yc, axis=-1, keepdims=True)
        yn = yc * lax.rsqrt(var + EPS)
        hsl = slice(h * RET_DV, (h + 1) * RET_DV)
        o_ref[rows, hsl] = (yn * nw_ref[:, hsl] * _silu(g_ref[rows, hsl].astype(F32))).astype(BF16)


def _retention(proj, col_off, cos2, sin2, nw, bsz, seq):
    rows = RET_STEP_CHUNKS * CHUNK
    nc = seq // rows
    qk = RET_HEADS * RET_DK
    vd = RET_HEADS * RET_DV
    assert vd == 2 * qk and col_off % vd == 0 and seq % rows == 0
    oq = col_off // qk
    ov = col_off // vd
    decay, xi, zeta = _retention_tables()
    table = lambda n: pl.BlockSpec((RET_HEADS, CHUNK, n), lambda b, c: (0, 0, 0))
    return pl.pallas_call(
        _retention_body,
        grid=(bsz, nc),
        in_specs=[
            pl.BlockSpec((rows, qk), lambda b, c: (b * nc + c, oq)),
            pl.BlockSpec((rows, qk), lambda b, c: (b * nc + c, oq + 1)),
            pl.BlockSpec((rows, vd), lambda b, c: (b * nc + c, ov + 1)),
            pl.BlockSpec((rows, vd), lambda b, c: (b * nc + c, ov + 2)),
            pl.BlockSpec((rows, RET_DK), lambda b, c: (c, 0)),
            pl.BlockSpec((rows, RET_DK), lambda b, c: (c, 0)),
            table(CHUNK), table(RET_DK), table(RET_DK),
            pl.BlockSpec((1, vd), lambda b, c: (0, 0)),
        ],
        out_specs=pl.BlockSpec((rows, vd), lambda b, c: (b * nc + c, 0)),
        out_shape=jax.ShapeDtypeStruct((bsz * seq, vd), BF16),
        scratch_shapes=[pltpu.VMEM((RET_HEADS, RET_DK, RET_DV), F32)],
        compiler_params=_params("parallel", "arbitrary"),
        name="retention",
    )(proj, proj, proj, proj, cos2, sin2, decay, xi, zeta, nw)


def _ssd_body(z_ref, xbc_ref, dt_ref, dtb_ref, alog_ref, dsk_ref, nw_ref,
              o_ref, prev_ref, *, inner, heads_per_group):
    c = CHUNK
    n_sub = z_ref.shape[0] // c
    gw = heads_per_group * SSD_HEAD_DIM

    @pl.when(pl.program_id(1) == 0)
    def _():
        prev_ref[...] = jnp.zeros_like(prev_ref)

    row = lax.broadcasted_iota(jnp.int32, (c, c), 0)
    col = lax.broadcasted_iota(jnp.int32, (c, c), 1)
    tri = row >= col
    lane_lo = lax.broadcasted_iota(jnp.int32, (c, LANES), 1) < SSD_HEAD_DIM
    neg_inf = jnp.float32(-jnp.inf)
    a = -jnp.exp(alog_ref[...])

    for cc in range(n_sub):
        rows = slice(cc * c, (cc + 1) * c)
        dtr = dt_ref[rows, :] + dtb_ref[...]
        dt = jnp.maximum(dtr, 0.0) + jnp.log1p(jnp.exp(-jnp.abs(dtr)))
        da = dt * a
        a_cs = jnp.dot(tri.astype(F32), da, precision=lax.Precision.HIGHEST,
                       preferred_element_type=F32)
        a_cs = a_cs * LOG2_E
        a_last = a_cs[c - 1:c, :]
        w1 = jnp.exp2(a_last - a_cs) * dt
        src_t = (a_cs - jnp.log2(dt)).T
        w1_t = w1.T

        for g in range(SSD_GROUPS):
            b_gb = xbc_ref[rows, inner + g * SSD_STATE: inner + (g + 1) * SSD_STATE]
            c_gb = xbc_ref[rows, inner + (SSD_GROUPS + g) * SSD_STATE:
                           inner + (SSD_GROUPS + g + 1) * SSD_STATE]
            cb = _dot_nt(c_gb, b_gb)
            c_g = c_gb.astype(F32)
            b_gt = b_gb.astype(F32).T
            y_pairs = []
            for pr in range(heads_per_group // 2):
                h0 = g * heads_per_group + 2 * pr
                psl = slice(h0 * SSD_HEAD_DIM, (h0 + 2) * SSD_HEAD_DIM)
                xs_b = xbc_ref[rows, psl]
                prev = prev_ref[:, psl]
                rhs = jnp.concatenate([xs_b, prev.astype(BF16)], axis=0)
                lhs, bws, cds = [], [], []
                for h in (h0, h0 + 1):
                    colb = jnp.broadcast_to(a_cs[:, h:h + 1], (c, c))
                    rowb = jnp.broadcast_to(src_t[h:h + 1, :], (c, c))
                    m = cb * jnp.exp2(jnp.where(tri, colb - rowb, neg_inf))
                    e = jnp.exp2(colb)
                    lhs.append(jnp.concatenate([m.astype(BF16), (e * c_g).astype(BF16)], axis=1))
                    bws.append((b_gt * jnp.broadcast_to(w1_t[h:h + 1, :], (c, c))).astype(BF16))
                    cds.append(e[c - 1:c, :])
                ys = _dot(jnp.concatenate(lhs, axis=0), rhs)
                sts = _dot(jnp.concatenate(bws, axis=0), xs_b)
                y_pairs.append(jnp.where(lane_lo, ys[0:c, :], ys[c:2 * c, :]))
                cd = jnp.where(lane_lo[0:1, :], cds[0], cds[1])
                prev_ref[:, psl] = cd * prev + jnp.where(lane_lo, sts[0:c, :], sts[c:2 * c, :])
            gsl = slice(g * gw, (g + 1) * gw)
            y = jnp.concatenate(y_pairs, axis=1) + dsk_ref[:, gsl] * xbc_ref[rows, gsl].astype(F32)
            gy = y * _silu(z_ref[rows, gsl].astype(F32))
            ms = jnp.mean(gy * gy, axis=-1, keepdims=True)
            o_ref[rows, gsl] = (gy * lax.rsqrt(ms + EPS) * nw_ref[:, gsl]).astype(BF16)


def _ssd(p_z, z_off, p_xbc, inner, dt_raw, dtb, alog, dskip, nw, bsz, seq):
    rows = SSD_STEP_CHUNKS * CHUNK
    nc = seq // rows
    n_heads = inner // SSD_HEAD_DIM
    heads_per_group = n_heads // SSD_GROUPS
    conv_dim = p_xbc.shape[1]
    assert conv_dim == inner + 2 * SSD_GROUPS * SSD_STATE and SSD_STATE == CHUNK and seq % rows == 0
    assert n_heads <= LANES and heads_per_group % 2 == 0 and 2 * SSD_HEAD_DIM == LANES
    assert z_off % inner == 0
    zb = z_off // inner
    body = functools.partial(_ssd_body, inner=inner, heads_per_group=heads_per_group)
    row = lambda b, c: (b * nc + c, 0)
    fixed = lambda b, c: (0, 0)
    return pl.pallas_call(
        body,
        grid=(bsz, nc),
        in_specs=[
            pl.BlockSpec((rows, inner), lambda b, c: (b * nc + c, zb)),
            pl.BlockSpec((rows, conv_dim), row),
            pl.BlockSpec((rows, LANES), row),
            pl.BlockSpec((1, LANES), fixed),
            pl.BlockSpec((1, LANES), fixed),
            pl.BlockSpec((1, inner), fixed),
            pl.BlockSpec((1, inner), fixed),
        ],
        out_specs=pl.BlockSpec((rows, inner), row),
        out_shape=jax.ShapeDtypeStruct((bsz * seq, inner), BF16),
        scratch_shapes=[pltpu.VMEM((SSD_STATE, inner), F32)],
        compiler_params=_params("parallel", "arbitrary"),
        name="ssd",
    )(p_z, p_xbc, dt_raw, dtb, alog, dskip, nw)


def _merge_body(yr_ref, ys_ref, wr_ref, ws_ref, ga_ref, gb_ref, o_ref):
    for rows in _row_blocks(yr_ref.shape[0]):
        br = _dot(yr_ref[rows, :], wr_ref[...])
        bs = _dot(ys_ref[rows, :], ws_ref[...])
        ga = _sigmoid(ga_ref[rows, :].astype(F32))
        gb = _sigmoid(gb_ref[rows, :].astype(F32))
        o_ref[rows, :] = (ga * br + gb * bs).astype(o_ref.dtype)


def _merge(y_ret, y_ssd, w_ret, w_ssd, gates, tm, tn):
    m, kr = y_ret.shape
    ks = y_ssd.shape[1]
    n = w_ret.shape[1]
    nb = n // tn
    return pl.pallas_call(
        _merge_body,
        grid=(m // tm, nb),
        in_specs=[
            pl.BlockSpec((tm, kr), lambda i, j: (i, 0)),
            pl.BlockSpec((tm, ks), lambda i, j: (i, 0)),
            pl.BlockSpec((kr, tn), lambda i, j: (0, j)),
            pl.BlockSpec((ks, tn), lambda i, j: (0, j)),
            pl.BlockSpec((tm, tn), lambda i, j: (i, j)),
            pl.BlockSpec((tm, tn), lambda i, j: (i, j + nb)),
        ],
        out_specs=pl.BlockSpec((tm, tn), lambda i, j: (i, j)),
        out_shape=jax.ShapeDtypeStruct((m, n), BF16),
        compiler_params=_params("parallel", "arbitrary"),
        name="merge",
    )(y_ret, y_ssd, w_ret, w_ssd, gates, gates)


def _mix_xattn_body(m_ref, x_ref, k_ref, v_ref, wmix_ref, nw_ref, wq_ref, wo_ref, o_ref, xn_ref):
    d = x_ref.shape[1]
    hd = d // XA_HEADS
    scale = hd ** -0.5
    o_ref[...] = x_ref[...] + _dot(m_ref[...], wmix_ref[...])
    xn_ref[...] = _rms(o_ref[...], nw_ref[...]).astype(BF16)
    for h in range(XA_HEADS):
        sl = slice(h * hd, (h + 1) * hd)
        q = _dot(xn_ref[...], wq_ref[:, sl]).astype(BF16)
        s = _dot_nt(q, k_ref[:, sl]) * scale
        s = s - jnp.max(s, axis=-1, keepdims=True)
        p = jnp.exp(s)
        p = p / jnp.sum(p, axis=-1, keepdims=True)
        o_h = _dot(p.astype(BF16), v_ref[:, sl]).astype(BF16)
        o_ref[...] += _dot(o_h, wo_ref[sl, :])


def _mix_xattn(merged, x, kv, w_mix, nw, w_q, w_o, bsz, seq, mem_len, tq):
    d = x.shape[1]
    nq = seq // tq
    tile = lambda b, i: (b * nq + i, 0)
    fixed = lambda b, i: (0, 0)
    resident = pl.Buffered(1)
    return pl.pallas_call(
        _mix_xattn_body,
        grid=(bsz, nq),
        in_specs=[
            pl.BlockSpec((tq, d), tile),
            pl.BlockSpec((tq, d), tile),
            pl.BlockSpec((mem_len, d), lambda b, i: (b, 0)),
            pl.BlockSpec((mem_len, d), lambda b, i: (b, 1)),
            pl.BlockSpec((d, d), fixed, pipeline_mode=resident),
            pl.BlockSpec((1, d), fixed),
            pl.BlockSpec((d, d), fixed, pipeline_mode=resident),
            pl.BlockSpec((d, d), fixed, pipeline_mode=resident),
        ],
        out_specs=pl.BlockSpec((tq, d), tile),
        out_shape=jax.ShapeDtypeStruct((bsz * seq, d), F32),
        scratch_shapes=[pltpu.VMEM((tq, d), BF16)],
        compiler_params=_params("parallel", "arbitrary"),
        name="mix_xattn",
    )(merged, x, kv, kv, w_mix, nw, w_q, w_o)


def _mlp_body(h_ref, nw_ref, w1_ref, w2_ref, fw_ref, o_ref, xn_ref):
    f = pl.program_id(1)
    blocks = _row_blocks(h_ref.shape[0])

    @pl.when(f == 0)
    def _():
        for rows in blocks:
            hh = h_ref[rows, :]
            xn_ref[rows, :] = _rms(hh, nw_ref[...]).astype(BF16)
            o_ref[rows, :] = hh

    w1b = w1_ref[...].astype(BF16)
    w2b = w2_ref[...].astype(BF16)
    for rows in blocks:
        u = _dot(xn_ref[rows, :], w1b)
        u = jnp.square(jnp.maximum(u, 0.0)).astype(BF16)
        o_ref[rows, :] += _dot(u, w2b)

    @pl.when(f == pl.num_programs(1) - 1)
    def _():
        for rows in blocks:
            o_ref[rows, :] = _rms(o_ref[rows, :], fw_ref[...])


def _mlp(h, nw, w1, w2, fw, tm, tf):
    m, d = h.shape
    dff = w1.shape[1]
    once = pl.Buffered(1)
    return pl.pallas_call(
        _mlp_body,
        grid=(m // tm, dff // tf),
        in_specs=[
            pl.BlockSpec((tm, d), lambda i, f: (i, 0)),
            pl.BlockSpec((1, d), lambda i, f: (0, 0)),
            pl.BlockSpec((d, tf), lambda i, f: (0, f)),
            pl.BlockSpec((tf, d), lambda i, f: (f, 0)),
            pl.BlockSpec((1, d), lambda i, f: (0, 0)),
        ],
        out_specs=pl.BlockSpec((tm, d), lambda i, f: (i, 0)),
        out_shape=jax.ShapeDtypeStruct((m, d), F32),
        scratch_shapes=[pltpu.VMEM((tm, d), BF16)],
        compiler_params=_params("parallel", "arbitrary"),
        name="mlp",
    )(h, nw, w1, w2, fw)


def _rope_tables(seq, dk):
    half = dk // 2
    pos = jnp.arange(seq, dtype=F32)
    inv = jnp.exp(-math.log(ROPE_BASE) * jnp.arange(half, dtype=F32) / half)
    ang = pos[:, None] * inv[None, :]
    cos, sin = jnp.cos(ang), jnp.sin(ang)
    return jnp.concatenate([cos, cos], axis=1), jnp.concatenate([-sin, sin], axis=1)


def kernel(x, mem, norm_mix_w, w_in, conv_w, conv_b, dt_bias, a_log, d_skip, ret_norm_w, ssd_norm_w, w_ret_out, w_ssd_out, w_mix_out, norm_xa_w, mem_norm_w, w_xq, w_xkv, w_xo, norm_ff_w, w_ff1, w_ff2, final_norm_w):
    bsz, seq, d = x.shape
    mem_len = mem.shape[1]
    depth = w_in.shape[0]
    n_tok = bsz * seq
    ret_qk = RET_HEADS * RET_DK
    ret_v = RET_HEADS * RET_DV
    inner = w_ssd_out.shape[1]
    conv_dim = conv_w.shape[2]
    n_heads = inner // SSD_HEAD_DIM
    ret_w = 2 * ret_qk + 2 * ret_v
    off_dt = ret_w + inner + conv_dim
    off_gate = off_dt + n_heads
    assert w_in.shape[2] == off_gate + 2 * d and depth == 1

    tn_main = 1024
    assert conv_dim % tn_main == 0 and ret_w % tn_main == 0 and inner % tn_main == 0
    shift = inner // tn_main
    z_off, ret_off = 0, inner

    cos2, sin2 = _rope_tables(seq, RET_DK)
    row = lambda v: v.reshape(1, -1)
    pad_heads = lambda v: jnp.pad(v.reshape(1, -1), ((0, 0), (0, LANES - n_heads)))

    w_in_t = jnp.swapaxes(w_in, 1, 2)

    h = x.reshape(n_tok, d)
    for l in range(depth):
        xn = _prenorm(h, row(norm_mix_w[l]), 512)
        proj = _in_proj_main(xn, w_in_t, l, 0, ret_w + inner, shift, 2048, tn_main)
        p_xbc = _in_proj_conv(xn, w_in_t, l, ret_w + inner, conv_dim, conv_w[l], row(conv_b[l]),
                              seq, 2048, tn_main)
        tail_rows = lax.optimization_barrier((w_in_t[l, off_gate:], w_in_t[l, off_dt:off_gate]))
        wt_tail = jnp.concatenate(
            [tail_rows[0].astype(BF16), tail_rows[1].astype(BF16),
             jnp.zeros((LANES - n_heads, d), BF16)], axis=0)
        gates, dt_raw = _in_proj_tail(xn, wt_tail, 2048, wt_tail.shape[0] // 3)
        y_ret = _retention(proj, ret_off, cos2, sin2, row(ret_norm_w[l]), bsz, seq)
        y_ssd = _ssd(proj, z_off, p_xbc, inner, dt_raw, pad_heads(dt_bias[l]), pad_heads(a_log[l]),
                     row(jnp.repeat(d_skip[l], SSD_HEAD_DIM)), row(ssd_norm_w[l]), bsz, seq)
        merged = _merge(y_ret, y_ssd, w_ret_out[l].astype(BF16), w_ssd_out[l].astype(BF16),
                        gates, 1024, 512)
        kv = _norm_matmul(mem.reshape(bsz * mem_len, d), row(mem_norm_w[l]), w_xkv[l],
                          BF16, bsz * mem_len, 1024, "xa_kv")
        h = _mix_xattn(merged, h, kv, w_mix_out[l].astype(BF16), row(norm_xa_w[l]),
                       w_xq[l].astype(BF16), w_xo[l].astype(BF16), bsz, seq, mem_len, 512)
        h = _mlp(h, row(norm_ff_w[l]), w_ff1[l], w_ff2[l], row(final_norm_w), 1024, 512)
    return h.reshape(bsz, seq, d)
```

```python
import functools
import itertools
import math

import numpy as np
import jax
import jax.numpy as jnp
from jax import lax
from jax.experimental import pallas as pl
from jax.experimental.pallas import tpu as pltpu

F32 = jnp.float32
BF16 = jnp.bfloat16
EPS = 1e-6

RET_HEADS = 8
RET_DK = 128
RET_DV = 256
CHUNK = 128
ROPE_BASE = 10000.0
SSD_HEAD_DIM = 64
SSD_GROUPS = 8
SSD_STATE = 128
SSD_CONV = 4
CONV_HIST = 8
XA_HEADS = 4

V7X_VMEM_BYTES = 64 * 1024 * 1024
VMEM_LIMIT = V7X_VMEM_BYTES - 4 * 1024 * 1024
LANES = 128
ROW_SPLIT = 512
CONV_ROW_SPLIT = 256
LOG2_E = math.log2(math.e)
RET_STEP_CHUNKS = 8
SSD_STEP_CHUNKS = 2

_LOG_G = np.log(1.0 - np.exp2(-5.0 - np.arange(RET_HEADS, dtype=np.float32))).astype(np.float32)
_RET_CHUNK_DECAY = np.exp(np.float32(CHUNK) * _LOG_G).astype(np.float32)


def _params(*sem):
    return pltpu.CompilerParams(dimension_semantics=sem, vmem_limit_bytes=VMEM_LIMIT)


def _rms(x, w):
    ms = jnp.mean(x * x, axis=-1, keepdims=True)
    return x * lax.rsqrt(ms + EPS) * w


def _sigmoid(x):
    return 0.5 + 0.5 * jnp.tanh(0.5 * x)


def _silu(x):
    h = 0.5 * x
    return h + h * jnp.tanh(h)


def _dot(a, b):
    return jnp.dot(a, b, preferred_element_type=F32)


def _dot_nt(a, b):
    return lax.dot_general(a, b, (((1,), (1,)), ((), ())), preferred_element_type=F32)


def _row_blocks(n_rows, step=ROW_SPLIT):
    step = min(step, n_rows)
    return [slice(r, r + step) for r in range(0, n_rows, step)]


def _prenorm_body(x_ref, nw_ref, o_ref):
    o_ref[...] = _rms(x_ref[...], nw_ref[...]).astype(o_ref.dtype)


def _prenorm(x, nw, tm):
    m, k = x.shape
    return pl.pallas_call(
        _prenorm_body,
        grid=(m // tm,),
        in_specs=[pl.BlockSpec((tm, k), lambda i: (i, 0)), pl.BlockSpec((1, k), lambda i: (0, 0))],
        out_specs=pl.BlockSpec((tm, k), lambda i: (i, 0)),
        out_shape=jax.ShapeDtypeStruct((m, k), BF16),
        compiler_params=_params("parallel"),
        name="prenorm",
    )(x, nw)


def _matmul_nt_body(a_ref, wt_ref, o_ref):
    wb = wt_ref[...].astype(BF16)
    for rows in _row_blocks(a_ref.shape[0]):
        o_ref[rows, :] = _dot_nt(a_ref[rows, :], wb).astype(o_ref.dtype)


def _in_proj_main(a, w3, layer, row_off, n_cols, shift, tm, tn):
    m, k = a.shape
    nb = n_cols // tn
    off = row_off // tn
    assert n_cols % tn == 0 and m % tm == 0 and row_off % tn == 0
    return pl.pallas_call(
        _matmul_nt_body,
        grid=(m // tm, nb),
        in_specs=[
            pl.BlockSpec((tm, k), lambda i, j: (i, 0)),
            pl.BlockSpec((None, tn, k), lambda i, j: (layer, j + off, 0)),
        ],
        out_specs=pl.BlockSpec((tm, tn), lambda i, j: (i, (j + shift) % nb)),
        out_shape=jax.ShapeDtypeStruct((m, n_cols), BF16),
        compiler_params=_params("parallel", "arbitrary"),
        name="in_proj_main",
    )(a, w3)


def _in_proj_conv_body(a_ref, wt_ref, cw_ref, cb_ref, o_ref, carry_ref, *, tiles_per_seq):
    i = pl.program_id(0)
    j = pl.program_id(1)
    tn = wt_ref.shape[0]

    @pl.when(i % tiles_per_seq == 0)
    def _():
        carry_ref[j] = jnp.zeros((CONV_HIST, tn), F32)

    wb = wt_ref[...].astype(BF16)
    cwh = 0.5 * cw_ref[...]
    cbh = 0.5 * cb_ref[...]
    hist = carry_ref[j]
    sub = lax.broadcasted_iota(jnp.int32, (CONV_HIST, tn), 0)
    for rows in _row_blocks(a_ref.shape[0], CONV_ROW_SPLIT):
        res = _dot_nt(a_ref[rows, :], wb)
        acc = cbh + cwh[SSD_CONV - 1:SSD_CONV, :] * res
        for k in range(SSD_CONV - 1):
            dist = SSD_CONV - 1 - k
            rolled = pltpu.roll(res, dist, 0)
            head = jnp.where(sub < dist, pltpu.roll(hist, dist, 0), rolled[0:CONV_HIST, :])
            shifted = jnp.concatenate([head, rolled[CONV_HIST:, :]], axis=0)
            acc = acc + cwh[k:k + 1, :] * shifted
        o_ref[rows, :] = (acc + acc * jnp.tanh(acc)).astype(o_ref.dtype)
        hist = res[res.shape[0] - CONV_HIST:, :]
    carry_ref[j] = hist


def _in_proj_conv(a, w3, layer, row_off, n_cols, conv_w, conv_b, seq, tm, tn):
    m, k = a.shape
    nb = n_cols // tn
    off = row_off // tn
    assert n_cols % tn == 0 and m % tm == 0 and row_off % tn == 0 and seq % tm == 0
    assert SSD_CONV - 1 <= CONV_HIST
    body = functools.partial(_in_proj_conv_body, tiles_per_seq=seq // tm)
    return pl.pallas_call(
        body,
        grid=(m // tm, nb),
        in_specs=[
            pl.BlockSpec((tm, k), lambda i, j: (i, 0)),
            pl.BlockSpec((None, tn, k), lambda i, j: (layer, j + off, 0)),
            pl.BlockSpec((SSD_CONV, tn), lambda i, j: (0, j)),
            pl.BlockSpec((1, tn), lambda i, j: (0, j)),
        ],
        out_specs=pl.BlockSpec((tm, tn), lambda i, j: (i, j)),
        out_shape=jax.ShapeDtypeStruct((m, n_cols), BF16),
        scratch_shapes=[pltpu.VMEM((nb, CONV_HIST, tn), F32)],
        compiler_params=_params("arbitrary", "arbitrary"),
        name="in_proj_conv",
    )(a, w3, conv_w, conv_b)


def _in_proj_tail_body(a_ref, wt_ref, g_ref, dt_ref):
    tn = wt_ref.shape[0]
    for rows in _row_blocks(a_ref.shape[0]):
        res = _dot_nt(a_ref[rows, :], wt_ref[...])
        g_ref[rows, :] = res.astype(g_ref.dtype)
        dt_ref[rows, :] = res[:, tn - LANES:]


def _in_proj_tail(a, wt_tail, tm, tn):
    m, k = a.shape
    n = wt_tail.shape[0]
    assert n % tn == 0 and tn % LANES == 0
    return pl.pallas_call(
        _in_proj_tail_body,
        grid=(m // tm, n // tn),
        in_specs=[
            pl.BlockSpec((tm, k), lambda i, j: (i, 0)),
            pl.BlockSpec((tn, k), lambda i, j: (j, 0)),
        ],
        out_specs=[
            pl.BlockSpec((tm, tn), lambda i, j: (i, j)),
            pl.BlockSpec((tm, LANES), lambda i, j: (i, 0)),
        ],
        out_shape=[jax.ShapeDtypeStruct((m, n), BF16), jax.ShapeDtypeStruct((m, LANES), F32)],
        compiler_params=_params("parallel", "arbitrary"),
        name="in_proj_tail",
    )(a, wt_tail)


def _norm_matmul_body(x_ref, nw_ref, w_ref, o_ref, xn_ref):
    @pl.when(pl.program_id(1) == 0)
    def _():
        xn_ref[...] = _rms(x_ref[...], nw_ref[...]).astype(BF16)

    o_ref[...] = _dot(xn_ref[...], w_ref[...].astype(BF16)).astype(o_ref.dtype)


def _norm_matmul(x, nw, w, out_dtype, tm, tn, name):
    m, k = x.shape
    n = w.shape[1]
    return pl.pallas_call(
        _norm_matmul_body,
        grid=(m // tm, n // tn),
        in_specs=[
            pl.BlockSpec((tm, k), lambda i, j: (i, 0)),
            pl.BlockSpec((1, k), lambda i, j: (0, 0)),
            pl.BlockSpec((k, tn), lambda i, j: (0, j)),
        ],
        out_specs=pl.BlockSpec((tm, tn), lambda i, j: (i, j)),
        out_shape=jax.ShapeDtypeStruct((m, n), out_dtype),
        scratch_shapes=[pltpu.VMEM((tm, k), BF16)],
        compiler_params=_params("parallel", "arbitrary"),
        name=name,
    )(x, nw, w)


def _retention_tables():
    idx = np.arange(CHUNK, dtype=np.float32)
    rel = idx[:, None] - idx[None, :]
    lg = _LOG_G[:, None, None]
    scale = np.float32(RET_DK ** -0.5)
    decay = np.where(rel[None] >= 0, np.exp(np.maximum(rel, 0.0)[None] * lg), 0.0) * scale
    xi = np.exp((idx + 1.0)[None, :, None] * lg) * scale
    zeta = np.exp((CHUNK - 1.0 - idx)[None, :, None] * lg)
    wide = (RET_HEADS, CHUNK, RET_DK)
    return (decay.astype(np.float32), np.broadcast_to(xi, wide).astype(np.float32),
            np.broadcast_to(zeta, wide).astype(np.float32))


def _retention_body(q_ref, k_ref, v_ref, g_ref, cos_ref, sin_ref, dec_ref, xi_ref, zeta_ref,
                    nw_ref, o_ref, state_ref):
    @pl.when(pl.program_id(1) == 0)
    def _():
        state_ref[...] = jnp.zeros_like(state_ref)

    for cc, h in itertools.product(range(RET_STEP_CHUNKS), range(RET_HEADS)):
        rows = slice(cc * CHUNK, (cc + 1) * CHUNK)
        cos = cos_ref[rows, :]
        sin = sin_ref[rows, :]
        q = q_ref[rows, h * RET_DK:(h + 1) * RET_DK].astype(F32)
        k = k_ref[rows, h * RET_DK:(h + 1) * RET_DK].astype(F32)
        v = v_ref[rows, h * RET_DV:(h + 1) * RET_DV]
        qr = q * cos + pltpu.roll(q, RET_DK // 2, 1) * sin
        kr = k * cos + pltpu.roll(k, RET_DK // 2, 1) * sin
        s = _dot_nt(qr.astype(BF16), kr.astype(BF16)) * dec_ref[h]
        st = state_ref[h]
        lhs = jnp.concatenate([s.astype(BF16), (qr * xi_ref[h]).astype(BF16)], axis=1)
        rhs = jnp.concatenate([v, st.astype(BF16)], axis=0)
        y = _dot(lhs, rhs)
        kz_t = (kr * zeta_ref[h]).T.astype(BF16)
        state_ref[h] = float(_RET_CHUNK_DECAY[h]) * st + _dot(kz_t, v)
        mu = jnp.mean(y, axis=-1, keepdims=True)
        yc = y - mu
        var = jnp.mean(yc * yc, axis=-1, keepdims=True)
        yn = yc * lax.rsqrt(var + EPS)
        hsl = slice(h * RET_DV, (h + 1) * RET_DV)
        o_ref[rows, hsl] = (yn * nw_ref[:, hsl] * _silu(g_ref[rows, hsl].astype(F32))).astype(BF16)


def _retention(proj, col_off, cos2, sin2, nw, bsz, seq):
    rows = RET_STEP_CHUNKS * CHUNK
    nc = seq // rows
    qk = RET_HEADS * RET_DK
    vd = RET_HEADS * RET_DV
    assert vd == 2 * qk and col_off % vd == 0 and seq % rows == 0
    oq = col_off // qk
    ov = col_off // vd
    decay, xi, zeta = _retention_tables()
    table = lambda n: pl.BlockSpec((RET_HEADS, CHUNK, n), lambda b, c: (0, 0, 0))
    return pl.pallas_call(
        _retention_body,
        grid=(bsz, nc),
        in_specs=[
            pl.BlockSpec((rows, qk), lambda b, c: (b * nc + c, oq)),
            pl.BlockSpec((rows, qk), lambda b, c: (b * nc + c, oq + 1)),
            pl.BlockSpec((rows, vd), lambda b, c: (b * nc + c, ov + 1)),
            pl.BlockSpec((rows, vd), lambda b, c: (b * nc + c, ov + 2)),
            pl.BlockSpec((rows, RET_DK), lambda b, c: (c, 0)),
            pl.BlockSpec((rows, RET_DK), lambda b, c: (c, 0)),
            table(CHUNK), table(RET_DK), table(RET_DK),
            pl.BlockSpec((1, vd), lambda b, c: (0, 0)),
        ],
        out_specs=pl.BlockSpec((rows, vd), lambda b, c: (b * nc + c, 0)),
        out_shape=jax.ShapeDtypeStruct((bsz * seq, vd), BF16),
        scratch_shapes=[pltpu.VMEM((RET_HEADS, RET_DK, RET_DV), F32)],
        compiler_params=_params("parallel", "arbitrary"),
        name="retention",
    )(proj, proj, proj, proj, cos2, sin2, decay, xi, zeta, nw)


def _ssd_body(z_ref, xbc_ref, dt_ref, dtb_ref, alog_ref, dsk_ref, nw_ref,
              o_ref, prev_ref, *, inner, heads_per_group):
    c = CHUNK
    n_sub = z_ref.shape[0] // c
    gw = heads_per_group * SSD_HEAD_DIM

    @pl.when(pl.program_id(1) == 0)
    def _():
        prev_ref[...] = jnp.zeros_like(prev_ref)

    row = lax.broadcasted_iota(jnp.int32, (c, c), 0)
    col = lax.broadcasted_iota(jnp.int32, (c, c), 1)
    tri = row >= col
    lane_lo = lax.broadcasted_iota(jnp.int32, (c, LANES), 1) < SSD_HEAD_DIM
    neg_inf = jnp.float32(-jnp.inf)
    a = -jnp.exp(alog_ref[...])

    for cc in range(n_sub):
        rows = slice(cc * c, (cc + 1) * c)
        dtr = dt_ref[rows, :] + dtb_ref[...]
        dt = jnp.maximum(dtr, 0.0) + jnp.log1p(jnp.exp(-jnp.abs(dtr)))
        da = dt * a
        a_cs = jnp.dot(tri.astype(F32), da, precision=lax.Precision.HIGHEST,
                       preferred_element_type=F32)
        a_cs = a_cs * LOG2_E
        a_last = a_cs[c - 1:c, :]
        w1 = jnp.exp2(a_last - a_cs) * dt
        src_t = (a_cs - jnp.log2(dt)).T
        w1_t = w1.T

        for g in range(SSD_GROUPS):
            b_gb = xbc_ref[rows, inner + g * SSD_STATE: inner + (g + 1) * SSD_STATE]
            c_gb = xbc_ref[rows, inner + (SSD_GROUPS + g) * SSD_STATE:
                           inner + (SSD_GROUPS + g + 1) * SSD_STATE]
            cb = _dot_nt(c_gb, b_gb).astype(BF16)
            b_gt = b_gb.astype(F32).T.astype(BF16)
            y_pairs = []
            for pr in range(heads_per_group // 2):
                h0 = g * heads_per_group + 2 * pr
                psl = slice(h0 * SSD_HEAD_DIM, (h0 + 2) * SSD_HEAD_DIM)
                xs_b = xbc_ref[rows, psl]
                prev = prev_ref[:, psl]
                rhs = jnp.concatenate([xs_b, prev.astype(BF16)], axis=0)
                lhs, bws, cds = [], [], []
                for h in (h0, h0 + 1):
                    colb = jnp.broadcast_to(a_cs[:, h:h + 1], (c, c))
                    rowb = jnp.broadcast_to(src_t[h:h + 1, :], (c, c))
                    m = cb * jnp.exp2(jnp.where(tri, colb - rowb, neg_inf)).astype(BF16)
                    e = jnp.exp2(colb)
                    lhs.append(jnp.concatenate([m, e.astype(BF16) * c_gb], axis=1))
                    bws.append(b_gt * jnp.broadcast_to(w1_t[h:h + 1, :], (c, c)).astype(BF16))
                    cds.append(e[c - 1:c, :])
                ys = _dot(jnp.concatenate(lhs, axis=0), rhs)
                sts = _dot(jnp.concatenate(bws, axis=0), xs_b)
                y_pairs.append(jnp.where(lane_lo, ys[0:c, :], ys[c:2 * c, :]))
                cd = jnp.where(lane_lo[0:1, :], cds[0], cds[1])
                prev_ref[:, psl] = cd * prev + jnp.where(lane_lo, sts[0:c, :], sts[c:2 * c, :])
            gsl = slice(g * gw, (g + 1) * gw)
            y = jnp.concatenate(y_pairs, axis=1) + dsk_ref[:, gsl] * xbc_ref[rows, gsl].astype(F32)
            gy = y * _silu(z_ref[rows, gsl].astype(F32))
            ms = jnp.mean(gy * gy, axis=-1, keepdims=True)
            o_ref[rows, gsl] = (gy * lax.rsqrt(ms + EPS) * nw_ref[:, gsl]).astype(BF16)


def _ssd(p_z, z_off, p_xbc, inner, dt_raw, dtb, alog, dskip, nw, bsz, seq):
    rows = SSD_STEP_CHUNKS * CHUNK
    nc = seq // rows
    n_heads = inner // SSD_HEAD_DIM
    heads_per_group = n_heads // SSD_GROUPS
    conv_dim = p_xbc.shape[1]
    assert conv_dim == inner + 2 * SSD_GROUPS * SSD_STATE and SSD_STATE == CHUNK and seq % rows == 0
    assert n_heads <= LANES and heads_per_group % 2 == 0 and 2 * SSD_HEAD_DIM == LANES
    assert z_off % inner == 0
    zb = z_off // inner
    body = functools.partial(_ssd_body, inner=inner, heads_per_group=heads_per_group)
    row = lambda b, c: (b * nc + c, 0)
    fixed = lambda b, c: (0, 0)
    return pl.pallas_call(
        body,
        grid=(bsz, nc),
        in_specs=[
            pl.BlockSpec((rows, inner), lambda b, c: (b * nc + c, zb)),
            pl.BlockSpec((rows, conv_dim), row),
            pl.BlockSpec((rows, LANES), row),
            pl.BlockSpec((1, LANES), fixed),
            pl.BlockSpec((1, LANES), fixed),
            pl.BlockSpec((1, inner), fixed),
            pl.BlockSpec((1, inner), fixed),
        ],
        out_specs=pl.BlockSpec((rows, inner), row),
        out_shape=jax.ShapeDtypeStruct((bsz * seq, inner), BF16),
        scratch_shapes=[pltpu.VMEM((SSD_STATE, inner), F32)],
        compiler_params=_params("parallel", "arbitrary"),
        name="ssd",
    )(p_z, p_xbc, dt_raw, dtb, alog, dskip, nw)


def _merge_body(yr_ref, ys_ref, wr_ref, ws_ref, ga_ref, gb_ref, o_ref):
    for rows in _row_blocks(yr_ref.shape[0]):
        br = _dot(yr_ref[rows, :], wr_ref[...])
        bs = _dot(ys_ref[rows, :], ws_ref[...])
        ga = _sigmoid(ga_ref[rows, :].astype(F32))
        gb = _sigmoid(gb_ref[rows, :].astype(F32))
        o_ref[rows, :] = (ga * br + gb * bs).astype(o_ref.dtype)


def _merge(y_ret, y_ssd, w_ret, w_ssd, gates, tm, tn):
    m, kr = y_ret.shape
    ks = y_ssd.shape[1]
    n = w_ret.shape[1]
    nb = n // tn
    return pl.pallas_call(
        _merge_body,
        grid=(m // tm, nb),
        in_specs=[
            pl.BlockSpec((tm, kr), lambda i, j: (i, 0)),
            pl.BlockSpec((tm, ks), lambda i, j: (i, 0)),
            pl.BlockSpec((kr, tn), lambda i, j: (0, j)),
            pl.BlockSpec((ks, tn), lambda i, j: (0, j)),
            pl.BlockSpec((tm, tn), lambda i, j: (i, j)),
            pl.BlockSpec((tm, tn), lambda i, j: (i, j + nb)),
        ],
        out_specs=pl.BlockSpec((tm, tn), lambda i, j: (i, j)),
        out_shape=jax.ShapeDtypeStruct((m, n), BF16),
        compiler_params=_params("parallel", "arbitrary"),
        name="merge",
    )(y_ret, y_ssd, w_ret, w_ssd, gates, gates)


def _mix_xattn_body(m_ref, x_ref, k_ref, v_ref, wmix_ref, nw_ref, wq_ref, wo_ref, o_ref, xn_ref):
    d = x_ref.shape[1]
    hd = d // XA_HEADS
    scale = hd ** -0.5
    o_ref[...] = x_ref[...] + _dot(m_ref[...], wmix_ref[...])
    xn_ref[...] = _rms(o_ref[...], nw_ref[...]).astype(BF16)
    for h in range(XA_HEADS):
        sl = slice(h * hd, (h + 1) * hd)
        q = _dot(xn_ref[...], wq_ref[:, sl]).astype(BF16)
        s = _dot_nt(q, k_ref[:, sl]) * scale
        s = s - jnp.max(s, axis=-1, keepdims=True)
        p = jnp.exp(s)
        p = p / jnp.sum(p, axis=-1, keepdims=True)
        o_h = _dot(p.astype(BF16), v_ref[:, sl]).astype(BF16)
        o_ref[...] += _dot(o_h, wo_ref[sl, :])


def _mix_xattn(merged, x, kv, w_mix, nw, w_q, w_o, bsz, seq, mem_len, tq):
    d = x.shape[1]
    nq = seq // tq
    tile = lambda b, i: (b * nq + i, 0)
    fixed = lambda b, i: (0, 0)
    resident = pl.Buffered(1)
    return pl.pallas_call(
        _mix_xattn_body,
        grid=(bsz, nq),
        in_specs=[
            pl.BlockSpec((tq, d), tile),
            pl.BlockSpec((tq, d), tile),
            pl.BlockSpec((mem_len, d), lambda b, i: (b, 0)),
            pl.BlockSpec((mem_len, d), lambda b, i: (b, 1)),
            pl.BlockSpec((d, d), fixed, pipeline_mode=resident),
            pl.BlockSpec((1, d), fixed),
            pl.BlockSpec((d, d), fixed, pipeline_mode=resident),
            pl.BlockSpec((d, d), fixed, pipeline_mode=resident),
        ],
        out_specs=pl.BlockSpec((tq, d), tile),
        out_shape=jax.ShapeDtypeStruct((bsz * seq, d), F32),
        scratch_shapes=[pltpu.VMEM((tq, d), BF16)],
        compiler_params=_params("parallel", "arbitrary"),
        name="mix_xattn",
    )(merged, x, kv, kv, w_mix, nw, w_q, w_o)


def _mlp_body(h_ref, nw_ref, w1_ref, w2_ref, fw_ref, o_ref, xn_ref):
    f = pl.program_id(1)
    blocks = _row_blocks(h_ref.shape[0])

    @pl.when(f == 0)
    def _():
        for rows in blocks:
            hh = h_ref[rows, :]
            xn_ref[rows, :] = _rms(hh, nw_ref[...]).astype(BF16)
            o_ref[rows, :] = hh

    w1b = w1_ref[...].astype(BF16)
    w2b = w2_ref[...].astype(BF16)
    for rows in blocks:
        u = _dot(xn_ref[rows, :], w1b)
        u = jnp.square(jnp.maximum(u, 0.0)).astype(BF16)
        o_ref[rows, :] += _dot(u, w2b)

    @pl.when(f == pl.num_programs(1) - 1)
    def _():
        for rows in blocks:
            o_ref[rows, :] = _rms(o_ref[rows, :], fw_ref[...])


def _mlp(h, nw, w1, w2, fw, tm, tf):
    m, d = h.shape
    dff = w1.shape[1]
    once = pl.Buffered(1)
    return pl.pallas_call(
        _mlp_body,
        grid=(m // tm, dff // tf),
        in_specs=[
            pl.BlockSpec((tm, d), lambda i, f: (i, 0)),
            pl.BlockSpec((1, d), lambda i, f: (0, 0)),
            pl.BlockSpec((d, tf), lambda i, f: (0, f)),
            pl.BlockSpec((tf, d), lambda i, f: (f, 0)),
            pl.BlockSpec((1, d), lambda i, f: (0, 0)),
        ],
        out_specs=pl.BlockSpec((tm, d), lambda i, f: (i, 0)),
        out_shape=jax.ShapeDtypeStruct((m, d), F32),
        scratch_shapes=[pltpu.VMEM((tm, d), BF16)],
        compiler_params=_params("parallel", "arbitrary"),
        name="mlp",
    )(h, nw, w1, w2, fw)


def _rope_tables(seq, dk):
    half = dk // 2
    pos = jnp.arange(seq, dtype=F32)
    inv = jnp.exp(-math.log(ROPE_BASE) * jnp.arange(half, dtype=F32) / half)
    ang = pos[:, None] * inv[None, :]
    cos, sin = jnp.cos(ang), jnp.sin(ang)
    return jnp.concatenate([cos, cos], axis=1), jnp.concatenate([-sin, sin], axis=1)


def kernel(x, mem, norm_mix_w, w_in, conv_w, conv_b, dt_bias, a_log, d_skip, ret_norm_w, ssd_norm_w, w_ret_out, w_ssd_out, w_mix_out, norm_xa_w, mem_norm_w, w_xq, w_xkv, w_xo, norm_ff_w, w_ff1, w_ff2, final_norm_w):
    bsz, seq, d = x.shape
    mem_len = mem.shape[1]
    depth = w_in.shape[0]
    n_tok = bsz * seq
    ret_qk = RET_HEADS * RET_DK
    ret_v = RET_HEADS * RET_DV
    inner = w_ssd_out.shape[1]
    conv_dim = conv_w.shape[2]
    n_heads = inner // SSD_HEAD_DIM
    ret_w = 2 * ret_qk + 2 * ret_v
    off_dt = ret_w + inner + conv_dim
    off_gate = off_dt + n_heads
    assert w_in.shape[2] == off_gate + 2 * d and depth == 1

    tn_main = 1024
    assert conv_dim % tn_main == 0 and ret_w % tn_main == 0 and inner % tn_main == 0
    shift = inner // tn_main
    z_off, ret_off = 0, inner

    cos2, sin2 = _rope_tables(seq, RET_DK)
    row = lambda v: v.reshape(1, -1)
    pad_heads = lambda v: jnp.pad(v.reshape(1, -1), ((0, 0), (0, LANES - n_heads)))

    w_in_t = jnp.swapaxes(w_in, 1, 2)

    h = x.reshape(n_tok, d)
    for l in range(depth):
        xn = _prenorm(h, row(norm_mix_w[l]), 512)
        proj = _in_proj_main(xn, w_in_t, l, 0, ret_w + inner, shift, 2048, tn_main)
        p_xbc = _in_proj_conv(xn, w_in_t, l, ret_w + inner, conv_dim, conv_w[l], row(conv_b[l]),
                              seq, 2048, tn_main)
        tail_rows = lax.optimization_barrier((w_in_t[l, off_gate:], w_in_t[l, off_dt:off_gate]))
        wt_tail = jnp.concatenate(
            [tail_rows[0].astype(BF16), tail_rows[1].astype(BF16),
             jnp.zeros((LANES - n_heads, d), BF16)], axis=0)
        gates, dt_raw = _in_proj_tail(xn, wt_tail, 2048, wt_tail.shape[0] // 3)
        y_ret = _retention(proj, ret_off, cos2, sin2, row(ret_norm_w[l]), bsz, seq)
        y_ssd = _ssd(proj, z_off, p_xbc, inner, dt_raw, pad_heads(dt_bias[l]), pad_heads(a_log[l]),
                     row(jnp.repeat(d_skip[l], SSD_HEAD_DIM)), row(ssd_norm_w[l]), bsz, seq)
        merged = _merge(y_ret, y_ssd, w_ret_out[l].astype(BF16), w_ssd_out[l].astype(BF16),
                        gates, 1024, 512)
        kv = _norm_matmul(mem.reshape(bsz * mem_len, d), row(mem_norm_w[l]), w_xkv[l],
                          BF16, bsz * mem_len, 1024, "xa_kv")
        h = _mix_xattn(merged, h, kv, w_mix_out[l].astype(BF16), row(norm_xa_w[l]),
                       w_xq[l].astype(BF16), w_xo[l].astype(BF16), bsz, seq, mem_len, 512)
        h = _mlp(h, row(norm_ff_w[l]), w_ff1[l], w_ff2[l], row(final_norm_w), 1024, 512)
    return h.reshape(bsz, seq, d)
```

```python
import functools
import itertools
import math

import numpy as np
import jax
import jax.numpy as jnp
from jax import lax
from jax.experimental import pallas as pl
from jax.experimental.pallas import tpu as pltpu

F32 = jnp.float32
BF16 = jnp.bfloat16
EPS = 1e-6

RET_HEADS = 8
RET_DK = 128
RET_DV = 256
CHUNK = 128
ROPE_BASE = 10000.0
SSD_HEAD_DIM = 64
SSD_GROUPS = 8
SSD_STATE = 128
SSD_CONV = 4
CONV_HIST = 8
XA_HEADS = 4

V7X_VMEM_BYTES = 64 * 1024 * 1024
VMEM_LIMIT = V7X_VMEM_BYTES - 4 * 1024 * 1024
LANES = 128
ROW_SPLIT = 512
CONV_ROW_SPLIT = 256
LOG2_E = math.log2(math.e)

PRENORM_ROWS = 512
IN_PROJ_TILE = (2048, 1024)
TAIL_ROWS = 2048
TAIL_COL_BLOCKS = 3
MERGE_TILE = (1024, 512)
KV_COLS = 1024
XATTN_ROWS = 512
MLP_TILE = (1024, 512)
RET_STEP_CHUNKS = 8
SSD_STEP_CHUNKS = 2

_LOG_G = np.log(1.0 - np.exp2(-5.0 - np.arange(RET_HEADS, dtype=np.float64)))
_RET_CHUNK_DECAY = np.exp(CHUNK * _LOG_G)


def _params(*sem):
    return pltpu.CompilerParams(dimension_semantics=sem, vmem_limit_bytes=VMEM_LIMIT)


def _rms(x, w):
    ms = jnp.mean(x * x, axis=-1, keepdims=True)
    return x * lax.rsqrt(ms + EPS) * w


def _sigmoid(x):
    return 0.5 + 0.5 * jnp.tanh(0.5 * x)


def _silu(x):
    h = 0.5 * x
    return h + h * jnp.tanh(h)


def _dot(a, b):
    return jnp.dot(a, b, preferred_element_type=F32)


def _dot_nt(a, b):
    return lax.dot_general(a, b, (((1,), (1,)), ((), ())), preferred_element_type=F32)


def _row_blocks(n_rows, step=ROW_SPLIT):
    step = min(step, n_rows)
    return [slice(r, r + step) for r in range(0, n_rows, step)]


def _prenorm_body(x_ref, nw_ref, o_ref):
    o_ref[...] = _rms(x_ref[...], nw_ref[...]).astype(o_ref.dtype)


def _prenorm(x, nw, tm):
    m, k = x.shape
    return pl.pallas_call(
        _prenorm_body,
        grid=(m // tm,),
        in_specs=[pl.BlockSpec((tm, k), lambda i: (i, 0)), pl.BlockSpec((1, k), lambda i: (0, 0))],
        out_specs=pl.BlockSpec((tm, k), lambda i: (i, 0)),
        out_shape=jax.ShapeDtypeStruct((m, k), BF16),
        compiler_params=_params("parallel"),
        name="prenorm",
    )(x, nw)


def _matmul_nt_body(a_ref, wt_ref, o_ref):
    wb = wt_ref[...].astype(BF16)
    for rows in _row_blocks(a_ref.shape[0]):
        o_ref[rows, :] = _dot_nt(a_ref[rows, :], wb).astype(o_ref.dtype)


def _in_proj_main(a, w3, layer, row_off, n_cols, shift, tm, tn):
    m, k = a.shape
    nb = n_cols // tn
    off = row_off // tn
    assert n_cols % tn == 0 and m % tm == 0 and row_off % tn == 0
    return pl.pallas_call(
        _matmul_nt_body,
        grid=(m // tm, nb),
        in_specs=[
            pl.BlockSpec((tm, k), lambda i, j: (i, 0)),
            pl.BlockSpec((None, tn, k), lambda i, j: (layer, j + off, 0)),
        ],
        out_specs=pl.BlockSpec((tm, tn), lambda i, j: (i, (j + shift) % nb)),
        out_shape=jax.ShapeDtypeStruct((m, n_cols), BF16),
        compiler_params=_params("parallel", "arbitrary"),
        name="in_proj_main",
    )(a, w3)


def _in_proj_conv_body(a_ref, wt_ref, cw_ref, cb_ref, o_ref, carry_ref, *, tiles_per_seq):
    i = pl.program_id(0)
    j = pl.program_id(1)
    tn = wt_ref.shape[0]

    @pl.when(i % tiles_per_seq == 0)
    def _():
        carry_ref[j] = jnp.zeros((CONV_HIST, tn), F32)

    wb = wt_ref[...].astype(BF16)
    cwh = 0.5 * cw_ref[...]
    cbh = 0.5 * cb_ref[...]
    hist = carry_ref[j]
    sub = lax.broadcasted_iota(jnp.int32, (CONV_HIST, tn), 0)
    for rows in _row_blocks(a_ref.shape[0], CONV_ROW_SPLIT):
        res = _dot_nt(a_ref[rows, :], wb)
        acc = cbh + cwh[SSD_CONV - 1:SSD_CONV, :] * res
        for k in range(SSD_CONV - 1):
            dist = SSD_CONV - 1 - k
            rolled = pltpu.roll(res, dist, 0)
            head = jnp.where(sub < dist, pltpu.roll(hist, dist, 0), rolled[0:CONV_HIST, :])
            shifted = jnp.concatenate([head, rolled[CONV_HIST:, :]], axis=0)
            acc = acc + cwh[k:k + 1, :] * shifted
        o_ref[rows, :] = (acc + acc * jnp.tanh(acc)).astype(o_ref.dtype)
        hist = res[res.shape[0] - CONV_HIST:, :]
    carry_ref[j] = hist


def _in_proj_conv(a, w3, layer, row_off, n_cols, conv_w, conv_b, seq, tm, tn):
    m, k = a.shape
    nb = n_cols // tn
    off = row_off // tn
    assert n_cols % tn == 0 and m % tm == 0 and row_off % tn == 0 and seq % tm == 0
    assert SSD_CONV - 1 <= CONV_HIST
    body = functools.partial(_in_proj_conv_body, tiles_per_seq=seq // tm)
    return pl.pallas_call(
        body,
        grid=(m // tm, nb),
        in_specs=[
            pl.BlockSpec((tm, k), lambda i, j: (i, 0)),
            pl.BlockSpec((None, tn, k), lambda i, j: (layer, j + off, 0)),
            pl.BlockSpec((SSD_CONV, tn), lambda i, j: (0, j)),
            pl.BlockSpec((1, tn), lambda i, j: (0, j)),
        ],
        out_specs=pl.BlockSpec((tm, tn), lambda i, j: (i, j)),
        out_shape=jax.ShapeDtypeStruct((m, n_cols), BF16),
        scratch_shapes=[pltpu.VMEM((nb, CONV_HIST, tn), F32)],
        compiler_params=_params("arbitrary", "arbitrary"),
        name="in_proj_conv",
    )(a, w3, conv_w, conv_b)


def _in_proj_tail_body(a_ref, wt_ref, g_ref, dt_ref):
    tn = wt_ref.shape[0]
    for rows in _row_blocks(a_ref.shape[0]):
        res = _dot_nt(a_ref[rows, :], wt_ref[...])
        g_ref[rows, :] = res.astype(g_ref.dtype)
        dt_ref[rows, :] = res[:, tn - LANES:]


def _in_proj_tail(a, wt_tail, tm, tn):
    m, k = a.shape
    n = wt_tail.shape[0]
    assert n % tn == 0 and tn % LANES == 0
    return pl.pallas_call(
        _in_proj_tail_body,
        grid=(m // tm, n // tn),
        in_specs=[
            pl.BlockSpec((tm, k), lambda i, j: (i, 0)),
            pl.BlockSpec((tn, k), lambda i, j: (j, 0)),
        ],
        out_specs=[
            pl.BlockSpec((tm, tn), lambda i, j: (i, j)),
            pl.BlockSpec((tm, LANES), lambda i, j: (i, 0)),
        ],
        out_shape=[jax.ShapeDtypeStruct((m, n), BF16), jax.ShapeDtypeStruct((m, LANES), F32)],
        compiler_params=_params("parallel", "arbitrary"),
        name="in_proj_tail",
    )(a, wt_tail)


def _norm_matmul_body(x_ref, nw_ref, w_ref, o_ref, xn_ref):
    @pl.when(pl.program_id(1) == 0)
    def _():
        xn_ref[...] = _rms(x_ref[...], nw_ref[...]).astype(BF16)

    o_ref[...] = _dot(xn_ref[...], w_ref[...].astype(BF16)).astype(o_ref.dtype)


def _norm_matmul(x, nw, w, out_dtype, tm, tn, name):
    m, k = x.shape
    n = w.shape[1]
    return pl.pallas_call(
        _norm_matmul_body,
        grid=(m // tm, n // tn),
        in_specs=[
            pl.BlockSpec((tm, k), lambda i, j: (i, 0)),
            pl.BlockSpec((1, k), lambda i, j: (0, 0)),
            pl.BlockSpec((k, tn), lambda i, j: (0, j)),
        ],
        out_specs=pl.BlockSpec((tm, tn), lambda i, j: (i, j)),
        out_shape=jax.ShapeDtypeStruct((m, n), out_dtype),
        scratch_shapes=[pltpu.VMEM((tm, k), BF16)],
        compiler_params=_params("parallel", "arbitrary"),
        name=name,
    )(x, nw, w)


def _retention_tables():
    idx = np.arange(CHUNK, dtype=np.float64)
    rel = idx[:, None] - idx[None, :]
    lg = _LOG_G[:, None, None]
    scale = RET_DK ** -0.5
    decay = np.where(rel[None] >= 0, np.exp(np.maximum(rel, 0.0)[None] * lg), 0.0) * scale
    xi = np.exp((idx + 1.0)[None, :, None] * lg) * scale
    zeta = np.exp((CHUNK - 1.0 - idx)[None, :, None] * lg)
    wide = (RET_HEADS, CHUNK, RET_DK)
    return (decay.astype(np.float32), np.broadcast_to(xi, wide).astype(np.float32),
            np.broadcast_to(zeta, wide).astype(np.float32))


def _retention_body(q_ref, k_ref, v_ref, g_ref, cos_ref, sin_ref, dec_ref, xi_ref, zeta_ref,
                    nw_ref, o_ref, state_ref):
    @pl.when(pl.program_id(1) == 0)
    def _():
        state_ref[...] = jnp.zeros_like(state_ref)

    for cc, h in itertools.product(range(RET_STEP_CHUNKS), range(RET_HEADS)):
        rows = slice(cc * CHUNK, (cc + 1) * CHUNK)
        cos = cos_ref[rows, :]
        sin = sin_ref[rows, :]
        q = q_ref[rows, h * RET_DK:(h + 1) * RET_DK].astype(F32)
        k = k_ref[rows, h * RET_DK:(h + 1) * RET_DK].astype(F32)
        v = v_ref[rows, h * RET_DV:(h + 1) * RET_DV]
        qr = q * cos + pltpu.roll(q, RET_DK // 2, 1) * sin
        kr = k * cos + pltpu.roll(k, RET_DK // 2, 1) * sin
        s = _dot_nt(qr.astype(BF16), kr.astype(BF16)) * dec_ref[h]
        st = state_ref[h]
        lhs = jnp.concatenate([s.astype(BF16), (qr * xi_ref[h]).astype(BF16)], axis=1)
        rhs = jnp.concatenate([v, st.astype(BF16)], axis=0)
        y = _dot(lhs, rhs)
        kz_t = (kr * zeta_ref[h]).T.astype(BF16)
        state_ref[h] = float(_RET_CHUNK_DECAY[h]) * st + _dot(kz_t, v)
        mu = jnp.mean(y, axis=-1, keepdims=True)
        yc = y - mu
        var = jnp.mean(yc * yc, axis=-1, keepdims=True)
        yn = yc * lax.rsqrt(var + EPS)
        hsl = slice(h * RET_DV, (h + 1) * RET_DV)
        o_ref[rows, hsl] = (yn * nw_ref[:, hsl] * _silu(g_ref[rows, hsl].astype(F32))).astype(BF16)


def _retention(proj, col_off, cos2, sin2, nw, bsz, seq):
    rows = RET_STEP_CHUNKS * CHUNK
    nc = seq // rows
    qk = RET_HEADS * RET_DK
    vd = RET_HEADS * RET_DV
    assert vd == 2 * qk and col_off % vd == 0 and seq % rows == 0
    oq = col_off // qk
    ov = col_off // vd
    decay, xi, zeta = _retention_tables()
    table = lambda n: pl.BlockSpec((RET_HEADS, CHUNK, n), lambda b, c: (0, 0, 0))
    return pl.pallas_call(
        _retention_body,
        grid=(bsz, nc),
        in_specs=[
            pl.BlockSpec((rows, qk), lambda b, c: (b * nc + c, oq)),
            pl.BlockSpec((rows, qk), lambda b, c: (b * nc + c, oq + 1)),
            pl.BlockSpec((rows, vd), lambda b, c: (b * nc + c, ov + 1)),
            pl.BlockSpec((rows, vd), lambda b, c: (b * nc + c, ov + 2)),
            pl.BlockSpec((rows, RET_DK), lambda b, c: (c, 0)),
            pl.BlockSpec((rows, RET_DK), lambda b, c: (c, 0)),
            table(CHUNK), table(RET_DK), table(RET_DK),
            pl.BlockSpec((1, vd), lambda b, c: (0, 0)),
        ],
        out_specs=pl.BlockSpec((rows, vd), lambda b, c: (b * nc + c, 0)),
        out_shape=jax.ShapeDtypeStruct((bsz * seq, vd), BF16),
        scratch_shapes=[pltpu.VMEM((RET_HEADS, RET_DK, RET_DV), F32)],
        compiler_params=_params("parallel", "arbitrary"),
        name="retention",
    )(proj, proj, proj, proj, cos2, sin2, decay, xi, zeta, nw)


def _ssd_body(z_ref, xbc_ref, dt_ref, dtb_ref, alog_ref, dsk_ref, nw_ref,
              o_ref, prev_ref, *, inner, heads_per_group):
    c = CHUNK
    n_sub = z_ref.shape[0] // c
    gw = heads_per_group * SSD_HEAD_DIM

    @pl.when(pl.program_id(1) == 0)
    def _():
        prev_ref[...] = jnp.zeros_like(prev_ref)

    row = lax.broadcasted_iota(jnp.int32, (c, c), 0)
    col = lax.broadcasted_iota(jnp.int32, (c, c), 1)
    tri = row >= col
    lane_lo = lax.broadcasted_iota(jnp.int32, (c, LANES), 1) < SSD_HEAD_DIM
    neg_inf = jnp.float32(-jnp.inf)
    a = -jnp.exp(alog_ref[...])

    for cc in range(n_sub):
        rows = slice(cc * c, (cc + 1) * c)
        dtr = dt_ref[rows, :] + dtb_ref[...]
        dt = jnp.maximum(dtr, 0.0) + jnp.log1p(jnp.exp(-jnp.abs(dtr)))
        da = dt * a
        a_cs = jnp.dot(tri.astype(F32), da, precision=lax.Precision.HIGHEST,
                       preferred_element_type=F32)
        a_cs = a_cs * LOG2_E
        a_last = a_cs[c - 1:c, :]
        w1 = jnp.exp2(a_last - a_cs) * dt
        src_t = (a_cs - jnp.log2(dt)).T
        w1_t = w1.T

        for g in range(SSD_GROUPS):
            b_gb = xbc_ref[rows, inner + g * SSD_STATE: inner + (g + 1) * SSD_STATE]
            c_gb = xbc_ref[rows, inner + (SSD_GROUPS + g) * SSD_STATE:
                           inner + (SSD_GROUPS + g + 1) * SSD_STATE]
            cb = _dot_nt(c_gb, b_gb).astype(BF16)
            b_gt = b_gb.astype(F32).T.astype(BF16)
            y_pairs = []
            for pr in range(heads_per_group // 2):
                h0 = g * heads_per_group + 2 * pr
                psl = slice(h0 * SSD_HEAD_DIM, (h0 + 2) * SSD_HEAD_DIM)
                xs_b = xbc_ref[rows, psl]
                prev = prev_ref[:, psl]
                rhs = jnp.concatenate([xs_b, prev.astype(BF16)], axis=0)
                lhs, bws, cds = [], [], []
                for h in (h0, h0 + 1):
                    colb = jnp.broadcast_to(a_cs[:, h:h + 1], (c, c))
                    rowb = jnp.broadcast_to(src_t[h:h + 1, :], (c, c))
                    m = cb * jnp.exp2(jnp.where(tri, colb - rowb, neg_inf)).astype(BF16)
                    e = jnp.exp2(colb)
                    lhs.append(jnp.concatenate([m, e.astype(BF16) * c_gb], axis=1))
                    bws.append(b_gt * jnp.broadcast_to(w1_t[h:h + 1, :], (c, c)).astype(BF16))
                    cds.append(e[c - 1:c, :])
                ys = _dot(jnp.concatenate(lhs, axis=0), rhs)
                sts = _dot(jnp.concatenate(bws, axis=0), xs_b)
                y_pairs.append(jnp.where(lane_lo, ys[0:c, :], ys[c:2 * c, :]))
                cd = jnp.where(lane_lo[0:1, :], cds[0], cds[1])
                prev_ref[:, psl] = cd * prev + jnp.where(lane_lo, sts[0:c, :], sts[c:2 * c, :])
            gsl = slice(g * gw, (g + 1) * gw)
            y = jnp.concatenate(y_pairs, axis=1) + dsk_ref[:, gsl] * xbc_ref[rows, gsl].astype(F32)
            gy = y * _silu(z_ref[rows, gsl].astype(F32))
            ms = jnp.mean(gy * gy, axis=-1, keepdims=True)
            o_ref[rows, gsl] = (gy * lax.rsqrt(ms + EPS) * nw_ref[:, gsl]).astype(BF16)


def _ssd(p_z, z_off, p_xbc, inner, dt_raw, dtb, alog, dskip, nw, bsz, seq):
    rows = SSD_STEP_CHUNKS * CHUNK
    nc = seq // rows
    n_heads = inner // SSD_HEAD_DIM
    heads_per_group = n_heads // SSD_GROUPS
    conv_dim = p_xbc.shape[1]
    assert conv_dim == inner + 2 * SSD_GROUPS * SSD_STATE and SSD_STATE == CHUNK and seq % rows == 0
    assert n_heads <= LANES and heads_per_group % 2 == 0 and 2 * SSD_HEAD_DIM == LANES
    assert z_off % inner == 0
    zb = z_off // inner
    body = functools.partial(_ssd_body, inner=inner, heads_per_group=heads_per_group)
    row = lambda b, c: (b * nc + c, 0)
    fixed = lambda b, c: (0, 0)
    return pl.pallas_call(
        body,
        grid=(bsz, nc),
        in_specs=[
            pl.BlockSpec((rows, inner), lambda b, c: (b * nc + c, zb)),
            pl.BlockSpec((rows, conv_dim), row),
            pl.BlockSpec((rows, LANES), row),
            pl.BlockSpec((1, LANES), fixed),
            pl.BlockSpec((1, LANES), fixed),
            pl.BlockSpec((1, inner), fixed),
            pl.BlockSpec((1, inner), fixed),
        ],
        out_specs=pl.BlockSpec((rows, inner), row),
        out_shape=jax.ShapeDtypeStruct((bsz * seq, inner), BF16),
        scratch_shapes=[pltpu.VMEM((SSD_STATE, inner), F32)],
        compiler_params=_params("parallel", "arbitrary"),
        name="ssd",
    )(p_z, p_xbc, dt_raw, dtb, alog, dskip, nw)


def _merge_body(yr_ref, ys_ref, wr_ref, ws_ref, ga_ref, gb_ref, o_ref):
    for rows in _row_blocks(yr_ref.shape[0]):
        br = _dot(yr_ref[rows, :], wr_ref[...])
        bs = _dot(ys_ref[rows, :], ws_ref[...])
        ga = _sigmoid(ga_ref[rows, :].astype(F32))
        gb = _sigmoid(gb_ref[rows, :].astype(F32))
        o_ref[rows, :] = (ga * br + gb * bs).astype(o_ref.dtype)


def _merge(y_ret, y_ssd, w_ret, w_ssd, gates, tm, tn):
    m, kr = y_ret.shape
    ks = y_ssd.shape[1]
    n = w_ret.shape[1]
    nb = n // tn
    return pl.pallas_call(
        _merge_body,
        grid=(m // tm, nb),
        in_specs=[
            pl.BlockSpec((tm, kr), lambda i, j: (i, 0)),
            pl.BlockSpec((tm, ks), lambda i, j: (i, 0)),
            pl.BlockSpec((kr, tn), lambda i, j: (0, j)),
            pl.BlockSpec((ks, tn), lambda i, j: (0, j)),
            pl.BlockSpec((tm, tn), lambda i, j: (i, j)),
            pl.BlockSpec((tm, tn), lambda i, j: (i, j + nb)),
        ],
        out_specs=pl.BlockSpec((tm, tn), lambda i, j: (i, j)),
        out_shape=jax.ShapeDtypeStruct((m, n), BF16),
        compiler_params=_params("parallel", "arbitrary"),
        name="merge",
    )(y_ret, y_ssd, w_ret, w_ssd, gates, gates)


def _mix_xattn_body(m_ref, x_ref, k_ref, v_ref, wmix_ref, nw_ref, wq_ref, wo_ref, o_ref, xn_ref):
    d = x_ref.shape[1]
    hd = d // XA_HEADS
    scale = hd ** -0.5
    o_ref[...] = x_ref[...] + _dot(m_ref[...], wmix_ref[...])
    xn_ref[...] = _rms(o_ref[...], nw_ref[...]).astype(BF16)
    for h in range(XA_HEADS):
        sl = slice(h * hd, (h + 1) * hd)
        q = _dot(xn_ref[...], wq_ref[:, sl]).astype(BF16)
        s = _dot_nt(q, k_ref[:, sl]) * scale
        s = s - jnp.max(s, axis=-1, keepdims=True)
        p = jnp.exp(s)
        p = p / jnp.sum(p, axis=-1, keepdims=True)
        o_h = _dot(p.astype(BF16), v_ref[:, sl]).astype(BF16)
        o_ref[...] += _dot(o_h, wo_ref[sl, :])


def _mix_xattn(merged, x, kv, w_mix, nw, w_q, w_o, bsz, seq, mem_len, tq):
    d = x.shape[1]
    nq = seq // tq
    tile = lambda b, i: (b * nq + i, 0)
    fixed = lambda b, i: (0, 0)
    resident = pl.Buffered(1)
    return pl.pallas_call(
        _mix_xattn_body,
        grid=(bsz, nq),
        in_specs=[
            pl.BlockSpec((tq, d), tile),
            pl.BlockSpec((tq, d), tile),
            pl.BlockSpec((mem_len, d), lambda b, i: (b, 0)),
            pl.BlockSpec((mem_len, d), lambda b, i: (b, 1)),
            pl.BlockSpec((d, d), fixed, pipeline_mode=resident),
            pl.BlockSpec((1, d), fixed),
            pl.BlockSpec((d, d), fixed, pipeline_mode=resident),
            pl.BlockSpec((d, d), fixed, pipeline_mode=resident),
        ],
        out_specs=pl.BlockSpec((tq, d), tile),
        out_shape=jax.ShapeDtypeStruct((bsz * seq, d), F32),
        scratch_shapes=[pltpu.VMEM((tq, d), BF16)],
        compiler_params=_params("parallel", "arbitrary"),
        name="mix_xattn",
    )(merged, x, kv, kv, w_mix, nw, w_q, w_o)


def _mlp_body(h_ref, nw_ref, w1_ref, w2_ref, fw_ref, o_ref, xn_ref):
    f = pl.program_id(1)
    blocks = _row_blocks(h_ref.shape[0])

    @pl.when(f == 0)
    def _():
        for rows in blocks:
            hh = h_ref[rows, :]
            xn_ref[rows, :] = _rms(hh, nw_ref[...]).astype(BF16)
            o_ref[rows, :] = hh

    w1b = w1_ref[...].astype(BF16)
    w2b = w2_ref[...].astype(BF16)
    for rows in blocks:
        u = _dot(xn_ref[rows, :], w1b)
        u = jnp.square(jnp.maximum(u, 0.0)).astype(BF16)
        o_ref[rows, :] += _dot(u, w2b)

    @pl.when(f == pl.num_programs(1) - 1)
    def _():
        for rows in blocks:
            o_ref[rows, :] = _rms(o_ref[rows, :], fw_ref[...])


def _mlp(h, nw, w1, w2, fw, tm, tf):
    m, d = h.shape
    dff = w1.shape[1]
    return pl.pallas_call(
        _mlp_body,
        grid=(m // tm, dff // tf),
        in_specs=[
            pl.BlockSpec((tm, d), lambda i, f: (i, 0)),
            pl.BlockSpec((1, d), lambda i, f: (0, 0)),
            pl.BlockSpec((d, tf), lambda i, f: (0, f)),
            pl.BlockSpec((tf, d), lambda i, f: (f, 0)),
            pl.BlockSpec((1, d), lambda i, f: (0, 0)),
        ],
        out_specs=pl.BlockSpec((tm, d), lambda i, f: (i, 0)),
        out_shape=jax.ShapeDtypeStruct((m, d), F32),
        scratch_shapes=[pltpu.VMEM((tm, d), BF16)],
        compiler_params=_params("parallel", "arbitrary"),
        name="mlp",
    )(h, nw, w1, w2, fw)


def _rope_tables(seq, dk):
    half = dk // 2
    inv = np.exp(-math.log(ROPE_BASE) * np.arange(half, dtype=np.float64) / half)
    ang = np.arange(seq, dtype=np.float64)[:, None] * inv[None, :]
    cos, sin = np.cos(ang).astype(np.float32), np.sin(ang).astype(np.float32)
    return np.concatenate([cos, cos], axis=1), np.concatenate([-sin, sin], axis=1)


def kernel(x, mem, norm_mix_w, w_in, conv_w, conv_b, dt_bias, a_log, d_skip, ret_norm_w, ssd_norm_w, w_ret_out, w_ssd_out, w_mix_out, norm_xa_w, mem_norm_w, w_xq, w_xkv, w_xo, norm_ff_w, w_ff1, w_ff2, final_norm_w):
    bsz, seq, d = x.shape
    mem_len = mem.shape[1]
    depth = w_in.shape[0]
    n_tok = bsz * seq
    ret_qk = RET_HEADS * RET_DK
    ret_v = RET_HEADS * RET_DV
    inner = w_ssd_out.shape[1]
    conv_dim = conv_w.shape[2]
    n_heads = inner // SSD_HEAD_DIM
    ret_w = 2 * ret_qk + 2 * ret_v
    off_dt = ret_w + inner + conv_dim
    off_gate = off_dt + n_heads
    assert w_in.shape[2] == off_gate + 2 * d and depth == 1

    tm_main, tn_main = IN_PROJ_TILE
    assert conv_dim % tn_main == 0 and ret_w % tn_main == 0 and inner % tn_main == 0
    shift = inner // tn_main
    z_off, ret_off = 0, inner

    cos2, sin2 = _rope_tables(seq, RET_DK)
    row = lambda v: v.reshape(1, -1)
    pad_heads = lambda v: jnp.pad(v.reshape(1, -1), ((0, 0), (0, LANES - n_heads)))

    w_in_t = jnp.swapaxes(w_in, 1, 2)

    h = x.reshape(n_tok, d)
    for l in range(depth):
        xn = _prenorm(h, row(norm_mix_w[l]), PRENORM_ROWS)
        proj = _in_proj_main(xn, w_in_t, l, 0, ret_w + inner, shift, tm_main, tn_main)
        p_xbc = _in_proj_conv(xn, w_in_t, l, ret_w + inner, conv_dim, conv_w[l], row(conv_b[l]),
                              seq, tm_main, tn_main)
        tail_rows = lax.optimization_barrier((w_in_t[l, off_gate:], w_in_t[l, off_dt:off_gate]))
        wt_tail = jnp.concatenate(
            [tail_rows[0].astype(BF16), tail_rows[1].astype(BF16),
             jnp.zeros((LANES - n_heads, d), BF16)], axis=0)
        gates, dt_raw = _in_proj_tail(xn, wt_tail, TAIL_ROWS, wt_tail.shape[0] // TAIL_COL_BLOCKS)
        y_ret = _retention(proj, ret_off, cos2, sin2, row(ret_norm_w[l]), bsz, seq)
        y_ssd = _ssd(proj, z_off, p_xbc, inner, dt_raw, pad_heads(dt_bias[l]), pad_heads(a_log[l]),
                     row(jnp.repeat(d_skip[l], SSD_HEAD_DIM)), row(ssd_norm_w[l]), bsz, seq)
        merged = _merge(y_ret, y_ssd, w_ret_out[l].astype(BF16), w_ssd_out[l].astype(BF16),
                        gates, *MERGE_TILE)
        kv = _norm_matmul(mem.reshape(bsz * mem_len, d), row(mem_norm_w[l]), w_xkv[l],
                          BF16, bsz * mem_len, KV_COLS, "xa_kv")
        h = _mix_xattn(merged, h, kv, w_mix_out[l].astype(BF16), row(norm_xa_w[l]),
                       w_xq[l].astype(BF16), w_xo[l].astype(BF16), bsz, seq, mem_len, XATTN_ROWS)
        h = _mlp(h, row(norm_ff_w[l]), w_ff1[l], w_ff2[l], row(final_norm_w), *MLP_TILE)
    return h.reshape(bsz, seq, d)
```

```python
import functools
import itertools
import math

import numpy as np
import jax
import jax.numpy as jnp
from jax import lax
from jax.experimental import pallas as pl
from jax.experimental.pallas import tpu as pltpu

F32 = jnp.float32
BF16 = jnp.bfloat16
EPS = 1e-6

RET_HEADS = 8
RET_DK = 128
RET_DV = 256
CHUNK = 128
ROPE_BASE = 10000.0
SSD_HEAD_DIM = 64
SSD_GROUPS = 8
SSD_STATE = 128
SSD_CONV = 4
CONV_HIST = 8
XA_HEADS = 4

V7X_VMEM_BYTES = 64 * 1024 * 1024
VMEM_LIMIT = V7X_VMEM_BYTES - 4 * 1024 * 1024
LANES = 128
ROW_SPLIT = 512
CONV_ROW_SPLIT = 256
LOG2_E = math.log2(math.e)

PRENORM_ROWS = 512
IN_PROJ_TILE = (2048, 1024)
CAST_SLABS = 32
TAIL_ROWS = 2048
TAIL_COL_BLOCKS = 3
MERGE_TILE = (1024, 512)
KV_COLS = 1024
XATTN_ROWS = 512
MLP_TILE = (1024, 512)
RET_STEP_CHUNKS = 8
SSD_STEP_CHUNKS = 2

_LOG_G = np.log(1.0 - np.exp2(-5.0 - np.arange(RET_HEADS, dtype=np.float64)))
_RET_CHUNK_DECAY = np.exp(CHUNK * _LOG_G)


def _params(*sem):
    return pltpu.CompilerParams(dimension_semantics=sem, vmem_limit_bytes=VMEM_LIMIT)


def _rms(x, w):
    ms = jnp.mean(x * x, axis=-1, keepdims=True)
    return x * lax.rsqrt(ms + EPS) * w


def _sigmoid(x):
    return 0.5 + 0.5 * jnp.tanh(0.5 * x)


def _silu(x):
    h = 0.5 * x
    return h + h * jnp.tanh(h)


def _dot(a, b):
    return jnp.dot(a, b, preferred_element_type=F32)


def _dot_nt(a, b):
    return lax.dot_general(a, b, (((1,), (1,)), ((), ())), preferred_element_type=F32)


def _row_blocks(n_rows, step=ROW_SPLIT):
    step = min(step, n_rows)
    return [slice(r, r + step) for r in range(0, n_rows, step)]


def _prenorm_body(x_ref, nw_ref, o_ref):
    o_ref[...] = _rms(x_ref[...], nw_ref[...]).astype(o_ref.dtype)


def _prenorm(x, nw, tm):
    m, k = x.shape
    return pl.pallas_call(
        _prenorm_body,
        grid=(m // tm,),
        in_specs=[pl.BlockSpec((tm, k), lambda i: (i, 0)), pl.BlockSpec((1, k), lambda i: (0, 0))],
        out_specs=pl.BlockSpec((tm, k), lambda i: (i, 0)),
        out_shape=jax.ShapeDtypeStruct((m, k), BF16),
        compiler_params=_params("parallel"),
        name="prenorm",
    )(x, nw)


def _matmul_nt_body(a_ref, wt_ref, *refs):
    n_cast = (len(refs) - 1) // 2
    o_ref = refs[n_cast]
    wb = wt_ref[...].astype(BF16)
    for rows in _row_blocks(a_ref.shape[0]):
        o_ref[rows, :] = _dot_nt(a_ref[rows, :], wb).astype(o_ref.dtype)
    for src_ref, dst_ref in zip(refs[:n_cast], refs[n_cast + 1:]):
        dst_ref[...] = src_ref[...].astype(dst_ref.dtype)


def _in_proj_main(a, w3, layer, row_off, n_cols, shift, to_bf16, tm, tn):
    m, k = a.shape
    nb = n_cols // tn
    off = row_off // tn
    n_steps = (m // tm) * nb
    assert n_cols % tn == 0 and m % tm == 0 and row_off % tn == 0
    assert CAST_SLABS <= n_steps and all(w.shape[0] % CAST_SLABS == 0 for w in to_bf16)
    slab = lambda i, j: (jnp.minimum(i * nb + j, CAST_SLABS - 1), 0)
    cast_specs = [pl.BlockSpec((w.shape[0] // CAST_SLABS, w.shape[1]), slab) for w in to_bf16]
    outs = pl.pallas_call(
        _matmul_nt_body,
        grid=(m // tm, nb),
        in_specs=[
            pl.BlockSpec((tm, k), lambda i, j: (i, 0)),
            pl.BlockSpec((None, tn, k), lambda i, j: (layer, j + off, 0)),
        ] + cast_specs,
        out_specs=[pl.BlockSpec((tm, tn), lambda i, j: (i, (j + shift) % nb))] + cast_specs,
        out_shape=[jax.ShapeDtypeStruct((m, n_cols), BF16)]
        + [jax.ShapeDtypeStruct(w.shape, BF16) for w in to_bf16],
        compiler_params=_params("arbitrary", "arbitrary"),
        name="in_proj_main",
    )(a, w3, *to_bf16)
    return outs[0], outs[1:]


def _in_proj_conv_body(a_ref, wt_ref, cw_ref, cb_ref, o_ref, carry_ref, *, tiles_per_seq):
    i = pl.program_id(0)
    j = pl.program_id(1)
    tn = wt_ref.shape[0]

    @pl.when(i % tiles_per_seq == 0)
    def _():
        carry_ref[j] = jnp.zeros((CONV_HIST, tn), F32)

    wb = wt_ref[...].astype(BF16)
    cwh = 0.5 * cw_ref[...]
    cbh = 0.5 * cb_ref[...]
    hist = carry_ref[j]
    sub = lax.broadcasted_iota(jnp.int32, (CONV_HIST, tn), 0)
    for rows in _row_blocks(a_ref.shape[0], CONV_ROW_SPLIT):
        res = _dot_nt(a_ref[rows, :], wb)
        acc = cbh + cwh[SSD_CONV - 1:SSD_CONV, :] * res
        for k in range(SSD_CONV - 1):
            dist = SSD_CONV - 1 - k
            rolled = pltpu.roll(res, dist, 0)
            head = jnp.where(sub < dist, pltpu.roll(hist, dist, 0), rolled[0:CONV_HIST, :])
            shifted = jnp.concatenate([head, rolled[CONV_HIST:, :]], axis=0)
            acc = acc + cwh[k:k + 1, :] * shifted
        o_ref[rows, :] = (acc + acc * jnp.tanh(acc)).astype(o_ref.dtype)
        hist = res[res.shape[0] - CONV_HIST:, :]
    carry_ref[j] = hist


def _in_proj_conv(a, w3, layer, row_off, n_cols, conv_w, conv_b, seq, tm, tn):
    m, k = a.shape
    nb = n_cols // tn
    off = row_off // tn
    assert n_cols % tn == 0 and m % tm == 0 and row_off % tn == 0 and seq % tm == 0
    assert SSD_CONV - 1 <= CONV_HIST
    body = functools.partial(_in_proj_conv_body, tiles_per_seq=seq // tm)
    return pl.pallas_call(
        body,
        grid=(m // tm, nb),
        in_specs=[
            pl.BlockSpec((tm, k), lambda i, j: (i, 0)),
            pl.BlockSpec((None, tn, k), lambda i, j: (layer, j + off, 0)),
            pl.BlockSpec((SSD_CONV, tn), lambda i, j: (0, j)),
            pl.BlockSpec((1, tn), lambda i, j: (0, j)),
        ],
        out_specs=pl.BlockSpec((tm, tn), lambda i, j: (i, j)),
        out_shape=jax.ShapeDtypeStruct((m, n_cols), BF16),
        scratch_shapes=[pltpu.VMEM((nb, CONV_HIST, tn), F32)],
        compiler_params=_params("arbitrary", "arbitrary"),
        name="in_proj_conv",
    )(a, w3, conv_w, conv_b)


def _in_proj_tail_body(a_ref, wt_ref, g_ref, dt_ref):
    tn = wt_ref.shape[0]
    for rows in _row_blocks(a_ref.shape[0]):
        res = _dot_nt(a_ref[rows, :], wt_ref[...])
        g_ref[rows, :] = res.astype(g_ref.dtype)
        dt_ref[rows, :] = res[:, tn - LANES:]


def _in_proj_tail(a, wt_tail, tm, tn):
    m, k = a.shape
    n = wt_tail.shape[0]
    assert n % tn == 0 and tn % LANES == 0
    return pl.pallas_call(
        _in_proj_tail_body,
        grid=(m // tm, n // tn),
        in_specs=[
            pl.BlockSpec((tm, k), lambda i, j: (i, 0)),
            pl.BlockSpec((tn, k), lambda i, j: (j, 0)),
        ],
        out_specs=[
            pl.BlockSpec((tm, tn), lambda i, j: (i, j)),
            pl.BlockSpec((tm, LANES), lambda i, j: (i, 0)),
        ],
        out_shape=[jax.ShapeDtypeStruct((m, n), BF16), jax.ShapeDtypeStruct((m, LANES), F32)],
        compiler_params=_params("parallel", "arbitrary"),
        name="in_proj_tail",
    )(a, wt_tail)


def _norm_matmul_body(x_ref, nw_ref, w_ref, o_ref, xn_ref):
    @pl.when(pl.program_id(1) == 0)
    def _():
        xn_ref[...] = _rms(x_ref[...], nw_ref[...]).astype(BF16)

    o_ref[...] = _dot(xn_ref[...], w_ref[...].astype(BF16)).astype(o_ref.dtype)


def _norm_matmul(x, nw, w, out_dtype, tm, tn, name):
    m, k = x.shape
    n = w.shape[1]
    return pl.pallas_call(
        _norm_matmul_body,
        grid=(m // tm, n // tn),
        in_specs=[
            pl.BlockSpec((tm, k), lambda i, j: (i, 0)),
            pl.BlockSpec((1, k), lambda i, j: (0, 0)),
            pl.BlockSpec((k, tn), lambda i, j: (0, j)),
        ],
        out_specs=pl.BlockSpec((tm, tn), lambda i, j: (i, j)),
        out_shape=jax.ShapeDtypeStruct((m, n), out_dtype),
        scratch_shapes=[pltpu.VMEM((tm, k), BF16)],
        compiler_params=_params("parallel", "arbitrary"),
        name=name,
    )(x, nw, w)


def _retention_tables():
    idx = np.arange(CHUNK, dtype=np.float64)
    rel = idx[:, None] - idx[None, :]
    lg = _LOG_G[:, None, None]
    scale = RET_DK ** -0.5
    decay = np.where(rel[None] >= 0, np.exp(np.maximum(rel, 0.0)[None] * lg), 0.0) * scale
    xi = np.exp((idx + 1.0)[None, :, None] * lg) * scale
    zeta = np.exp((CHUNK - 1.0 - idx)[None, :, None] * lg)
    wide = (RET_HEADS, CHUNK, RET_DK)
    return (decay.astype(np.float32), np.broadcast_to(xi, wide).astype(np.float32),
            np.broadcast_to(zeta, wide).astype(np.float32))


def _retention_body(q_ref, k_ref, v_ref, g_ref, cos_ref, sin_ref, dec_ref, xi_ref, zeta_ref,
                    nw_ref, o_ref, state_ref):
    @pl.when(pl.program_id(1) == 0)
    def _():
        state_ref[...] = jnp.zeros_like(state_ref)

    for cc, h in itertools.product(range(RET_STEP_CHUNKS), range(RET_HEADS)):
        rows = slice(cc * CHUNK, (cc + 1) * CHUNK)
        cos = cos_ref[rows, :]
        sin = sin_ref[rows, :]
        q = q_ref[rows, h * RET_DK:(h + 1) * RET_DK].astype(F32)
        k = k_ref[rows, h * RET_DK:(h + 1) * RET_DK].astype(F32)
        v = v_ref[rows, h * RET_DV:(h + 1) * RET_DV]
        qr = q * cos + pltpu.roll(q, RET_DK // 2, 1) * sin
        kr = k * cos + pltpu.roll(k, RET_DK // 2, 1) * sin
        s = _dot_nt(qr.astype(BF16), kr.astype(BF16)) * dec_ref[h]
        st = state_ref[h]
        lhs = jnp.concatenate([s.astype(BF16), (qr * xi_ref[h]).astype(BF16)], axis=1)
        rhs = jnp.concatenate([v, st.astype(BF16)], axis=0)
        y = _dot(lhs, rhs)
        kz_t = (kr * zeta_ref[h]).T.astype(BF16)
        state_ref[h] = float(_RET_CHUNK_DECAY[h]) * st + _dot(kz_t, v)
        mu = jnp.mean(y, axis=-1, keepdims=True)
        yc = y - mu
        var = jnp.mean(yc * yc, axis=-1, keepdims=True)
        yn = yc * lax.rsqrt(var + EPS)
        hsl = slice(h * RET_DV, (h + 1) * RET_DV)
        o_ref[rows, hsl] = (yn * nw_ref[:, hsl] * _silu(g_ref[rows, hsl].astype(F32))).astype(BF16)


def _retention(proj, col_off, cos2, sin2, nw, bsz, seq):
    rows = RET_STEP_CHUNKS * CHUNK
    nc = seq // rows
    qk = RET_HEADS * RET_DK
    vd = RET_HEADS * RET_DV
    assert vd == 2 * qk and col_off % vd == 0 and seq % rows == 0
    oq = col_off // qk
    ov = col_off // vd
    decay, xi, zeta = _retention_tables()
    table = lambda n: pl.BlockSpec((RET_HEADS, CHUNK, n), lambda b, c: (0, 0, 0))
    return pl.pallas_call(
        _retention_body,
        grid=(bsz, nc),
        in_specs=[
            pl.BlockSpec((rows, qk), lambda b, c: (b * nc + c, oq)),
            pl.BlockSpec((rows, qk), lambda b, c: (b * nc + c, oq + 1)),
            pl.BlockSpec((rows, vd), lambda b, c: (b * nc + c, ov + 1)),
            pl.BlockSpec((rows, vd), lambda b, c: (b * nc + c, ov + 2)),
            pl.BlockSpec((rows, RET_DK), lambda b, c: (c, 0)),
            pl.BlockSpec((rows, RET_DK), lambda b, c: (c, 0)),
            table(CHUNK), table(RET_DK), table(RET_DK),
            pl.BlockSpec((1, vd), lambda b, c: (0, 0)),
        ],
        out_specs=pl.BlockSpec((rows, vd), lambda b, c: (b * nc + c, 0)),
        out_shape=jax.ShapeDtypeStruct((bsz * seq, vd), BF16),
        scratch_shapes=[pltpu.VMEM((RET_HEADS, RET_DK, RET_DV), F32)],
        compiler_params=_params("parallel", "arbitrary"),
        name="retention",
    )(proj, proj, proj, proj, cos2, sin2, decay, xi, zeta, nw)


def _ssd_body(z_ref, xbc_ref, dt_ref, dtb_ref, alog_ref, dsk_ref, nw_ref,
              o_ref, prev_ref, *, inner, heads_per_group):
    c = CHUNK
    n_sub = z_ref.shape[0] // c
    gw = heads_per_group * SSD_HEAD_DIM

    @pl.when(pl.program_id(1) == 0)
    def _():
        prev_ref[...] = jnp.zeros_like(prev_ref)

    row = lax.broadcasted_iota(jnp.int32, (c, c), 0)
    col = lax.broadcasted_iota(jnp.int32, (c, c), 1)
    tri = row >= col
    lane_lo = lax.broadcasted_iota(jnp.int32, (c, LANES), 1) < SSD_HEAD_DIM
    neg_inf = jnp.float32(-jnp.inf)
    a = -jnp.exp(alog_ref[...])

    for cc in range(n_sub):
        rows = slice(cc * c, (cc + 1) * c)
        dtr = dt_ref[rows, :] + dtb_ref[...]
        dt = jnp.maximum(dtr, 0.0) + jnp.log1p(jnp.exp(-jnp.abs(dtr)))
        da = dt * a
        a_cs = jnp.dot(tri.astype(F32), da, precision=lax.Precision.HIGHEST,
                       preferred_element_type=F32)
        a_cs = a_cs * LOG2_E
        a_last = a_cs[c - 1:c, :]
        w1 = jnp.exp2(a_last - a_cs) * dt
        src_t = (a_cs - jnp.log2(dt)).T
        w1_t = w1.T

        for g in range(SSD_GROUPS):
            b_gb = xbc_ref[rows, inner + g * SSD_STATE: inner + (g + 1) * SSD_STATE]
            c_gb = xbc_ref[rows, inner + (SSD_GROUPS + g) * SSD_STATE:
                           inner + (SSD_GROUPS + g + 1) * SSD_STATE]
            cb = _dot_nt(c_gb, b_gb).astype(BF16)
            b_gt = b_gb.astype(F32).T.astype(BF16)
            y_pairs = []
            for pr in range(heads_per_group // 2):
                h0 = g * heads_per_group + 2 * pr
                psl = slice(h0 * SSD_HEAD_DIM, (h0 + 2) * SSD_HEAD_DIM)
                xs_b = xbc_ref[rows, psl]
                prev = prev_ref[:, psl]
                rhs = jnp.concatenate([xs_b, prev.astype(BF16)], axis=0)
                lhs, bws, cds = [], [], []
                for h in (h0, h0 + 1):
                    colb = jnp.broadcast_to(a_cs[:, h:h + 1], (c, c))
                    rowb = jnp.broadcast_to(src_t[h:h + 1, :], (c, c))
                    m = cb * jnp.exp2(jnp.where(tri, colb - rowb, neg_inf)).astype(BF16)
                    e = jnp.exp2(colb)
                    lhs.append(jnp.concatenate([m, e.astype(BF16) * c_gb], axis=1))
                    bws.append(b_gt * jnp.broadcast_to(w1_t[h:h + 1, :], (c, c)).astype(BF16))
                    cds.append(e[c - 1:c, :])
                ys = _dot(jnp.concatenate(lhs, axis=0), rhs)
                sts = _dot(jnp.concatenate(bws, axis=0), xs_b)
                y_pairs.append(jnp.where(lane_lo, ys[0:c, :], ys[c:2 * c, :]))
                cd = jnp.where(lane_lo[0:1, :], cds[0], cds[1])
                prev_ref[:, psl] = cd * prev + jnp.where(lane_lo, sts[0:c, :], sts[c:2 * c, :])
            gsl = slice(g * gw, (g + 1) * gw)
            y = jnp.concatenate(y_pairs, axis=1) + dsk_ref[:, gsl] * xbc_ref[rows, gsl].astype(F32)
            gy = y * _silu(z_ref[rows, gsl].astype(F32))
            ms = jnp.mean(gy * gy, axis=-1, keepdims=True)
            o_ref[rows, gsl] = (gy * lax.rsqrt(ms + EPS) * nw_ref[:, gsl]).astype(BF16)


def _ssd(p_z, z_off, p_xbc, inner, dt_raw, dtb, alog, dskip, nw, bsz, seq):
    rows = SSD_STEP_CHUNKS * CHUNK
    nc = seq // rows
    n_heads = inner // SSD_HEAD_DIM
    heads_per_group = n_heads // SSD_GROUPS
    conv_dim = p_xbc.shape[1]
    assert conv_dim == inner + 2 * SSD_GROUPS * SSD_STATE and SSD_STATE == CHUNK and seq % rows == 0
    assert n_heads <= LANES and heads_per_group % 2 == 0 and 2 * SSD_HEAD_DIM == LANES
    assert z_off % inner == 0
    zb = z_off // inner
    body = functools.partial(_ssd_body, inner=inner, heads_per_group=heads_per_group)
    row = lambda b, c: (b * nc + c, 0)
    fixed = lambda b, c: (0, 0)
    return pl.pallas_call(
        body,
        grid=(bsz, nc),
        in_specs=[
            pl.BlockSpec((rows, inner), lambda b, c: (b * nc + c, zb)),
            pl.BlockSpec((rows, conv_dim), row),
            pl.BlockSpec((rows, LANES), row),
            pl.BlockSpec((1, LANES), fixed),
            pl.BlockSpec((1, LANES), fixed),
            pl.BlockSpec((1, inner), fixed),
            pl.BlockSpec((1, inner), fixed),
        ],
        out_specs=pl.BlockSpec((rows, inner), row),
        out_shape=jax.ShapeDtypeStruct((bsz * seq, inner), BF16),
        scratch_shapes=[pltpu.VMEM((SSD_STATE, inner), F32)],
        compiler_params=_params("parallel", "arbitrary"),
        name="ssd",
    )(p_z, p_xbc, dt_raw, dtb, alog, dskip, nw)


def _merge_body(yr_ref, ys_ref, wr_ref, ws_ref, ga_ref, gb_ref, o_ref):
    for rows in _row_blocks(yr_ref.shape[0]):
        br = _dot(yr_ref[rows, :], wr_ref[...])
        bs = _dot(ys_ref[rows, :], ws_ref[...])
        ga = _sigmoid(ga_ref[rows, :].astype(F32))
        gb = _sigmoid(gb_ref[rows, :].astype(F32))
        o_ref[rows, :] = (ga * br + gb * bs).astype(o_ref.dtype)


def _merge(y_ret, y_ssd, w_ret, w_ssd, gates, tm, tn):
    m, kr = y_ret.shape
    ks = y_ssd.shape[1]
    n = w_ret.shape[1]
    nb = n // tn
    return pl.pallas_call(
        _merge_body,
        grid=(m // tm, nb),
        in_specs=[
            pl.BlockSpec((tm, kr), lambda i, j: (i, 0)),
            pl.BlockSpec((tm, ks), lambda i, j: (i, 0)),
            pl.BlockSpec((kr, tn), lambda i, j: (0, j)),
            pl.BlockSpec((ks, tn), lambda i, j: (0, j)),
            pl.BlockSpec((tm, tn), lambda i, j: (i, j)),
            pl.BlockSpec((tm, tn), lambda i, j: (i, j + nb)),
        ],
        out_specs=pl.BlockSpec((tm, tn), lambda i, j: (i, j)),
        out_shape=jax.ShapeDtypeStruct((m, n), BF16),
        compiler_params=_params("parallel", "arbitrary"),
        name="merge",
    )(y_ret, y_ssd, w_ret, w_ssd, gates, gates)


def _mix_xattn_body(m_ref, x_ref, k_ref, v_ref, wmix_ref, nw_ref, wq_ref, wo_ref, o_ref, xn_ref):
    d = x_ref.shape[1]
    hd = d // XA_HEADS
    scale = hd ** -0.5
    o_ref[...] = x_ref[...] + _dot(m_ref[...], wmix_ref[...])
    xn_ref[...] = _rms(o_ref[...], nw_ref[...]).astype(BF16)
    for h in range(XA_HEADS):
        sl = slice(h * hd, (h + 1) * hd)
        q = _dot(xn_ref[...], wq_ref[:, sl]).astype(BF16)
        s = _dot_nt(q, k_ref[:, sl]) * scale
        s = s - jnp.max(s, axis=-1, keepdims=True)
        p = jnp.exp(s)
        p = p / jnp.sum(p, axis=-1, keepdims=True)
        o_h = _dot(p.astype(BF16), v_ref[:, sl]).astype(BF16)
        o_ref[...] += _dot(o_h, wo_ref[sl, :])


def _mix_xattn(merged, x, kv, w_mix, nw, w_q, w_o, bsz, seq, mem_len, tq):
    d = x.shape[1]
    nq = seq // tq
    tile = lambda b, i: (b * nq + i, 0)
    fixed = lambda b, i: (0, 0)
    resident = pl.Buffered(1)
    return pl.pallas_call(
        _mix_xattn_body,
        grid=(bsz, nq),
        in_specs=[
            pl.BlockSpec((tq, d), tile),
            pl.BlockSpec((tq, d), tile),
            pl.BlockSpec((mem_len, d), lambda b, i: (b, 0)),
            pl.BlockSpec((mem_len, d), lambda b, i: (b, 1)),
            pl.BlockSpec((d, d), fixed, pipeline_mode=resident),
            pl.BlockSpec((1, d), fixed),
            pl.BlockSpec((d, d), fixed, pipeline_mode=resident),
            pl.BlockSpec((d, d), fixed, pipeline_mode=resident),
        ],
        out_specs=pl.BlockSpec((tq, d), tile),
        out_shape=jax.ShapeDtypeStruct((bsz * seq, d), F32),
        scratch_shapes=[pltpu.VMEM((tq, d), BF16)],
        compiler_params=_params("parallel", "arbitrary"),
        name="mix_xattn",
    )(merged, x, kv, kv, w_mix, nw, w_q, w_o)


def _mlp_body(h_ref, nw_ref, w1_ref, w2_ref, fw_ref, o_ref, xn_ref):
    f = pl.program_id(1)
    blocks = _row_blocks(h_ref.shape[0])

    @pl.when(f == 0)
    def _():
        for rows in blocks:
            hh = h_ref[rows, :]
            xn_ref[rows, :] = _rms(hh, nw_ref[...]).astype(BF16)
            o_ref[rows, :] = hh

    w1b = w1_ref[...].astype(BF16)
    w2b = w2_ref[...].astype(BF16)
    for rows in blocks:
        u = _dot(xn_ref[rows, :], w1b)
        u = jnp.square(jnp.maximum(u, 0.0)).astype(BF16)
        o_ref[rows, :] += _dot(u, w2b)

    @pl.when(f == pl.num_programs(1) - 1)
    def _():
        for rows in blocks:
            o_ref[rows, :] = _rms(o_ref[rows, :], fw_ref[...])


def _mlp(h, nw, w1, w2, fw, tm, tf):
    m, d = h.shape
    dff = w1.shape[1]
    return pl.pallas_call(
        _mlp_body,
        grid=(m // tm, dff // tf),
        in_specs=[
            pl.BlockSpec((tm, d), lambda i, f: (i, 0)),
            pl.BlockSpec((1, d), lambda i, f: (0, 0)),
            pl.BlockSpec((d, tf), lambda i, f: (0, f)),
            pl.BlockSpec((tf, d), lambda i, f: (f, 0)),
            pl.BlockSpec((1, d), lambda i, f: (0, 0)),
        ],
        out_specs=pl.BlockSpec((tm, d), lambda i, f: (i, 0)),
        out_shape=jax.ShapeDtypeStruct((m, d), F32),
        scratch_shapes=[pltpu.VMEM((tm, d), BF16)],
        compiler_params=_params("parallel", "arbitrary"),
        name="mlp",
    )(h, nw, w1, w2, fw)


def _rope_tables(seq, dk):
    half = dk // 2
    inv = np.exp(-math.log(ROPE_BASE) * np.arange(half, dtype=np.float64) / half)
    ang = np.arange(seq, dtype=np.float64)[:, None] * inv[None, :]
    cos, sin = np.cos(ang).astype(np.float32), np.sin(ang).astype(np.float32)
    return np.concatenate([cos, cos], axis=1), np.concatenate([-sin, sin], axis=1)


def kernel(x, mem, norm_mix_w, w_in, conv_w, conv_b, dt_bias, a_log, d_skip, ret_norm_w, ssd_norm_w, w_ret_out, w_ssd_out, w_mix_out, norm_xa_w, mem_norm_w, w_xq, w_xkv, w_xo, norm_ff_w, w_ff1, w_ff2, final_norm_w):
    bsz, seq, d = x.shape
    mem_len = mem.shape[1]
    depth = w_in.shape[0]
    n_tok = bsz * seq
    ret_qk = RET_HEADS * RET_DK
    ret_v = RET_HEADS * RET_DV
    inner = w_ssd_out.shape[1]
    conv_dim = conv_w.shape[2]
    n_heads = inner // SSD_HEAD_DIM
    ret_w = 2 * ret_qk + 2 * ret_v
    off_dt = ret_w + inner + conv_dim
    off_gate = off_dt + n_heads
    assert w_in.shape[2] == off_gate + 2 * d and depth == 1

    tm_main, tn_main = IN_PROJ_TILE
    assert conv_dim % tn_main == 0 and ret_w % tn_main == 0 and inner % tn_main == 0
    shift = inner // tn_main
    z_off, ret_off = 0, inner

    cos2, sin2 = _rope_tables(seq, RET_DK)
    row = lambda v: v.reshape(1, -1)
    pad_heads = lambda v: jnp.pad(v.reshape(1, -1), ((0, 0), (0, LANES - n_heads)))

    w_in_t = jnp.swapaxes(w_in, 1, 2)

    h = x.reshape(n_tok, d)
    for l in range(depth):
        xn = _prenorm(h, row(norm_mix_w[l]), PRENORM_ROWS)
        proj, (w_ret_b, w_ssd_b, w_mix_b, w_xq_b, w_xo_b) = _in_proj_main(
            xn, w_in_t, l, 0, ret_w + inner, shift,
            (w_ret_out[l], w_ssd_out[l], w_mix_out[l], w_xq[l], w_xo[l]), tm_main, tn_main)
        p_xbc = _in_proj_conv(xn, w_in_t, l, ret_w + inner, conv_dim, conv_w[l], row(conv_b[l]),
                              seq, tm_main, tn_main)
        tail_rows = lax.optimization_barrier((w_in_t[l, off_gate:], w_in_t[l, off_dt:off_gate]))
        wt_tail = jnp.concatenate(
            [tail_rows[0].astype(BF16), tail_rows[1].astype(BF16),
             jnp.zeros((LANES - n_heads, d), BF16)], axis=0)
        gates, dt_raw = _in_proj_tail(xn, wt_tail, TAIL_ROWS, wt_tail.shape[0] // TAIL_COL_BLOCKS)
        y_ret = _retention(proj, ret_off, cos2, sin2, row(ret_norm_w[l]), bsz, seq)
        y_ssd = _ssd(proj, z_off, p_xbc, inner, dt_raw, pad_heads(dt_bias[l]), pad_heads(a_log[l]),
                     row(jnp.repeat(d_skip[l], SSD_HEAD_DIM)), row(ssd_norm_w[l]), bsz, seq)
        merged = _merge(y_ret, y_ssd, w_ret_b, w_ssd_b, gates, *MERGE_TILE)
        kv = _norm_matmul(mem.reshape(bsz * mem_len, d), row(mem_norm_w[l]), w_xkv[l],
                          BF16, bsz * mem_len, KV_COLS, "xa_kv")
        h = _mix_xattn(merged, h, kv, w_mix_b, row(norm_xa_w[l]), w_xq_b, w_xo_b,
                       bsz, seq, mem_len, XATTN_ROWS)
        h = _mlp(h, row(norm_ff_w[l]), w_ff1[l], w_ff2[l], row(final_norm_w), *MLP_TILE)
    return h.reshape(bsz, seq, d)
```

```python
import functools
import itertools
import math

import numpy as np
import jax
import jax.numpy as jnp
from jax import lax
from jax.experimental import pallas as pl
from jax.experimental.pallas import tpu as pltpu

F32 = jnp.float32
BF16 = jnp.bfloat16
EPS = 1e-6

RET_HEADS = 8
RET_DK = 128
RET_DV = 256
CHUNK = 128
ROPE_BASE = 10000.0
SSD_HEAD_DIM = 64
SSD_GROUPS = 8
SSD_STATE = 128
SSD_CONV = 4
CONV_HIST = 8
XA_HEADS = 4

V7X_VMEM_BYTES = 64 * 1024 * 1024
VMEM_LIMIT = V7X_VMEM_BYTES - 4 * 1024 * 1024
LANES = 128
ROW_SPLIT = 512
CONV_ROW_SPLIT = 256
LOG2_E = math.log2(math.e)

PRENORM_ROWS = 512
IN_PROJ_TILE = (2048, 1024)
CAST_SLABS = 32
TAIL_ROWS = 2048
TAIL_COL_BLOCKS = 3
MERGE_TILE = (1024, 512)
KV_COLS = 1024
XATTN_ROWS = 512
MLP_TILE = (1024, 512)
RET_STEP_CHUNKS = 8
SSD_STEP_CHUNKS = 2

_LOG_G = np.log(1.0 - np.exp2(-5.0 - np.arange(RET_HEADS, dtype=np.float64)))
_RET_CHUNK_DECAY = np.exp(CHUNK * _LOG_G)


def _params(*sem):
    return pltpu.CompilerParams(dimension_semantics=sem, vmem_limit_bytes=VMEM_LIMIT)


def _rms(x, w):
    ms = jnp.mean(x * x, axis=-1, keepdims=True)
    return x * lax.rsqrt(ms + EPS) * w


def _sigmoid(x):
    return 0.5 + 0.5 * jnp.tanh(0.5 * x)


def _silu(x):
    h = 0.5 * x
    return h + h * jnp.tanh(h)


def _dot(a, b):
    return jnp.dot(a, b, preferred_element_type=F32)


def _dot_nt(a, b):
    return lax.dot_general(a, b, (((1,), (1,)), ((), ())), preferred_element_type=F32)


def _row_blocks(n_rows, step=ROW_SPLIT):
    step = min(step, n_rows)
    return [slice(r, r + step) for r in range(0, n_rows, step)]


def _prenorm_body(x_ref, nw_ref, o_ref):
    o_ref[...] = _rms(x_ref[...], nw_ref[...]).astype(o_ref.dtype)


def _prenorm(x, nw, tm):
    m, k = x.shape
    return pl.pallas_call(
        _prenorm_body,
        grid=(m // tm,),
        in_specs=[pl.BlockSpec((tm, k), lambda i: (i, 0)), pl.BlockSpec((1, k), lambda i: (0, 0))],
        out_specs=pl.BlockSpec((tm, k), lambda i: (i, 0)),
        out_shape=jax.ShapeDtypeStruct((m, k), BF16),
        compiler_params=_params("parallel"),
        name="prenorm",
    )(x, nw)


def _matmul_nt_body(a_ref, wt_ref, t0_ref, t1_ref, *refs, nb, tail_blocks):
    n_cast = (len(refs) - 2) // 2
    o_ref, tail_ref = refs[n_cast], refs[n_cast + 1]
    wb = wt_ref[...].astype(BF16)
    for rows in _row_blocks(a_ref.shape[0]):
        o_ref[rows, :] = _dot_nt(a_ref[rows, :], wb).astype(o_ref.dtype)
    for src_ref, dst_ref in zip(refs[:n_cast], refs[n_cast + 2:]):
        dst_ref[...] = src_ref[...].astype(dst_ref.dtype)
    slab = t0_ref.shape[0]
    step = pl.program_id(0) * nb + pl.program_id(1)
    tail_ref[0:slab, :] = t0_ref[...].astype(tail_ref.dtype)
    tail_ref[slab:2 * slab, :] = jnp.where(step >= tail_blocks - 1, 0.0, t1_ref[...]).astype(tail_ref.dtype)


def _in_proj_main(a, w3, layer, row_off, n_cols, shift, to_bf16, gate_row0, n_gate_rows,
                  dt_row0, slab, tm, tn):
    m, k = a.shape
    nb = n_cols // tn
    off = row_off // tn
    n_steps = (m // tm) * nb
    assert n_cols % tn == 0 and m % tm == 0 and row_off % tn == 0
    assert CAST_SLABS <= n_steps and all(w.shape[0] % CAST_SLABS == 0 for w in to_bf16)
    cast_slab = lambda i, j: (jnp.minimum(i * nb + j, CAST_SLABS - 1), 0)
    cast_specs = [pl.BlockSpec((w.shape[0] // CAST_SLABS, w.shape[1]), cast_slab) for w in to_bf16]
    n_gate = n_gate_rows // slab
    tail_blocks = (n_gate + 2) // 2
    assert gate_row0 % slab == 0 and dt_row0 % slab == 0 and n_gate_rows % slab == 0
    assert n_gate % 2 == 0 and tail_blocks <= n_steps
    tail_blk = lambda i, j: jnp.minimum(i * nb + j, tail_blocks - 1)

    def tail_src(t):
        def index(i, j):
            q = 2 * tail_blk(i, j) + t
            return (layer, jnp.where(q < n_gate, gate_row0 // slab + q, dt_row0 // slab), 0)
        return pl.BlockSpec((None, slab, k), index)

    body = functools.partial(_matmul_nt_body, nb=nb, tail_blocks=tail_blocks)
    outs = pl.pallas_call(
        body,
        grid=(m // tm, nb),
        in_specs=[
            pl.BlockSpec((tm, k), lambda i, j: (i, 0)),
            pl.BlockSpec((None, tn, k), lambda i, j: (layer, j + off, 0)),
            tail_src(0), tail_src(1),
        ] + cast_specs,
        out_specs=[pl.BlockSpec((tm, tn), lambda i, j: (i, (j + shift) % nb)),
                   pl.BlockSpec((2 * slab, k), lambda i, j: (tail_blk(i, j), 0))] + cast_specs,
        out_shape=[jax.ShapeDtypeStruct((m, n_cols), BF16),
                   jax.ShapeDtypeStruct((2 * slab * tail_blocks, k), BF16)]
        + [jax.ShapeDtypeStruct(w.shape, BF16) for w in to_bf16],
        compiler_params=_params("arbitrary", "arbitrary"),
        name="in_proj_main",
    )(a, w3, w3, w3, *to_bf16)
    return outs[0], outs[1], outs[2:]


def _in_proj_conv_body(a_ref, wt_ref, cw_ref, cb_ref, o_ref, carry_ref, *, tiles_per_seq):
    i = pl.program_id(0)
    j = pl.program_id(1)
    tn = wt_ref.shape[0]

    @pl.when(i % tiles_per_seq == 0)
    def _():
        carry_ref[j] = jnp.zeros((CONV_HIST, tn), F32)

    wb = wt_ref[...].astype(BF16)
    cwh = 0.5 * cw_ref[...]
    cbh = 0.5 * cb_ref[...]
    hist = carry_ref[j]
    sub = lax.broadcasted_iota(jnp.int32, (CONV_HIST, tn), 0)
    for rows in _row_blocks(a_ref.shape[0], CONV_ROW_SPLIT):
        res = _dot_nt(a_ref[rows, :], wb)
        acc = cbh + cwh[SSD_CONV - 1:SSD_CONV, :] * res
        for k in range(SSD_CONV - 1):
            dist = SSD_CONV - 1 - k
            rolled = pltpu.roll(res, dist, 0)
            head = jnp.where(sub < dist, pltpu.roll(hist, dist, 0), rolled[0:CONV_HIST, :])
            shifted = jnp.concatenate([head, rolled[CONV_HIST:, :]], axis=0)
            acc = acc + cwh[k:k + 1, :] * shifted
        o_ref[rows, :] = (acc + acc * jnp.tanh(acc)).astype(o_ref.dtype)
        hist = res[res.shape[0] - CONV_HIST:, :]
    carry_ref[j] = hist


def _in_proj_conv(a, w3, layer, row_off, n_cols, conv_w, conv_b, seq, tm, tn):
    m, k = a.shape
    nb = n_cols // tn
    off = row_off // tn
    assert n_cols % tn == 0 and m % tm == 0 and row_off % tn == 0 and seq % tm == 0
    assert SSD_CONV - 1 <= CONV_HIST
    body = functools.partial(_in_proj_conv_body, tiles_per_seq=seq // tm)
    return pl.pallas_call(
        body,
        grid=(m // tm, nb),
        in_specs=[
            pl.BlockSpec((tm, k), lambda i, j: (i, 0)),
            pl.BlockSpec((None, tn, k), lambda i, j: (layer, j + off, 0)),
            pl.BlockSpec((SSD_CONV, tn), lambda i, j: (0, j)),
            pl.BlockSpec((1, tn), lambda i, j: (0, j)),
        ],
        out_specs=pl.BlockSpec((tm, tn), lambda i, j: (i, j)),
        out_shape=jax.ShapeDtypeStruct((m, n_cols), BF16),
        scratch_shapes=[pltpu.VMEM((nb, CONV_HIST, tn), F32)],
        compiler_params=_params("arbitrary", "arbitrary"),
        name="in_proj_conv",
    )(a, w3, conv_w, conv_b)


def _in_proj_tail_body(a_ref, wt_ref, g_ref, dt_ref):
    tn = wt_ref.shape[0]
    for rows in _row_blocks(a_ref.shape[0]):
        res = _dot_nt(a_ref[rows, :], wt_ref[...])
        g_ref[rows, :] = res.astype(g_ref.dtype)
        dt_ref[rows, :] = res[:, tn - LANES:]


def _in_proj_tail(a, wt_tail, tm, tn):
    m, k = a.shape
    n = wt_tail.shape[0]
    assert n % tn == 0 and tn % LANES == 0
    return pl.pallas_call(
        _in_proj_tail_body,
        grid=(m // tm, n // tn),
        in_specs=[
            pl.BlockSpec((tm, k), lambda i, j: (i, 0)),
            pl.BlockSpec((tn, k), lambda i, j: (j, 0)),
        ],
        out_specs=[
            pl.BlockSpec((tm, tn), lambda i, j: (i, j)),
            pl.BlockSpec((tm, LANES), lambda i, j: (i, 0)),
        ],
        out_shape=[jax.ShapeDtypeStruct((m, n), BF16), jax.ShapeDtypeStruct((m, LANES), F32)],
        compiler_params=_params("parallel", "arbitrary"),
        name="in_proj_tail",
    )(a, wt_tail)


def _norm_matmul_body(x_ref, nw_ref, w_ref, o_ref, xn_ref):
    @pl.when(pl.program_id(1) == 0)
    def _():
        xn_ref[...] = _rms(x_ref[...], nw_ref[...]).astype(BF16)

    o_ref[...] = _dot(xn_ref[...], w_ref[...].astype(BF16)).astype(o_ref.dtype)


def _norm_matmul(x, nw, w, out_dtype, tm, tn, name):
    m, k = x.shape
    n = w.shape[1]
    return pl.pallas_call(
        _norm_matmul_body,
        grid=(m // tm, n // tn),
        in_specs=[
            pl.BlockSpec((tm, k), lambda i, j: (i, 0)),
            pl.BlockSpec((1, k), lambda i, j: (0, 0)),
            pl.BlockSpec((k, tn), lambda i, j: (0, j)),
        ],
        out_specs=pl.BlockSpec((tm, tn), lambda i, j: (i, j)),
        out_shape=jax.ShapeDtypeStruct((m, n), out_dtype),
        scratch_shapes=[pltpu.VMEM((tm, k), BF16)],
        compiler_params=_params("parallel", "arbitrary"),
        name=name,
    )(x, nw, w)


def _retention_tables():
    idx = np.arange(CHUNK, dtype=np.float64)
    rel = idx[:, None] - idx[None, :]
    lg = _LOG_G[:, None, None]
    scale = RET_DK ** -0.5
    decay = np.where(rel[None] >= 0, np.exp(np.maximum(rel, 0.0)[None] * lg), 0.0) * scale
    xi = np.exp((idx + 1.0)[None, :, None] * lg) * scale
    zeta = np.exp((CHUNK - 1.0 - idx)[None, :, None] * lg)
    wide = (RET_HEADS, CHUNK, RET_DK)
    return (decay.astype(np.float32), np.broadcast_to(xi, wide).astype(np.float32),
            np.broadcast_to(zeta, wide).astype(np.float32))


def _retention_body(q_ref, k_ref, v_ref, g_ref, cos_ref, sin_ref, dec_ref, xi_ref, zeta_ref,
                    nw_ref, o_ref, state_ref):
    @pl.when(pl.program_id(1) == 0)
    def _():
        state_ref[...] = jnp.zeros_like(state_ref)

    for cc, h in itertools.product(range(RET_STEP_CHUNKS), range(RET_HEADS)):
        rows = slice(cc * CHUNK, (cc + 1) * CHUNK)
        cos = cos_ref[rows, :]
        sin = sin_ref[rows, :]
        q = q_ref[rows, h * RET_DK:(h + 1) * RET_DK].astype(F32)
        k = k_ref[rows, h * RET_DK:(h + 1) * RET_DK].astype(F32)
        v = v_ref[rows, h * RET_DV:(h + 1) * RET_DV]
        qr = q * cos + pltpu.roll(q, RET_DK // 2, 1) * sin
        kr = k * cos + pltpu.roll(k, RET_DK // 2, 1) * sin
        s = _dot_nt(qr.astype(BF16), kr.astype(BF16)) * dec_ref[h]
        st = state_ref[h]
        lhs = jnp.concatenate([s.astype(BF16), (qr * xi_ref[h]).astype(BF16)], axis=1)
        rhs = jnp.concatenate([v, st.astype(BF16)], axis=0)
        y = _dot(lhs, rhs)
        kz_t = (kr * zeta_ref[h]).T.astype(BF16)
        state_ref[h] = float(_RET_CHUNK_DECAY[h]) * st + _dot(kz_t, v)
        mu = jnp.mean(y, axis=-1, keepdims=True)
        yc = y - mu
        var = jnp.mean(yc * yc, axis=-1, keepdims=True)
        yn = yc * lax.rsqrt(var + EPS)
        hsl = slice(h * RET_DV, (h + 1) * RET_DV)
        o_ref[rows, hsl] = (yn * nw_ref[:, hsl] * _silu(g_ref[rows, hsl].astype(F32))).astype(BF16)


def _retention(proj, col_off, cos2, sin2, nw, bsz, seq):
    rows = RET_STEP_CHUNKS * CHUNK
    nc = seq // rows
    qk = RET_HEADS * RET_DK
    vd = RET_HEADS * RET_DV
    assert vd == 2 * qk and col_off % vd == 0 and seq % rows == 0
    oq = col_off // qk
    ov = col_off // vd
    decay, xi, zeta = _retention_tables()
    table = lambda n: pl.BlockSpec((RET_HEADS, CHUNK, n), lambda b, c: (0, 0, 0))
    return pl.pallas_call(
        _retention_body,
        grid=(bsz, nc),
        in_specs=[
            pl.BlockSpec((rows, qk), lambda b, c: (b * nc + c, oq)),
            pl.BlockSpec((rows, qk), lambda b, c: (b * nc + c, oq + 1)),
            pl.BlockSpec((rows, vd), lambda b, c: (b * nc + c, ov + 1)),
            pl.BlockSpec((rows, vd), lambda b, c: (b * nc + c, ov + 2)),
            pl.BlockSpec((rows, RET_DK), lambda b, c: (c, 0)),
            pl.BlockSpec((rows, RET_DK), lambda b, c: (c, 0)),
            table(CHUNK), table(RET_DK), table(RET_DK),
            pl.BlockSpec((1, vd), lambda b, c: (0, 0)),
        ],
        out_specs=pl.BlockSpec((rows, vd), lambda b, c: (b * nc + c, 0)),
        out_shape=jax.ShapeDtypeStruct((bsz * seq, vd), BF16),
        scratch_shapes=[pltpu.VMEM((RET_HEADS, RET_DK, RET_DV), F32)],
        compiler_params=_params("parallel", "arbitrary"),
        name="retention",
    )(proj, proj, proj, proj, cos2, sin2, decay, xi, zeta, nw)


def _ssd_body(z_ref, xbc_ref, dt_ref, dtb_ref, alog_ref, dsk_ref, nw_ref,
              o_ref, prev_ref, *, inner, heads_per_group):
    c = CHUNK
    n_sub = z_ref.shape[0] // c
    gw = heads_per_group * SSD_HEAD_DIM

    @pl.when(pl.program_id(1) == 0)
    def _():
        prev_ref[...] = jnp.zeros_like(prev_ref)

    row = lax.broadcasted_iota(jnp.int32, (c, c), 0)
    col = lax.broadcasted_iota(jnp.int32, (c, c), 1)
    tri = row >= col
    lane_lo = lax.broadcasted_iota(jnp.int32, (c, LANES), 1) < SSD_HEAD_DIM
    neg_inf = jnp.float32(-jnp.inf)
    a = -jnp.exp(alog_ref[...])

    for cc in range(n_sub):
        rows = slice(cc * c, (cc + 1) * c)
        dtr = dt_ref[rows, :] + dtb_ref[...]
        dt = jnp.maximum(dtr, 0.0) + jnp.log1p(jnp.exp(-jnp.abs(dtr)))
        da = dt * a
        a_cs = jnp.dot(tri.astype(F32), da, precision=lax.Precision.HIGHEST,
                       preferred_element_type=F32)
        a_cs = a_cs * LOG2_E
        a_last = a_cs[c - 1:c, :]
        w1 = jnp.exp2(a_last - a_cs) * dt
        src_t = (a_cs - jnp.log2(dt)).T
        w1_t = w1.T

        for g in range(SSD_GROUPS):
            b_gb = xbc_ref[rows, inner + g * SSD_STATE: inner + (g + 1) * SSD_STATE]
            c_gb = xbc_ref[rows, inner + (SSD_GROUPS + g) * SSD_STATE:
                           inner + (SSD_GROUPS + g + 1) * SSD_STATE]
            cb = _dot_nt(c_gb, b_gb).astype(BF16)
            b_gt = b_gb.astype(F32).T.astype(BF16)
            y_pairs = []
            for pr in range(heads_per_group // 2):
                h0 = g * heads_per_group + 2 * pr
                psl = slice(h0 * SSD_HEAD_DIM, (h0 + 2) * SSD_HEAD_DIM)
                xs_b = xbc_ref[rows, psl]
                prev = prev_ref[:, psl]
                rhs = jnp.concatenate([xs_b, prev.astype(BF16)], axis=0)
                lhs, bws, cds = [], [], []
                for h in (h0, h0 + 1):
                    colb = jnp.broadcast_to(a_cs[:, h:h + 1], (c, c))
                    rowb = jnp.broadcast_to(src_t[h:h + 1, :], (c, c))
                    m = cb * jnp.exp2(jnp.where(tri, colb - rowb, neg_inf)).astype(BF16)
                    e = jnp.exp2(colb)
                    lhs.append(jnp.concatenate([m, e.astype(BF16) * c_gb], axis=1))
                    bws.append(b_gt * jnp.broadcast_to(w1_t[h:h + 1, :], (c, c)).astype(BF16))
                    cds.append(e[c - 1:c, :])
                ys = _dot(jnp.concatenate(lhs, axis=0), rhs)
                sts = _dot(jnp.concatenate(bws, axis=0), xs_b)
                y_pairs.append(jnp.where(lane_lo, ys[0:c, :], ys[c:2 * c, :]))
                cd = jnp.where(lane_lo[0:1, :], cds[0], cds[1])
                prev_ref[:, psl] = cd * prev + jnp.where(lane_lo, sts[0:c, :], sts[c:2 * c, :])
            gsl = slice(g * gw, (g + 1) * gw)
            y = jnp.concatenate(y_pairs, axis=1) + dsk_ref[:, gsl] * xbc_ref[rows, gsl].astype(F32)
            gy = y * _silu(z_ref[rows, gsl].astype(F32))
            ms = jnp.mean(gy * gy, axis=-1, keepdims=True)
            o_ref[rows, gsl] = (gy * lax.rsqrt(ms + EPS) * nw_ref[:, gsl]).astype(BF16)


def _ssd(p_z, z_off, p_xbc, inner, dt_raw, dtb, alog, dskip, nw, bsz, seq):
    rows = SSD_STEP_CHUNKS * CHUNK
    nc = seq // rows
    n_heads = inner // SSD_HEAD_DIM
    heads_per_group = n_heads // SSD_GROUPS
    conv_dim = p_xbc.shape[1]
    assert conv_dim == inner + 2 * SSD_GROUPS * SSD_STATE and SSD_STATE == CHUNK and seq % rows == 0
    assert n_heads <= LANES and heads_per_group % 2 == 0 and 2 * SSD_HEAD_DIM == LANES
    assert z_off % inner == 0
    zb = z_off // inner
    body = functools.partial(_ssd_body, inner=inner, heads_per_group=heads_per_group)
    row = lambda b, c: (b * nc + c, 0)
    fixed = lambda b, c: (0, 0)
    return pl.pallas_call(
        body,
        grid=(bsz, nc),
        in_specs=[
            pl.BlockSpec((rows, inner), lambda b, c: (b * nc + c, zb)),
            pl.BlockSpec((rows, conv_dim), row),
            pl.BlockSpec((rows, LANES), row),
            pl.BlockSpec((1, LANES), fixed),
            pl.BlockSpec((1, LANES), fixed),
            pl.BlockSpec((1, inner), fixed),
            pl.BlockSpec((1, inner), fixed),
        ],
        out_specs=pl.BlockSpec((rows, inner), row),
        out_shape=jax.ShapeDtypeStruct((bsz * seq, inner), BF16),
        scratch_shapes=[pltpu.VMEM((SSD_STATE, inner), F32)],
        compiler_params=_params("parallel", "arbitrary"),
        name="ssd",
    )(p_z, p_xbc, dt_raw, dtb, alog, dskip, nw)


def _merge_body(yr_ref, ys_ref, wr_ref, ws_ref, ga_ref, gb_ref, o_ref):
    for rows in _row_blocks(yr_ref.shape[0]):
        br = _dot(yr_ref[rows, :], wr_ref[...])
        bs = _dot(ys_ref[rows, :], ws_ref[...])
        ga = _sigmoid(ga_ref[rows, :].astype(F32))
        gb = _sigmoid(gb_ref[rows, :].astype(F32))
        o_ref[rows, :] = (ga * br + gb * bs).astype(o_ref.dtype)


def _merge(y_ret, y_ssd, w_ret, w_ssd, gates, tm, tn):
    m, kr = y_ret.shape
    ks = y_ssd.shape[1]
    n = w_ret.shape[1]
    nb = n // tn
    return pl.pallas_call(
        _merge_body,
        grid=(m // tm, nb),
        in_specs=[
            pl.BlockSpec((tm, kr), lambda i, j: (i, 0)),
            pl.BlockSpec((tm, ks), lambda i, j: (i, 0)),
            pl.BlockSpec((kr, tn), lambda i, j: (0, j)),
            pl.BlockSpec((ks, tn), lambda i, j: (0, j)),
            pl.BlockSpec((tm, tn), lambda i, j: (i, j)),
            pl.BlockSpec((tm, tn), lambda i, j: (i, j + nb)),
        ],
        out_specs=pl.BlockSpec((tm, tn), lambda i, j: (i, j)),
        out_shape=jax.ShapeDtypeStruct((m, n), BF16),
        compiler_params=_params("parallel", "arbitrary"),
        name="merge",
    )(y_ret, y_ssd, w_ret, w_ssd, gates, gates)


def _mix_xattn_body(m_ref, x_ref, k_ref, v_ref, wmix_ref, nw_ref, wq_ref, wo_ref, o_ref, xn_ref):
    d = x_ref.shape[1]
    hd = d // XA_HEADS
    scale = hd ** -0.5
    o_ref[...] = x_ref[...] + _dot(m_ref[...], wmix_ref[...])
    xn_ref[...] = _rms(o_ref[...], nw_ref[...]).astype(BF16)
    for h in range(XA_HEADS):
        sl = slice(h * hd, (h + 1) * hd)
        q = _dot(xn_ref[...], wq_ref[:, sl]).astype(BF16)
        s = _dot_nt(q, k_ref[:, sl]) * scale
        s = s - jnp.max(s, axis=-1, keepdims=True)
        p = jnp.exp(s)
        p = p / jnp.sum(p, axis=-1, keepdims=True)
        o_h = _dot(p.astype(BF16), v_ref[:, sl]).astype(BF16)
        o_ref[...] += _dot(o_h, wo_ref[sl, :])


def _mix_xattn(merged, x, kv, w_mix, nw, w_q, w_o, bsz, seq, mem_len, tq):
    d = x.shape[1]
    nq = seq // tq
    tile = lambda b, i: (b * nq + i, 0)
    fixed = lambda b, i: (0, 0)
    resident = pl.Buffered(1)
    return pl.pallas_call(
        _mix_xattn_body,
        grid=(bsz, nq),
        in_specs=[
            pl.BlockSpec((tq, d), tile),
            pl.BlockSpec((tq, d), tile),
            pl.BlockSpec((mem_len, d), lambda b, i: (b, 0)),
            pl.BlockSpec((mem_len, d), lambda b, i: (b, 1)),
            pl.BlockSpec((d, d), fixed, pipeline_mode=resident),
            pl.BlockSpec((1, d), fixed),
            pl.BlockSpec((d, d), fixed, pipeline_mode=resident),
            pl.BlockSpec((d, d), fixed, pipeline_mode=resident),
        ],
        out_specs=pl.BlockSpec((tq, d), tile),
        out_shape=jax.ShapeDtypeStruct((bsz * seq, d), F32),
        scratch_shapes=[pltpu.VMEM((tq, d), BF16)],
        compiler_params=_params("parallel", "arbitrary"),
        name="mix_xattn",
    )(merged, x, kv, kv, w_mix, nw, w_q, w_o)


def _mlp_body(h_ref, nw_ref, w1_ref, w2_ref, fw_ref, o_ref, xn_ref):
    f = pl.program_id(1)
    blocks = _row_blocks(h_ref.shape[0])

    @pl.when(f == 0)
    def _():
        for rows in blocks:
            hh = h_ref[rows, :]
            xn_ref[rows, :] = _rms(hh, nw_ref[...]).astype(BF16)
            o_ref[rows, :] = hh

    w1b = w1_ref[...].astype(BF16)
    w2b = w2_ref[...].astype(BF16)
    for rows in blocks:
        u = _dot(xn_ref[rows, :], w1b)
        u = jnp.square(jnp.maximum(u, 0.0)).astype(BF16)
        o_ref[rows, :] += _dot(u, w2b)

    @pl.when(f == pl.num_programs(1) - 1)
    def _():
        for rows in blocks:
            o_ref[rows, :] = _rms(o_ref[rows, :], fw_ref[...])


def _mlp(h, nw, w1, w2, fw, tm, tf):
    m, d = h.shape
    dff = w1.shape[1]
    return pl.pallas_call(
        _mlp_body,
        grid=(m // tm, dff // tf),
        in_specs=[
            pl.BlockSpec((tm, d), lambda i, f: (i, 0)),
            pl.BlockSpec((1, d), lambda i, f: (0, 0)),
            pl.BlockSpec((d, tf), lambda i, f: (0, f)),
            pl.BlockSpec((tf, d), lambda i, f: (f, 0)),
            pl.BlockSpec((1, d), lambda i, f: (0, 0)),
        ],
        out_specs=pl.BlockSpec((tm, d), lambda i, f: (i, 0)),
        out_shape=jax.ShapeDtypeStruct((m, d), F32),
        scratch_shapes=[pltpu.VMEM((tm, d), BF16)],
        compiler_params=_params("parallel", "arbitrary"),
        name="mlp",
    )(h, nw, w1, w2, fw)


def _rope_tables(seq, dk):
    half = dk // 2
    inv = np.exp(-math.log(ROPE_BASE) * np.arange(half, dtype=np.float64) / half)
    ang = np.arange(seq, dtype=np.float64)[:, None] * inv[None, :]
    cos, sin = np.cos(ang).astype(np.float32), np.sin(ang).astype(np.float32)
    return np.concatenate([cos, cos], axis=1), np.concatenate([-sin, sin], axis=1)


def kernel(x, mem, norm_mix_w, w_in, conv_w, conv_b, dt_bias, a_log, d_skip, ret_norm_w, ssd_norm_w, w_ret_out, w_ssd_out, w_mix_out, norm_xa_w, mem_norm_w, w_xq, w_xkv, w_xo, norm_ff_w, w_ff1, w_ff2, final_norm_w):
    bsz, seq, d = x.shape
    mem_len = mem.shape[1]
    depth = w_in.shape[0]
    n_tok = bsz * seq
    ret_qk = RET_HEADS * RET_DK
    ret_v = RET_HEADS * RET_DV
    inner = w_ssd_out.shape[1]
    conv_dim = conv_w.shape[2]
    n_heads = inner // SSD_HEAD_DIM
    ret_w = 2 * ret_qk + 2 * ret_v
    off_dt = ret_w + inner + conv_dim
    off_gate = off_dt + n_heads
    assert w_in.shape[2] == off_gate + 2 * d and depth == 1

    tm_main, tn_main = IN_PROJ_TILE
    assert conv_dim % tn_main == 0 and ret_w % tn_main == 0 and inner % tn_main == 0
    shift = inner // tn_main
    z_off, ret_off = 0, inner

    cos2, sin2 = _rope_tables(seq, RET_DK)
    row = lambda v: v.reshape(1, -1)
    pad_heads = lambda v: jnp.pad(v.reshape(1, -1), ((0, 0), (0, LANES - n_heads)))

    w_in_t = jnp.swapaxes(w_in, 1, 2)

    h = x.reshape(n_tok, d)
    for l in range(depth):
        xn = _prenorm(h, row(norm_mix_w[l]), PRENORM_ROWS)
        assert LANES - n_heads == n_heads
        proj, wt_tail, (w_ret_b, w_ssd_b, w_mix_b, w_xq_b, w_xo_b) = _in_proj_main(
            xn, w_in_t, l, 0, ret_w + inner, shift,
            (w_ret_out[l], w_ssd_out[l], w_mix_out[l], w_xq[l], w_xo[l]),
            off_gate, 2 * d, off_dt, n_heads, tm_main, tn_main)
        p_xbc = _in_proj_conv(xn, w_in_t, l, ret_w + inner, conv_dim, conv_w[l], row(conv_b[l]),
                              seq, tm_main, tn_main)
        gates, dt_raw = _in_proj_tail(xn, wt_tail, TAIL_ROWS, wt_tail.shape[0] // TAIL_COL_BLOCKS)
        y_ret = _retention(proj, ret_off, cos2, sin2, row(ret_norm_w[l]), bsz, seq)
        y_ssd = _ssd(proj, z_off, p_xbc, inner, dt_raw, pad_heads(dt_bias[l]), pad_heads(a_log[l]),
                     row(jnp.repeat(d_skip[l], SSD_HEAD_DIM)), row(ssd_norm_w[l]), bsz, seq)
        merged = _merge(y_ret, y_ssd, w_ret_b, w_ssd_b, gates, *MERGE_TILE)
        kv = _norm_matmul(mem.reshape(bsz * mem_len, d), row(mem_norm_w[l]), w_xkv[l],
                          BF16, bsz * mem_len, KV_COLS, "xa_kv")
        h = _mix_xattn(merged, h, kv, w_mix_b, row(norm_xa_w[l]), w_xq_b, w_xo_b,
                       bsz, seq, mem_len, XATTN_ROWS)
        h = _mlp(h, row(norm_ff_w[l]), w_ff1[l], w_ff2[l], row(final_norm_w), *MLP_TILE)
    return h.reshape(bsz, seq, d)
```

```python
import functools
import itertools
import math

import numpy as np
import jax
import jax.numpy as jnp
from jax import lax
from jax.experimental import pallas as pl
from jax.experimental.pallas import tpu as pltpu

F32 = jnp.float32
BF16 = jnp.bfloat16
EPS = 1e-6

RET_HEADS = 8
RET_DK = 128
RET_DV = 256
CHUNK = 128
ROPE_BASE = 10000.0
SSD_HEAD_DIM = 64
SSD_GROUPS = 8
SSD_STATE = 128
SSD_CONV = 4
CONV_HIST = 8
XA_HEADS = 4

V7X_VMEM_BYTES = 64 * 1024 * 1024
VMEM_LIMIT = V7X_VMEM_BYTES - 4 * 1024 * 1024
LANES = 128
ROW_SPLIT = 512
CONV_ROW_SPLIT = 256
LOG2_E = math.log2(math.e)

PRENORM_ROWS = 512
IN_PROJ_TILE = (2048, 1024)
CAST_SLABS = 32
TAIL_ROWS = 2048
TAIL_COL_BLOCKS = 3
MERGE_TILE = (1024, 512)
KV_COLS = 1024
XATTN_ROWS = 512
MLP_TILE = (1024, 512)
RET_STEP_CHUNKS = 8
SSD_STEP_CHUNKS = 2

_LOG_G = np.log(1.0 - np.exp2(-5.0 - np.arange(RET_HEADS, dtype=np.float64)))
_RET_CHUNK_DECAY = np.exp(CHUNK * _LOG_G)


def _params(*sem):
    return pltpu.CompilerParams(dimension_semantics=sem, vmem_limit_bytes=VMEM_LIMIT)


def _rms(x, w):
    ms = jnp.mean(x * x, axis=-1, keepdims=True)
    return x * lax.rsqrt(ms + EPS) * w


def _sigmoid(x):
    return 0.5 + 0.5 * jnp.tanh(0.5 * x)


def _silu(x):
    h = 0.5 * x
    return h + h * jnp.tanh(h)


def _dot(a, b):
    return jnp.dot(a, b, preferred_element_type=F32)


def _dot_nt(a, b):
    return lax.dot_general(a, b, (((1,), (1,)), ((), ())), preferred_element_type=F32)


def _row_blocks(n_rows, step=ROW_SPLIT):
    step = min(step, n_rows)
    return [slice(r, r + step) for r in range(0, n_rows, step)]


def _prenorm_body(x_ref, nw_ref, o_ref):
    o_ref[...] = _rms(x_ref[...], nw_ref[...]).astype(o_ref.dtype)


def _prenorm(x, nw, tm):
    m, k = x.shape
    return pl.pallas_call(
        _prenorm_body,
        grid=(m // tm,),
        in_specs=[pl.BlockSpec((tm, k), lambda i: (i, 0)), pl.BlockSpec((1, k), lambda i: (0, 0))],
        out_specs=pl.BlockSpec((tm, k), lambda i: (i, 0)),
        out_shape=jax.ShapeDtypeStruct((m, k), BF16),
        compiler_params=_params("parallel"),
        name="prenorm",
    )(x, nw)


def _matmul_nt_body(a_ref, wt_ref, t0_ref, t1_ref, *refs, nb, tail_blocks):
    n_cast = (len(refs) - 2) // 2
    o_ref, tail_ref = refs[n_cast], refs[n_cast + 1]
    wb = wt_ref[...].astype(BF16)
    for rows in _row_blocks(a_ref.shape[0]):
        o_ref[rows, :] = _dot_nt(a_ref[rows, :], wb).astype(o_ref.dtype)
    for src_ref, dst_ref in zip(refs[:n_cast], refs[n_cast + 2:]):
        dst_ref[...] = src_ref[...].astype(dst_ref.dtype)
    slab = t0_ref.shape[0]
    step = pl.program_id(0) * nb + pl.program_id(1)
    tail_ref[0:slab, :] = t0_ref[...].astype(tail_ref.dtype)
    tail_ref[slab:2 * slab, :] = jnp.where(step >= tail_blocks - 1, 0.0, t1_ref[...]).astype(tail_ref.dtype)


def _in_proj_main(a, w3, layer, row_off, n_cols, shift, to_bf16, gate_row0, n_gate_rows,
                  dt_row0, slab, tm, tn):
    m, k = a.shape
    nb = n_cols // tn
    off = row_off // tn
    n_steps = (m // tm) * nb
    assert n_cols % tn == 0 and m % tm == 0 and row_off % tn == 0
    assert CAST_SLABS <= n_steps and all(w.shape[0] % CAST_SLABS == 0 for w in to_bf16)
    cast_slab = lambda i, j: (jnp.minimum(i * nb + j, CAST_SLABS - 1), 0)
    cast_specs = [pl.BlockSpec((w.shape[0] // CAST_SLABS, w.shape[1]), cast_slab) for w in to_bf16]
    n_gate = n_gate_rows // slab
    tail_blocks = (n_gate + 2) // 2
    assert gate_row0 % slab == 0 and dt_row0 % slab == 0 and n_gate_rows % slab == 0
    assert n_gate % 2 == 0 and tail_blocks <= n_steps
    tail_blk = lambda i, j: jnp.minimum(i * nb + j, tail_blocks - 1)

    def tail_src(t):
        def index(i, j):
            q = 2 * tail_blk(i, j) + t
            return (layer, jnp.where(q < n_gate, gate_row0 // slab + q, dt_row0 // slab), 0)
        return pl.BlockSpec((None, slab, k), index)

    body = functools.partial(_matmul_nt_body, nb=nb, tail_blocks=tail_blocks)
    outs = pl.pallas_call(
        body,
        grid=(m // tm, nb),
        in_specs=[
            pl.BlockSpec((tm, k), lambda i, j: (i, 0)),
            pl.BlockSpec((None, tn, k), lambda i, j: (layer, j + off, 0)),
            tail_src(0), tail_src(1),
        ] + cast_specs,
        out_specs=[pl.BlockSpec((tm, tn), lambda i, j: (i, (j + shift) % nb)),
                   pl.BlockSpec((2 * slab, k), lambda i, j: (tail_blk(i, j), 0))] + cast_specs,
        out_shape=[jax.ShapeDtypeStruct((m, n_cols), BF16),
                   jax.ShapeDtypeStruct((2 * slab * tail_blocks, k), BF16)]
        + [jax.ShapeDtypeStruct(w.shape, BF16) for w in to_bf16],
        compiler_params=_params("arbitrary", "arbitrary"),
        name="in_proj_main",
    )(a, w3, w3, w3, *to_bf16)
    return outs[0], outs[1], outs[2:]


def _in_proj_conv_body(a_ref, wt_ref, cw_ref, cb_ref, o_ref, carry_ref, *, tiles_per_seq):
    i = pl.program_id(0)
    j = pl.program_id(1)
    tn = wt_ref.shape[0]

    @pl.when(i % tiles_per_seq == 0)
    def _():
        carry_ref[j] = jnp.zeros((CONV_HIST, tn), F32)

    wb = wt_ref[...].astype(BF16)
    cwh = 0.5 * cw_ref[...]
    cbh = 0.5 * cb_ref[...]
    hist = carry_ref[j]
    for rows in _row_blocks(a_ref.shape[0], CONV_ROW_SPLIT):
        res = _dot_nt(a_ref[rows, :], wb)
        nt = res.shape[0] // CONV_HIST
        tiles = jnp.concatenate([hist, res], axis=0).reshape(nt + 1, CONV_HIST, tn)
        sub = lax.broadcasted_iota(jnp.int32, (nt, CONV_HIST, tn), 1)
        acc = cbh + cwh[SSD_CONV - 1:SSD_CONV, :] * res
        for k in range(SSD_CONV - 1):
            dist = SSD_CONV - 1 - k
            mixed = jnp.where(sub >= CONV_HIST - dist, tiles[:-1], tiles[1:])
            shifted = pltpu.roll(mixed, dist, 1).reshape(nt * CONV_HIST, tn)
            acc = acc + cwh[k:k + 1, :] * shifted
        o_ref[rows, :] = (acc + acc * jnp.tanh(acc)).astype(o_ref.dtype)
        hist = res[res.shape[0] - CONV_HIST:, :]
    carry_ref[j] = hist


def _in_proj_conv(a, w3, layer, row_off, n_cols, conv_w, conv_b, seq, tm, tn):
    m, k = a.shape
    nb = n_cols // tn
    off = row_off // tn
    assert n_cols % tn == 0 and m % tm == 0 and row_off % tn == 0 and seq % tm == 0
    assert SSD_CONV - 1 <= CONV_HIST
    body = functools.partial(_in_proj_conv_body, tiles_per_seq=seq // tm)
    return pl.pallas_call(
        body,
        grid=(m // tm, nb),
        in_specs=[
            pl.BlockSpec((tm, k), lambda i, j: (i, 0)),
            pl.BlockSpec((None, tn, k), lambda i, j: (layer, j + off, 0)),
            pl.BlockSpec((SSD_CONV, tn), lambda i, j: (0, j)),
            pl.BlockSpec((1, tn), lambda i, j: (0, j)),
        ],
        out_specs=pl.BlockSpec((tm, tn), lambda i, j: (i, j)),
        out_shape=jax.ShapeDtypeStruct((m, n_cols), BF16),
        scratch_shapes=[pltpu.VMEM((nb, CONV_HIST, tn), F32)],
        compiler_params=_params("arbitrary", "arbitrary"),
        name="in_proj_conv",
    )(a, w3, conv_w, conv_b)


def _in_proj_tail_body(a_ref, wt_ref, g_ref, dt_ref):
    tn = wt_ref.shape[0]
    for rows in _row_blocks(a_ref.shape[0]):
        res = _dot_nt(a_ref[rows, :], wt_ref[...])
        g_ref[rows, :] = res.astype(g_ref.dtype)
        dt_ref[rows, :] = res[:, tn - LANES:]


def _in_proj_tail(a, wt_tail, tm, tn):
    m, k = a.shape
    n = wt_tail.shape[0]
    assert n % tn == 0 and tn % LANES == 0
    return pl.pallas_call(
        _in_proj_tail_body,
        grid=(m // tm, n // tn),
        in_specs=[
            pl.BlockSpec((tm, k), lambda i, j: (i, 0)),
            pl.BlockSpec((tn, k), lambda i, j: (j, 0)),
        ],
        out_specs=[
            pl.BlockSpec((tm, tn), lambda i, j: (i, j)),
            pl.BlockSpec((tm, LANES), lambda i, j: (i, 0)),
        ],
        out_shape=[jax.ShapeDtypeStruct((m, n), BF16), jax.ShapeDtypeStruct((m, LANES), F32)],
        compiler_params=_params("parallel", "arbitrary"),
        name="in_proj_tail",
    )(a, wt_tail)


def _norm_matmul_body(x_ref, nw_ref, w_ref, o_ref, xn_ref):
    @pl.when(pl.program_id(1) == 0)
    def _():
        xn_ref[...] = _rms(x_ref[...], nw_ref[...]).astype(BF16)

    o_ref[...] = _dot(xn_ref[...], w_ref[...].astype(BF16)).astype(o_ref.dtype)


def _norm_matmul(x, nw, w, out_dtype, tm, tn, name):
    m, k = x.shape
    n = w.shape[1]
    return pl.pallas_call(
        _norm_matmul_body,
        grid=(m // tm, n // tn),
        in_specs=[
            pl.BlockSpec((tm, k), lambda i, j: (i, 0)),
            pl.BlockSpec((1, k), lambda i, j: (0, 0)),
            pl.BlockSpec((k, tn), lambda i, j: (0, j)),
        ],
        out_specs=pl.BlockSpec((tm, tn), lambda i, j: (i, j)),
        out_shape=jax.ShapeDtypeStruct((m, n), out_dtype),
        scratch_shapes=[pltpu.VMEM((tm, k), BF16)],
        compiler_params=_params("parallel", "arbitrary"),
        name=name,
    )(x, nw, w)


def _retention_tables():
    idx = np.arange(CHUNK, dtype=np.float64)
    rel = idx[:, None] - idx[None, :]
    lg = _LOG_G[:, None, None]
    scale = RET_DK ** -0.5
    decay = np.where(rel[None] >= 0, np.exp(np.maximum(rel, 0.0)[None] * lg), 0.0) * scale
    xi = np.exp((idx + 1.0)[None, :, None] * lg) * scale
    zeta = np.exp((CHUNK - 1.0 - idx)[None, :, None] * lg)
    wide = (RET_HEADS, CHUNK, RET_DK)
    return (decay.astype(np.float32), np.broadcast_to(xi, wide).astype(np.float32),
            np.broadcast_to(zeta, wide).astype(np.float32))


def _retention_body(q_ref, k_ref, v_ref, g_ref, cos_ref, sin_ref, dec_ref, xi_ref, zeta_ref,
                    nw_ref, o_ref, state_ref):
    @pl.when(pl.program_id(1) == 0)
    def _():
        state_ref[...] = jnp.zeros_like(state_ref)

    for cc, h in itertools.product(range(RET_STEP_CHUNKS), range(RET_HEADS)):
        rows = slice(cc * CHUNK, (cc + 1) * CHUNK)
        cos = cos_ref[rows, :]
        sin = sin_ref[rows, :]
        q = q_ref[rows, h * RET_DK:(h + 1) * RET_DK].astype(F32)
        k = k_ref[rows, h * RET_DK:(h + 1) * RET_DK].astype(F32)
        v = v_ref[rows, h * RET_DV:(h + 1) * RET_DV]
        qr = q * cos + pltpu.roll(q, RET_DK // 2, 1) * sin
        kr = k * cos + pltpu.roll(k, RET_DK // 2, 1) * sin
        s = _dot_nt(qr.astype(BF16), kr.astype(BF16)) * dec_ref[h]
        st = state_ref[h]
        lhs = jnp.concatenate([s.astype(BF16), (qr * xi_ref[h]).astype(BF16)], axis=1)
        rhs = jnp.concatenate([v, st.astype(BF16)], axis=0)
        y = _dot(lhs, rhs)
        kz_t = (kr * zeta_ref[h]).T.astype(BF16)
        state_ref[h] = float(_RET_CHUNK_DECAY[h]) * st + _dot(kz_t, v)
        mu = jnp.mean(y, axis=-1, keepdims=True)
        yc = y - mu
        var = jnp.mean(yc * yc, axis=-1, keepdims=True)
        yn = yc * lax.rsqrt(var + EPS)
        hsl = slice(h * RET_DV, (h + 1) * RET_DV)
        o_ref[rows, hsl] = (yn * nw_ref[:, hsl] * _silu(g_ref[rows, hsl].astype(F32))).astype(BF16)


def _retention(proj, col_off, cos2, sin2, nw, bsz, seq):
    rows = RET_STEP_CHUNKS * CHUNK
    nc = seq // rows
    qk = RET_HEADS * RET_DK
    vd = RET_HEADS * RET_DV
    assert vd == 2 * qk and col_off % vd == 0 and seq % rows == 0
    oq = col_off // qk
    ov = col_off // vd
    decay, xi, zeta = _retention_tables()
    table = lambda n: pl.BlockSpec((RET_HEADS, CHUNK, n), lambda b, c: (0, 0, 0))
    return pl.pallas_call(
        _retention_body,
        grid=(bsz, nc),
        in_specs=[
            pl.BlockSpec((rows, qk), lambda b, c: (b * nc + c, oq)),
            pl.BlockSpec((rows, qk), lambda b, c: (b * nc + c, oq + 1)),
            pl.BlockSpec((rows, vd), lambda b, c: (b * nc + c, ov + 1)),
            pl.BlockSpec((rows, vd), lambda b, c: (b * nc + c, ov + 2)),
            pl.BlockSpec((rows, RET_DK), lambda b, c: (c, 0)),
            pl.BlockSpec((rows, RET_DK), lambda b, c: (c, 0)),
            table(CHUNK), table(RET_DK), table(RET_DK),
            pl.BlockSpec((1, vd), lambda b, c: (0, 0)),
        ],
        out_specs=pl.BlockSpec((rows, vd), lambda b, c: (b * nc + c, 0)),
        out_shape=jax.ShapeDtypeStruct((bsz * seq, vd), BF16),
        scratch_shapes=[pltpu.VMEM((RET_HEADS, RET_DK, RET_DV), F32)],
        compiler_params=_params("parallel", "arbitrary"),
        name="retention",
    )(proj, proj, proj, proj, cos2, sin2, decay, xi, zeta, nw)


def _ssd_body(z_ref, xbc_ref, dt_ref, dtb_ref, alog_ref, dsk_ref, nw_ref,
              o_ref, prev_ref, *, inner, heads_per_group):
    c = CHUNK
    n_sub = z_ref.shape[0] // c
    gw = heads_per_group * SSD_HEAD_DIM

    @pl.when(pl.program_id(1) == 0)
    def _():
        prev_ref[...] = jnp.zeros_like(prev_ref)

    row = lax.broadcasted_iota(jnp.int32, (c, c), 0)
    col = lax.broadcasted_iota(jnp.int32, (c, c), 1)
    tri = row >= col
    lane_lo = lax.broadcasted_iota(jnp.int32, (c, LANES), 1) < SSD_HEAD_DIM
    neg_inf = jnp.float32(-jnp.inf)
    a = -jnp.exp(alog_ref[...])

    for cc in range(n_sub):
        rows = slice(cc * c, (cc + 1) * c)
        dtr = dt_ref[rows, :] + dtb_ref[...]
        dt = jnp.maximum(dtr, 0.0) + jnp.log1p(jnp.exp(-jnp.abs(dtr)))
        da = dt * a
        a_cs = jnp.dot(tri.astype(F32), da, precision=lax.Precision.HIGHEST,
                       preferred_element_type=F32)
        a_cs = a_cs * LOG2_E
        a_last = a_cs[c - 1:c, :]
        w1 = jnp.exp2(a_last - a_cs) * dt
        src_t = (a_cs - jnp.log2(dt)).T
        w1_t = w1.T

        for g in range(SSD_GROUPS):
            b_gb = xbc_ref[rows, inner + g * SSD_STATE: inner + (g + 1) * SSD_STATE]
            c_gb = xbc_ref[rows, inner + (SSD_GROUPS + g) * SSD_STATE:
                           inner + (SSD_GROUPS + g + 1) * SSD_STATE]
            cb = _dot_nt(c_gb, b_gb).astype(BF16)
            b_gt = b_gb.astype(F32).T.astype(BF16)
            y_pairs = []
            for pr in range(heads_per_group // 2):
                h0 = g * heads_per_group + 2 * pr
                psl = slice(h0 * SSD_HEAD_DIM, (h0 + 2) * SSD_HEAD_DIM)
                xs_b = xbc_ref[rows, psl]
                prev = prev_ref[:, psl]
                rhs = jnp.concatenate([xs_b, prev.astype(BF16)], axis=0)
                lhs, bws, cds = [], [], []
                for h in (h0, h0 + 1):
                    colb = jnp.broadcast_to(a_cs[:, h:h + 1], (c, c))
                    rowb = jnp.broadcast_to(src_t[h:h + 1, :], (c, c))
                    m = cb * jnp.exp2(jnp.where(tri, colb - rowb, neg_inf)).astype(BF16)
                    e = jnp.exp2(colb)
                    lhs.append(jnp.concatenate([m, e.astype(BF16) * c_gb], axis=1))
                    bws.append(b_gt * jnp.broadcast_to(w1_t[h:h + 1, :], (c, c)).astype(BF16))
                    cds.append(e[c - 1:c, :])
                ys = _dot(jnp.concatenate(lhs, axis=0), rhs)
                sts = _dot(jnp.concatenate(bws, axis=0), xs_b)
                y_pairs.append(jnp.where(lane_lo, ys[0:c, :], ys[c:2 * c, :]))
                cd = jnp.where(lane_lo[0:1, :], cds[0], cds[1])
                prev_ref[:, psl] = cd * prev + jnp.where(lane_lo, sts[0:c, :], sts[c:2 * c, :])
            gsl = slice(g * gw, (g + 1) * gw)
            y = jnp.concatenate(y_pairs, axis=1) + dsk_ref[:, gsl] * xbc_ref[rows, gsl].astype(F32)
            gy = y * _silu(z_ref[rows, gsl].astype(F32))
            ms = jnp.mean(gy * gy, axis=-1, keepdims=True)
            o_ref[rows, gsl] = (gy * lax.rsqrt(ms + EPS) * nw_ref[:, gsl]).astype(BF16)


def _ssd(p_z, z_off, p_xbc, inner, dt_raw, dtb, alog, dskip, nw, bsz, seq):
    rows = SSD_STEP_CHUNKS * CHUNK
    nc = seq // rows
    n_heads = inner // SSD_HEAD_DIM
    heads_per_group = n_heads // SSD_GROUPS
    conv_dim = p_xbc.shape[1]
    assert conv_dim == inner + 2 * SSD_GROUPS * SSD_STATE and SSD_STATE == CHUNK and seq % rows == 0
    assert n_heads <= LANES and heads_per_group % 2 == 0 and 2 * SSD_HEAD_DIM == LANES
    assert z_off % inner == 0
    zb = z_off // inner
    body = functools.partial(_ssd_body, inner=inner, heads_per_group=heads_per_group)
    row = lambda b, c: (b * nc + c, 0)
    fixed = lambda b, c: (0, 0)
    return pl.pallas_call(
        body,
        grid=(bsz, nc),
        in_specs=[
            pl.BlockSpec((rows, inner), lambda b, c: (b * nc + c, zb)),
            pl.BlockSpec((rows, conv_dim), row),
            pl.BlockSpec((rows, LANES), row),
            pl.BlockSpec((1, LANES), fixed),
            pl.BlockSpec((1, LANES), fixed),
            pl.BlockSpec((1, inner), fixed),
            pl.BlockSpec((1, inner), fixed),
        ],
        out_specs=pl.BlockSpec((rows, inner), row),
        out_shape=jax.ShapeDtypeStruct((bsz * seq, inner), BF16),
        scratch_shapes=[pltpu.VMEM((SSD_STATE, inner), F32)],
        compiler_params=_params("parallel", "arbitrary"),
        name="ssd",
    )(p_z, p_xbc, dt_raw, dtb, alog, dskip, nw)


def _merge_body(yr_ref, ys_ref, wr_ref, ws_ref, ga_ref, gb_ref, o_ref):
    for rows in _row_blocks(yr_ref.shape[0]):
        br = _dot(yr_ref[rows, :], wr_ref[...])
        bs = _dot(ys_ref[rows, :], ws_ref[...])
        ga = _sigmoid(ga_ref[rows, :].astype(F32))
        gb = _sigmoid(gb_ref[rows, :].astype(F32))
        o_ref[rows, :] = (ga * br + gb * bs).astype(o_ref.dtype)


def _merge(y_ret, y_ssd, w_ret, w_ssd, gates, tm, tn):
    m, kr = y_ret.shape
    ks = y_ssd.shape[1]
    n = w_ret.shape[1]
    nb = n // tn
    return pl.pallas_call(
        _merge_body,
        grid=(m // tm, nb),
        in_specs=[
            pl.BlockSpec((tm, kr), lambda i, j: (i, 0)),
            pl.BlockSpec((tm, ks), lambda i, j: (i, 0)),
            pl.BlockSpec((kr, tn), lambda i, j: (0, j)),
            pl.BlockSpec((ks, tn), lambda i, j: (0, j)),
            pl.BlockSpec((tm, tn), lambda i, j: (i, j)),
            pl.BlockSpec((tm, tn), lambda i, j: (i, j + nb)),
        ],
        out_specs=pl.BlockSpec((tm, tn), lambda i, j: (i, j)),
        out_shape=jax.ShapeDtypeStruct((m, n), BF16),
        compiler_params=_params("parallel", "arbitrary"),
        name="merge",
    )(y_ret, y_ssd, w_ret, w_ssd, gates, gates)


def _mix_xattn_body(m_ref, x_ref, k_ref, v_ref, wmix_ref, nw_ref, wq_ref, wo_ref, o_ref, xn_ref):
    d = x_ref.shape[1]
    hd = d // XA_HEADS
    scale = hd ** -0.5
    o_ref[...] = x_ref[...] + _dot(m_ref[...], wmix_ref[...])
    xn_ref[...] = _rms(o_ref[...], nw_ref[...]).astype(BF16)
    for h in range(XA_HEADS):
        sl = slice(h * hd, (h + 1) * hd)
        q = _dot(xn_ref[...], wq_ref[:, sl]).astype(BF16)
        s = _dot_nt(q, k_ref[:, sl]) * scale
        s = s - jnp.max(s, axis=-1, keepdims=True)
        p = jnp.exp(s)
        p = p / jnp.sum(p, axis=-1, keepdims=True)
        o_h = _dot(p.astype(BF16), v_ref[:, sl]).astype(BF16)
        o_ref[...] += _dot(o_h, wo_ref[sl, :])


def _mix_xattn(merged, x, kv, w_mix, nw, w_q, w_o, bsz, seq, mem_len, tq):
    d = x.shape[1]
    nq = seq // tq
    tile = lambda b, i: (b * nq + i, 0)
    fixed = lambda b, i: (0, 0)
    resident = pl.Buffered(1)
    return pl.pallas_call(
        _mix_xattn_body,
        grid=(bsz, nq),
        in_specs=[
            pl.BlockSpec((tq, d), tile),
            pl.BlockSpec((tq, d), tile),
            pl.BlockSpec((mem_len, d), lambda b, i: (b, 0)),
            pl.BlockSpec((mem_len, d), lambda b, i: (b, 1)),
            pl.BlockSpec((d, d), fixed, pipeline_mode=resident),
            pl.BlockSpec((1, d), fixed),
            pl.BlockSpec((d, d), fixed, pipeline_mode=resident),
            pl.BlockSpec((d, d), fixed, pipeline_mode=resident),
        ],
        out_specs=pl.BlockSpec((tq, d), tile),
        out_shape=jax.ShapeDtypeStruct((bsz * seq, d), F32),
        scratch_shapes=[pltpu.VMEM((tq, d), BF16)],
        compiler_params=_params("parallel", "arbitrary"),
        name="mix_xattn",
    )(merged, x, kv, kv, w_mix, nw, w_q, w_o)


def _mlp_body(h_ref, nw_ref, w1_ref, w2_ref, fw_ref, o_ref, xn_ref):
    f = pl.program_id(1)
    blocks = _row_blocks(h_ref.shape[0])

    @pl.when(f == 0)
    def _():
        for rows in blocks:
            hh = h_ref[rows, :]
            xn_ref[rows, :] = _rms(hh, nw_ref[...]).astype(BF16)
            o_ref[rows, :] = hh

    w1b = w1_ref[...].astype(BF16)
    w2b = w2_ref[...].astype(BF16)
    for rows in blocks:
        u = _dot(xn_ref[rows, :], w1b)
        u = jnp.square(jnp.maximum(u, 0.0)).astype(BF16)
        o_ref[rows, :] += _dot(u, w2b)

    @pl.when(f == pl.num_programs(1) - 1)
    def _():
        for rows in blocks:
            o_ref[rows, :] = _rms(o_ref[rows, :], fw_ref[...])


def _mlp(h, nw, w1, w2, fw, tm, tf):
    m, d = h.shape
    dff = w1.shape[1]
    return pl.pallas_call(
        _mlp_body,
        grid=(m // tm, dff // tf),
        in_specs=[
            pl.BlockSpec((tm, d), lambda i, f: (i, 0)),
            pl.BlockSpec((1, d), lambda i, f: (0, 0)),
            pl.BlockSpec((d, tf), lambda i, f: (0, f)),
            pl.BlockSpec((tf, d), lambda i, f: (f, 0)),
            pl.BlockSpec((1, d), lambda i, f: (0, 0)),
        ],
        out_specs=pl.BlockSpec((tm, d), lambda i, f: (i, 0)),
        out_shape=jax.ShapeDtypeStruct((m, d), F32),
        scratch_shapes=[pltpu.VMEM((tm, d), BF16)],
        compiler_params=_params("parallel", "arbitrary"),
        name="mlp",
    )(h, nw, w1, w2, fw)


def _rope_tables(seq, dk):
    half = dk // 2
    inv = np.exp(-math.log(ROPE_BASE) * np.arange(half, dtype=np.float64) / half)
    ang = np.arange(seq, dtype=np.float64)[:, None] * inv[None, :]
    cos, sin = np.cos(ang).astype(np.float32), np.sin(ang).astype(np.float32)
    return np.concatenate([cos, cos], axis=1), np.concatenate([-sin, sin], axis=1)


def kernel(x, mem, norm_mix_w, w_in, conv_w, conv_b, dt_bias, a_log, d_skip, ret_norm_w, ssd_norm_w, w_ret_out, w_ssd_out, w_mix_out, norm_xa_w, mem_norm_w, w_xq, w_xkv, w_xo, norm_ff_w, w_ff1, w_ff2, final_norm_w):
    bsz, seq, d = x.shape
    mem_len = mem.shape[1]
    depth = w_in.shape[0]
    n_tok = bsz * seq
    ret_qk = RET_HEADS * RET_DK
    ret_v = RET_HEADS * RET_DV
    inner = w_ssd_out.shape[1]
    conv_dim = conv_w.shape[2]
    n_heads = inner // SSD_HEAD_DIM
    ret_w = 2 * ret_qk + 2 * ret_v
    off_dt = ret_w + inner + conv_dim
    off_gate = off_dt + n_heads
    assert w_in.shape[2] == off_gate + 2 * d and depth == 1

    tm_main, tn_main = IN_PROJ_TILE
    assert conv_dim % tn_main == 0 and ret_w % tn_main == 0 and inner % tn_main == 0
    shift = inner // tn_main
    z_off, ret_off = 0, inner

    cos2, sin2 = _rope_tables(seq, RET_DK)
    row = lambda v: v.reshape(1, -1)
    pad_heads = lambda v: jnp.pad(v.reshape(1, -1), ((0, 0), (0, LANES - n_heads)))

    w_in_t = jnp.swapaxes(w_in, 1, 2)

    h = x.reshape(n_tok, d)
    for l in range(depth):
        xn = _prenorm(h, row(norm_mix_w[l]), PRENORM_ROWS)
        assert LANES - n_heads == n_heads
        proj, wt_tail, (w_ret_b, w_ssd_b, w_mix_b, w_xq_b, w_xo_b) = _in_proj_main(
            xn, w_in_t, l, 0, ret_w + inner, shift,
            (w_ret_out[l], w_ssd_out[l], w_mix_out[l], w_xq[l], w_xo[l]),
            off_gate, 2 * d, off_dt, n_heads, tm_main, tn_main)
        p_xbc = _in_proj_conv(xn, w_in_t, l, ret_w + inner, conv_dim, conv_w[l], row(conv_b[l]),
                              seq, tm_main, tn_main)
        gates, dt_raw = _in_proj_tail(xn, wt_tail, TAIL_ROWS, wt_tail.shape[0] // TAIL_COL_BLOCKS)
        y_ret = _retention(proj, ret_off, cos2, sin2, row(ret_norm_w[l]), bsz, seq)
        y_ssd = _ssd(proj, z_off, p_xbc, inner, dt_raw, pad_heads(dt_bias[l]), pad_heads(a_log[l]),
                     row(jnp.repeat(d_skip[l], SSD_HEAD_DIM)), row(ssd_norm_w[l]), bsz, seq)
        merged = _merge(y_ret, y_ssd, w_ret_b, w_ssd_b, gates, *MERGE_TILE)
        kv = _norm_matmul(mem.reshape(bsz * mem_len, d), row(mem_norm_w[l]), w_xkv[l],
                          BF16, bsz * mem_len, KV_COLS, "xa_kv")
        h = _mix_xattn(merged, h, kv, w_mix_b, row(norm_xa_w[l]), w_xq_b, w_xo_b,
                       bsz, seq, mem_len, XATTN_ROWS)
        h = _mlp(h, row(norm_ff_w[l]), w_ff1[l], w_ff2[l], row(final_norm_w), *MLP_TILE)
    return h.reshape(bsz, seq, d)
```

```python
import functools
import itertools
import math

import numpy as np
import jax
import jax.numpy as jnp
from jax import lax
from jax.experimental import pallas as pl
from jax.experimental.pallas import tpu as pltpu

F32 = jnp.float32
BF16 = jnp.bfloat16
EPS = 1e-6

RET_HEADS = 8
RET_DK = 128
RET_DV = 256
CHUNK = 128
ROPE_BASE = 10000.0
SSD_HEAD_DIM = 64
SSD_GROUPS = 8
SSD_STATE = 128
SSD_CONV = 4
CONV_HIST = 8
XA_HEADS = 4

V7X_VMEM_BYTES = 64 * 1024 * 1024
VMEM_LIMIT = V7X_VMEM_BYTES - 4 * 1024 * 1024
LANES = 128
ROW_SPLIT = 512
CONV_ROW_SPLIT = 256
LOG2_E = math.log2(math.e)

PRENORM_ROWS = 512
IN_PROJ_TILE = (2048, 1024)
CAST_SLABS = 32
TAIL_ROWS = 2048
TAIL_COL_BLOCKS = 3
MERGE_TILE = (1024, 512)
KV_COLS = 1024
XATTN_ROWS = 512
MLP_TILE = (1024, 1024)
RET_STEP_CHUNKS = 8
SSD_STEP_CHUNKS = 2

_LOG_G = np.log(1.0 - np.exp2(-5.0 - np.arange(RET_HEADS, dtype=np.float64)))
_RET_CHUNK_DECAY = np.exp(CHUNK * _LOG_G)


def _params(*sem):
    return pltpu.CompilerParams(dimension_semantics=sem, vmem_limit_bytes=VMEM_LIMIT)


def _rms(x, w):
    ms = jnp.mean(x * x, axis=-1, keepdims=True)
    return x * lax.rsqrt(ms + EPS) * w


def _sigmoid(x):
    return 0.5 + 0.5 * jnp.tanh(0.5 * x)


def _silu(x):
    h = 0.5 * x
    return h + h * jnp.tanh(h)


def _dot(a, b):
    return jnp.dot(a, b, preferred_element_type=F32)


def _dot_nt(a, b):
    return lax.dot_general(a, b, (((1,), (1,)), ((), ())), preferred_element_type=F32)


def _cast_slab_specs(weights, n_steps, step_of):
    assert CAST_SLABS <= n_steps and all(w.shape[0] % CAST_SLABS == 0 for w in weights)
    slab = lambda *idx: (jnp.minimum(step_of(*idx), CAST_SLABS - 1), 0)
    return [pl.BlockSpec((w.shape[0] // CAST_SLABS, w.shape[1]), slab) for w in weights]


def _cast_slabs(src_refs, dst_refs):
    for src_ref, dst_ref in zip(src_refs, dst_refs):
        dst_ref[...] = src_ref[...].astype(dst_ref.dtype)


def _row_blocks(n_rows, step=ROW_SPLIT):
    step = min(step, n_rows)
    return [slice(r, r + step) for r in range(0, n_rows, step)]


def _prenorm_body(x_ref, nw_ref, o_ref):
    o_ref[...] = _rms(x_ref[...], nw_ref[...]).astype(o_ref.dtype)


def _prenorm(x, nw, tm):
    m, k = x.shape
    return pl.pallas_call(
        _prenorm_body,
        grid=(m // tm,),
        in_specs=[pl.BlockSpec((tm, k), lambda i: (i, 0)), pl.BlockSpec((1, k), lambda i: (0, 0))],
        out_specs=pl.BlockSpec((tm, k), lambda i: (i, 0)),
        out_shape=jax.ShapeDtypeStruct((m, k), BF16),
        compiler_params=_params("parallel"),
        name="prenorm",
    )(x, nw)


def _matmul_nt_body(a_ref, wt_ref, t0_ref, t1_ref, *refs, nb, tail_blocks):
    n_cast = (len(refs) - 2) // 2
    o_ref, tail_ref = refs[n_cast], refs[n_cast + 1]
    wb = wt_ref[...].astype(BF16)
    for rows in _row_blocks(a_ref.shape[0]):
        o_ref[rows, :] = _dot_nt(a_ref[rows, :], wb).astype(o_ref.dtype)
    _cast_slabs(refs[:n_cast], refs[n_cast + 2:])
    slab = t0_ref.shape[0]
    step = pl.program_id(0) * nb + pl.program_id(1)
    tail_ref[0:slab, :] = t0_ref[...].astype(tail_ref.dtype)
    tail_ref[slab:2 * slab, :] = jnp.where(step >= tail_blocks - 1, 0.0, t1_ref[...]).astype(tail_ref.dtype)


def _in_proj_main(a, w3, layer, row_off, n_cols, shift, to_bf16, gate_row0, n_gate_rows,
                  dt_row0, slab, tm, tn):
    m, k = a.shape
    nb = n_cols // tn
    off = row_off // tn
    n_steps = (m // tm) * nb
    assert n_cols % tn == 0 and m % tm == 0 and row_off % tn == 0
    cast_specs = _cast_slab_specs(to_bf16, n_steps, lambda i, j: i * nb + j)
    n_gate = n_gate_rows // slab
    tail_blocks = (n_gate + 2) // 2
    assert gate_row0 % slab == 0 and dt_row0 % slab == 0 and n_gate_rows % slab == 0
    assert n_gate % 2 == 0 and tail_blocks <= n_steps
    tail_blk = lambda i, j: jnp.minimum(i * nb + j, tail_blocks - 1)

    def tail_src(t):
        def index(i, j):
            q = 2 * tail_blk(i, j) + t
            return (layer, jnp.where(q < n_gate, gate_row0 // slab + q, dt_row0 // slab), 0)
        return pl.BlockSpec((None, slab, k), index)

    body = functools.partial(_matmul_nt_body, nb=nb, tail_blocks=tail_blocks)
    outs = pl.pallas_call(
        body,
        grid=(m // tm, nb),
        in_specs=[
            pl.BlockSpec((tm, k), lambda i, j: (i, 0)),
            pl.BlockSpec((None, tn, k), lambda i, j: (layer, j + off, 0)),
            tail_src(0), tail_src(1),
        ] + cast_specs,
        out_specs=[pl.BlockSpec((tm, tn), lambda i, j: (i, (j + shift) % nb)),
                   pl.BlockSpec((2 * slab, k), lambda i, j: (tail_blk(i, j), 0))] + cast_specs,
        out_shape=[jax.ShapeDtypeStruct((m, n_cols), BF16),
                   jax.ShapeDtypeStruct((2 * slab * tail_blocks, k), BF16)]
        + [jax.ShapeDtypeStruct(w.shape, BF16) for w in to_bf16],
        compiler_params=_params("arbitrary", "arbitrary"),
        name="in_proj_main",
    )(a, w3, w3, w3, *to_bf16)
    return outs[0], outs[1], outs[2:]


def _in_proj_conv_body(a_ref, wt_ref, cw_ref, cb_ref, o_ref, carry_ref, *, tiles_per_seq):
    i = pl.program_id(0)
    j = pl.program_id(1)
    tn = wt_ref.shape[0]

    @pl.when(i % tiles_per_seq == 0)
    def _():
        carry_ref[j] = jnp.zeros((CONV_HIST, tn), F32)

    wb = wt_ref[...].astype(BF16)
    cwh = 0.5 * cw_ref[...]
    cbh = 0.5 * cb_ref[...]
    hist = carry_ref[j]
    for rows in _row_blocks(a_ref.shape[0], CONV_ROW_SPLIT):
        res = _dot_nt(a_ref[rows, :], wb)
        nt = res.shape[0] // CONV_HIST
        tiles = jnp.concatenate([hist, res], axis=0).reshape(nt + 1, CONV_HIST, tn)
        sub = lax.broadcasted_iota(jnp.int32, (nt, CONV_HIST, tn), 1)
        acc = cbh + cwh[SSD_CONV - 1:SSD_CONV, :] * res
        for k in range(SSD_CONV - 1):
            dist = SSD_CONV - 1 - k
            mixed = jnp.where(sub >= CONV_HIST - dist, tiles[:-1], tiles[1:])
            shifted = pltpu.roll(mixed, dist, 1).reshape(nt * CONV_HIST, tn)
            acc = acc + cwh[k:k + 1, :] * shifted
        o_ref[rows, :] = (acc + acc * jnp.tanh(acc)).astype(o_ref.dtype)
        hist = res[res.shape[0] - CONV_HIST:, :]
    carry_ref[j] = hist


def _in_proj_conv(a, w3, layer, row_off, n_cols, conv_w, conv_b, seq, tm, tn):
    m, k = a.shape
    nb = n_cols // tn
    off = row_off // tn
    assert n_cols % tn == 0 and m % tm == 0 and row_off % tn == 0 and seq % tm == 0
    assert SSD_CONV - 1 <= CONV_HIST
    body = functools.partial(_in_proj_conv_body, tiles_per_seq=seq // tm)
    return pl.pallas_call(
        body,
        grid=(m // tm, nb),
        in_specs=[
            pl.BlockSpec((tm, k), lambda i, j: (i, 0)),
            pl.BlockSpec((None, tn, k), lambda i, j: (layer, j + off, 0)),
            pl.BlockSpec((SSD_CONV, tn), lambda i, j: (0, j)),
            pl.BlockSpec((1, tn), lambda i, j: (0, j)),
        ],
        out_specs=pl.BlockSpec((tm, tn), lambda i, j: (i, j)),
        out_shape=jax.ShapeDtypeStruct((m, n_cols), BF16),
        scratch_shapes=[pltpu.VMEM((nb, CONV_HIST, tn), F32)],
        compiler_params=_params("arbitrary", "arbitrary"),
        name="in_proj_conv",
    )(a, w3, conv_w, conv_b)


def _in_proj_tail_body(a_ref, wt_ref, g_ref, dt_ref):
    tn = wt_ref.shape[0]
    for rows in _row_blocks(a_ref.shape[0]):
        res = _dot_nt(a_ref[rows, :], wt_ref[...])
        g_ref[rows, :] = res.astype(g_ref.dtype)
        dt_ref[rows, :] = res[:, tn - LANES:]


def _in_proj_tail(a, wt_tail, tm, tn):
    m, k = a.shape
    n = wt_tail.shape[0]
    assert n % tn == 0 and tn % LANES == 0
    return pl.pallas_call(
        _in_proj_tail_body,
        grid=(m // tm, n // tn),
        in_specs=[
            pl.BlockSpec((tm, k), lambda i, j: (i, 0)),
            pl.BlockSpec((tn, k), lambda i, j: (j, 0)),
        ],
        out_specs=[
            pl.BlockSpec((tm, tn), lambda i, j: (i, j)),
            pl.BlockSpec((tm, LANES), lambda i, j: (i, 0)),
        ],
        out_shape=[jax.ShapeDtypeStruct((m, n), BF16), jax.ShapeDtypeStruct((m, LANES), F32)],
        compiler_params=_params("parallel", "arbitrary"),
        name="in_proj_tail",
    )(a, wt_tail)


def _norm_matmul_body(x_ref, nw_ref, w_ref, o_ref, xn_ref):
    @pl.when(pl.program_id(1) == 0)
    def _():
        xn_ref[...] = _rms(x_ref[...], nw_ref[...]).astype(BF16)

    o_ref[...] = _dot(xn_ref[...], w_ref[...].astype(BF16)).astype(o_ref.dtype)


def _norm_matmul(x, nw, w, out_dtype, tm, tn, name):
    m, k = x.shape
    n = w.shape[1]
    return pl.pallas_call(
        _norm_matmul_body,
        grid=(m // tm, n // tn),
        in_specs=[
            pl.BlockSpec((tm, k), lambda i, j: (i, 0)),
            pl.BlockSpec((1, k), lambda i, j: (0, 0)),
            pl.BlockSpec((k, tn), lambda i, j: (0, j)),
        ],
        out_specs=pl.BlockSpec((tm, tn), lambda i, j: (i, j)),
        out_shape=jax.ShapeDtypeStruct((m, n), out_dtype),
        scratch_shapes=[pltpu.VMEM((tm, k), BF16)],
        compiler_params=_params("parallel", "arbitrary"),
        name=name,
    )(x, nw, w)


def _retention_tables():
    idx = np.arange(CHUNK, dtype=np.float64)
    rel = idx[:, None] - idx[None, :]
    lg = _LOG_G[:, None, None]
    scale = RET_DK ** -0.5
    decay = np.where(rel[None] >= 0, np.exp(np.maximum(rel, 0.0)[None] * lg), 0.0) * scale
    xi = np.exp((idx + 1.0)[None, :, None] * lg) * scale
    zeta = np.exp((CHUNK - 1.0 - idx)[None, :, None] * lg)
    wide = (RET_HEADS, CHUNK, RET_DK)
    return (decay.astype(np.float32), np.broadcast_to(xi, wide).astype(np.float32),
            np.broadcast_to(zeta, wide).astype(np.float32))


def _retention_body(q_ref, k_ref, v_ref, g_ref, cos_ref, sin_ref, dec_ref, xi_ref, zeta_ref,
                    nw_ref, o_ref, state_ref):
    @pl.when(pl.program_id(1) == 0)
    def _():
        state_ref[...] = jnp.zeros_like(state_ref)

    for cc, h in itertools.product(range(RET_STEP_CHUNKS), range(RET_HEADS)):
        rows = slice(cc * CHUNK, (cc + 1) * CHUNK)
        cos = cos_ref[rows, :]
        sin = sin_ref[rows, :]
        q = q_ref[rows, h * RET_DK:(h + 1) * RET_DK].astype(F32)
        k = k_ref[rows, h * RET_DK:(h + 1) * RET_DK].astype(F32)
        v = v_ref[rows, h * RET_DV:(h + 1) * RET_DV]
        qr = q * cos + pltpu.roll(q, RET_DK // 2, 1) * sin
        kr = k * cos + pltpu.roll(k, RET_DK // 2, 1) * sin
        s = _dot_nt(qr.astype(BF16), kr.astype(BF16)) * dec_ref[h]
        st = state_ref[h]
        lhs = jnp.concatenate([s.astype(BF16), (qr * xi_ref[h]).astype(BF16)], axis=1)
        rhs = jnp.concatenate([v, st.astype(BF16)], axis=0)
        y = _dot(lhs, rhs)
        kz_t = (kr * zeta_ref[h]).T.astype(BF16)
        state_ref[h] = float(_RET_CHUNK_DECAY[h]) * st + _dot(kz_t, v)
        mu = jnp.mean(y, axis=-1, keepdims=True)
        yc = y - mu
        var = jnp.mean(yc * yc, axis=-1, keepdims=True)
        yn = yc * lax.rsqrt(var + EPS)
        hsl = slice(h * RET_DV, (h + 1) * RET_DV)
        o_ref[rows, hsl] = (yn * nw_ref[:, hsl] * _silu(g_ref[rows, hsl].astype(F32))).astype(BF16)


def _retention(proj, col_off, cos2, sin2, nw, bsz, seq):
    rows = RET_STEP_CHUNKS * CHUNK
    nc = seq // rows
    qk = RET_HEADS * RET_DK
    vd = RET_HEADS * RET_DV
    assert vd == 2 * qk and col_off % vd == 0 and seq % rows == 0
    oq = col_off // qk
    ov = col_off // vd
    decay, xi, zeta = _retention_tables()
    table = lambda n: pl.BlockSpec((RET_HEADS, CHUNK, n), lambda b, c: (0, 0, 0))
    return pl.pallas_call(
        _retention_body,
        grid=(bsz, nc),
        in_specs=[
            pl.BlockSpec((rows, qk), lambda b, c: (b * nc + c, oq)),
            pl.BlockSpec((rows, qk), lambda b, c: (b * nc + c, oq + 1)),
            pl.BlockSpec((rows, vd), lambda b, c: (b * nc + c, ov + 1)),
            pl.BlockSpec((rows, vd), lambda b, c: (b * nc + c, ov + 2)),
            pl.BlockSpec((rows, RET_DK), lambda b, c: (c, 0)),
            pl.BlockSpec((rows, RET_DK), lambda b, c: (c, 0)),
            table(CHUNK), table(RET_DK), table(RET_DK),
            pl.BlockSpec((1, vd), lambda b, c: (0, 0)),
        ],
        out_specs=pl.BlockSpec((rows, vd), lambda b, c: (b * nc + c, 0)),
        out_shape=jax.ShapeDtypeStruct((bsz * seq, vd), BF16),
        scratch_shapes=[pltpu.VMEM((RET_HEADS, RET_DK, RET_DV), F32)],
        compiler_params=_params("parallel", "arbitrary"),
        name="retention",
    )(proj, proj, proj, proj, cos2, sin2, decay, xi, zeta, nw)


def _ssd_body(z_ref, xbc_ref, dt_ref, dtb_ref, alog_ref, dsk_ref, nw_ref, wsrc_ref,
              o_ref, wdst_ref, prev_ref, *, inner, heads_per_group):
    _cast_slabs([wsrc_ref], [wdst_ref])

    c = CHUNK
    n_sub = z_ref.shape[0] // c
    gw = heads_per_group * SSD_HEAD_DIM

    @pl.when(pl.program_id(1) == 0)
    def _():
        prev_ref[...] = jnp.zeros_like(prev_ref)

    row = lax.broadcasted_iota(jnp.int32, (c, c), 0)
    col = lax.broadcasted_iota(jnp.int32, (c, c), 1)
    tri = row >= col
    lane_lo = lax.broadcasted_iota(jnp.int32, (c, LANES), 1) < SSD_HEAD_DIM
    neg_inf = jnp.float32(-jnp.inf)
    a = -jnp.exp(alog_ref[...])

    for cc in range(n_sub):
        rows = slice(cc * c, (cc + 1) * c)
        dtr = dt_ref[rows, :] + dtb_ref[...]
        dt = jnp.maximum(dtr, 0.0) + jnp.log1p(jnp.exp(-jnp.abs(dtr)))
        da = dt * a
        a_cs = jnp.dot(tri.astype(F32), da, precision=lax.Precision.HIGHEST,
                       preferred_element_type=F32)
        a_cs = a_cs * LOG2_E
        a_last = a_cs[c - 1:c, :]
        w1 = jnp.exp2(a_last - a_cs) * dt
        src_t = (a_cs - jnp.log2(dt)).T
        w1_t = w1.T

        for g in range(SSD_GROUPS):
            b_gb = xbc_ref[rows, inner + g * SSD_STATE: inner + (g + 1) * SSD_STATE]
            c_gb = xbc_ref[rows, inner + (SSD_GROUPS + g) * SSD_STATE:
                           inner + (SSD_GROUPS + g + 1) * SSD_STATE]
            cb = _dot_nt(c_gb, b_gb).astype(BF16)
            b_gt = b_gb.astype(F32).T.astype(BF16)
            y_pairs = []
            for pr in range(heads_per_group // 2):
                h0 = g * heads_per_group + 2 * pr
                psl = slice(h0 * SSD_HEAD_DIM, (h0 + 2) * SSD_HEAD_DIM)
                xs_b = xbc_ref[rows, psl]
                prev = prev_ref[:, psl]
                rhs = jnp.concatenate([xs_b, prev.astype(BF16)], axis=0)
                lhs, bws, cds = [], [], []
                for h in (h0, h0 + 1):
                    colb = jnp.broadcast_to(a_cs[:, h:h + 1], (c, c))
                    rowb = jnp.broadcast_to(src_t[h:h + 1, :], (c, c))
                    m = cb * jnp.exp2(jnp.where(tri, colb - rowb, neg_inf)).astype(BF16)
                    e = jnp.exp2(colb)
                    lhs.append(jnp.concatenate([m, e.astype(BF16) * c_gb], axis=1))
                    bws.append(b_gt * jnp.broadcast_to(w1_t[h:h + 1, :], (c, c)).astype(BF16))
                    cds.append(e[c - 1:c, :])
                ys = _dot(jnp.concatenate(lhs, axis=0), rhs)
                sts = _dot(jnp.concatenate(bws, axis=0), xs_b)
                y_pairs.append(jnp.where(lane_lo, ys[0:c, :], ys[c:2 * c, :]))
                cd = jnp.where(lane_lo[0:1, :], cds[0], cds[1])
                prev_ref[:, psl] = cd * prev + jnp.where(lane_lo, sts[0:c, :], sts[c:2 * c, :])
            gsl = slice(g * gw, (g + 1) * gw)
            y = jnp.concatenate(y_pairs, axis=1) + dsk_ref[:, gsl] * xbc_ref[rows, gsl].astype(F32)
            gy = y * _silu(z_ref[rows, gsl].astype(F32))
            ms = jnp.mean(gy * gy, axis=-1, keepdims=True)
            o_ref[rows, gsl] = (gy * lax.rsqrt(ms + EPS) * nw_ref[:, gsl]).astype(BF16)


def _ssd(p_z, z_off, p_xbc, inner, dt_raw, dtb, alog, dskip, nw, w_to_bf16, bsz, seq):
    rows = SSD_STEP_CHUNKS * CHUNK
    nc = seq // rows
    n_heads = inner // SSD_HEAD_DIM
    heads_per_group = n_heads // SSD_GROUPS
    conv_dim = p_xbc.shape[1]
    assert conv_dim == inner + 2 * SSD_GROUPS * SSD_STATE and SSD_STATE == CHUNK and seq % rows == 0
    assert n_heads <= LANES and heads_per_group % 2 == 0 and 2 * SSD_HEAD_DIM == LANES
    assert z_off % inner == 0
    zb = z_off // inner
    body = functools.partial(_ssd_body, inner=inner, heads_per_group=heads_per_group)
    cast_spec = _cast_slab_specs([w_to_bf16], bsz * nc, lambda b, c: b * nc + c)
    row = lambda b, c: (b * nc + c, 0)
    fixed = lambda b, c: (0, 0)
    return pl.pallas_call(
        body,
        grid=(bsz, nc),
        in_specs=[
            pl.BlockSpec((rows, inner), lambda b, c: (b * nc + c, zb)),
            pl.BlockSpec((rows, conv_dim), row),
            pl.BlockSpec((rows, LANES), row),
            pl.BlockSpec((1, LANES), fixed),
            pl.BlockSpec((1, LANES), fixed),
            pl.BlockSpec((1, inner), fixed),
            pl.BlockSpec((1, inner), fixed),
        ] + cast_spec,
        out_specs=[pl.BlockSpec((rows, inner), row)] + cast_spec,
        out_shape=[jax.ShapeDtypeStruct((bsz * seq, inner), BF16),
                   jax.ShapeDtypeStruct(w_to_bf16.shape, BF16)],
        scratch_shapes=[pltpu.VMEM((SSD_STATE, inner), F32)],
        compiler_params=_params("arbitrary", "arbitrary"),
        name="ssd",
    )(p_z, p_xbc, dt_raw, dtb, alog, dskip, nw, w_to_bf16)


def _merge_body(yr_ref, ys_ref, wr_ref, ws_ref, ga_ref, gb_ref, wsrc_ref, o_ref, wdst_ref):
    _cast_slabs([wsrc_ref], [wdst_ref])
    for rows in _row_blocks(yr_ref.shape[0]):
        br = _dot(yr_ref[rows, :], wr_ref[...])
        bs = _dot(ys_ref[rows, :], ws_ref[...])
        ga = _sigmoid(ga_ref[rows, :].astype(F32))
        gb = _sigmoid(gb_ref[rows, :].astype(F32))
        o_ref[rows, :] = (ga * br + gb * bs).astype(o_ref.dtype)


def _merge(y_ret, y_ssd, w_ret, w_ssd, gates, w_to_bf16, tm, tn):
    m, kr = y_ret.shape
    ks = y_ssd.shape[1]
    n = w_ret.shape[1]
    nb = n // tn
    cast_spec = _cast_slab_specs([w_to_bf16], (m // tm) * nb, lambda i, j: i * nb + j)
    return pl.pallas_call(
        _merge_body,
        grid=(m // tm, nb),
        in_specs=[
            pl.BlockSpec((tm, kr), lambda i, j: (i, 0)),
            pl.BlockSpec((tm, ks), lambda i, j: (i, 0)),
            pl.BlockSpec((kr, tn), lambda i, j: (0, j)),
            pl.BlockSpec((ks, tn), lambda i, j: (0, j)),
            pl.BlockSpec((tm, tn), lambda i, j: (i, j)),
            pl.BlockSpec((tm, tn), lambda i, j: (i, j + nb)),
        ] + cast_spec,
        out_specs=[pl.BlockSpec((tm, tn), lambda i, j: (i, j))] + cast_spec,
        out_shape=[jax.ShapeDtypeStruct((m, n), BF16), jax.ShapeDtypeStruct(w_to_bf16.shape, BF16)],
        compiler_params=_params("arbitrary", "arbitrary"),
        name="merge",
    )(y_ret, y_ssd, w_ret, w_ssd, gates, gates, w_to_bf16)


def _mix_xattn_body(m_ref, x_ref, k_ref, v_ref, wmix_ref, nw_ref, wq_ref, wo_ref, o_ref, xn_ref):
    d = x_ref.shape[1]
    hd = d // XA_HEADS
    scale = hd ** -0.5
    o_ref[...] = x_ref[...] + _dot(m_ref[...], wmix_ref[...])
    xn_ref[...] = _rms(o_ref[...], nw_ref[...]).astype(BF16)
    for h in range(XA_HEADS):
        sl = slice(h * hd, (h + 1) * hd)
        q = _dot(xn_ref[...], wq_ref[:, sl]).astype(BF16)
        s = _dot_nt(q, k_ref[:, sl]) * scale
        s = s - jnp.max(s, axis=-1, keepdims=True)
        p = jnp.exp(s)
        p = p / jnp.sum(p, axis=-1, keepdims=True)
        o_h = _dot(p.astype(BF16), v_ref[:, sl]).astype(BF16)
        o_ref[...] += _dot(o_h, wo_ref[sl, :])


def _mix_xattn(merged, x, kv, w_mix, nw, w_q, w_o, bsz, seq, mem_len, tq):
    d = x.shape[1]
    nq = seq // tq
    tile = lambda b, i: (b * nq + i, 0)
    fixed = lambda b, i: (0, 0)
    resident = pl.Buffered(1)
    return pl.pallas_call(
        _mix_xattn_body,
        grid=(bsz, nq),
        in_specs=[
            pl.BlockSpec((tq, d), tile),
            pl.BlockSpec((tq, d), tile),
            pl.BlockSpec((mem_len, d), lambda b, i: (b, 0)),
            pl.BlockSpec((mem_len, d), lambda b, i: (b, 1)),
            pl.BlockSpec((d, d), fixed, pipeline_mode=resident),
            pl.BlockSpec((1, d), fixed),
            pl.BlockSpec((d, d), fixed, pipeline_mode=resident),
            pl.BlockSpec((d, d), fixed, pipeline_mode=resident),
        ],
        out_specs=pl.BlockSpec((tq, d), tile),
        out_shape=jax.ShapeDtypeStruct((bsz * seq, d), F32),
        scratch_shapes=[pltpu.VMEM((tq, d), BF16)],
        compiler_params=_params("parallel", "arbitrary"),
        name="mix_xattn",
    )(merged, x, kv, kv, w_mix, nw, w_q, w_o)


def _mlp_body(h_ref, nw_ref, w1_ref, w2_ref, fw_ref, o_ref, xn_ref):
    f = pl.program_id(1)
    blocks = _row_blocks(h_ref.shape[0])

    @pl.when(f == 0)
    def _():
        for rows in blocks:
            hh = h_ref[rows, :]
            xn_ref[rows, :] = _rms(hh, nw_ref[...]).astype(BF16)
            o_ref[rows, :] = hh

    for rows in blocks:
        u = _dot(xn_ref[rows, :], w1_ref[...])
        u = jnp.square(jnp.maximum(u, 0.0)).astype(BF16)
        o_ref[rows, :] += _dot(u, w2_ref[...])

    @pl.when(f == pl.num_programs(1) - 1)
    def _():
        for rows in blocks:
            o_ref[rows, :] = _rms(o_ref[rows, :], fw_ref[...])


def _mlp(h, nw, w1, w2, fw, tm, tf):
    m, d = h.shape
    dff = w1.shape[1]
    return pl.pallas_call(
        _mlp_body,
        grid=(m // tm, dff // tf),
        in_specs=[
            pl.BlockSpec((tm, d), lambda i, f: (i, 0)),
            pl.BlockSpec((1, d), lambda i, f: (0, 0)),
            pl.BlockSpec((d, tf), lambda i, f: (0, f)),
            pl.BlockSpec((tf, d), lambda i, f: (f, 0)),
            pl.BlockSpec((1, d), lambda i, f: (0, 0)),
        ],
        out_specs=pl.BlockSpec((tm, d), lambda i, f: (i, 0)),
        out_shape=jax.ShapeDtypeStruct((m, d), F32),
        scratch_shapes=[pltpu.VMEM((tm, d), BF16)],
        compiler_params=_params("parallel", "arbitrary"),
        name="mlp",
    )(h, nw, w1, w2, fw)


def _rope_tables(seq, dk):
    half = dk // 2
    inv = np.exp(-math.log(ROPE_BASE) * np.arange(half, dtype=np.float64) / half)
    ang = np.arange(seq, dtype=np.float64)[:, None] * inv[None, :]
    cos, sin = np.cos(ang).astype(np.float32), np.sin(ang).astype(np.float32)
    return np.concatenate([cos, cos], axis=1), np.concatenate([-sin, sin], axis=1)


def kernel(x, mem, norm_mix_w, w_in, conv_w, conv_b, dt_bias, a_log, d_skip, ret_norm_w, ssd_norm_w, w_ret_out, w_ssd_out, w_mix_out, norm_xa_w, mem_norm_w, w_xq, w_xkv, w_xo, norm_ff_w, w_ff1, w_ff2, final_norm_w):
    bsz, seq, d = x.shape
    mem_len = mem.shape[1]
    depth = w_in.shape[0]
    n_tok = bsz * seq
    ret_qk = RET_HEADS * RET_DK
    ret_v = RET_HEADS * RET_DV
    inner = w_ssd_out.shape[1]
    conv_dim = conv_w.shape[2]
    n_heads = inner // SSD_HEAD_DIM
    ret_w = 2 * ret_qk + 2 * ret_v
    off_dt = ret_w + inner + conv_dim
    off_gate = off_dt + n_heads
    assert w_in.shape[2] == off_gate + 2 * d and depth == 1

    tm_main, tn_main = IN_PROJ_TILE
    assert conv_dim % tn_main == 0 and ret_w % tn_main == 0 and inner % tn_main == 0
    shift = inner // tn_main
    z_off, ret_off = 0, inner

    cos2, sin2 = _rope_tables(seq, RET_DK)
    row = lambda v: v.reshape(1, -1)
    pad_heads = lambda v: jnp.pad(v.reshape(1, -1), ((0, 0), (0, LANES - n_heads)))

    w_in_t = jnp.swapaxes(w_in, 1, 2)

    h = x.reshape(n_tok, d)
    for l in range(depth):
        xn = _prenorm(h, row(norm_mix_w[l]), PRENORM_ROWS)
        assert LANES - n_heads == n_heads
        proj, wt_tail, (w_ret_b, w_ssd_b, w_mix_b, w_xq_b, w_xo_b) = _in_proj_main(
            xn, w_in_t, l, 0, ret_w + inner, shift,
            (w_ret_out[l], w_ssd_out[l], w_mix_out[l], w_xq[l], w_xo[l]),
            off_gate, 2 * d, off_dt, n_heads, tm_main, tn_main)
        p_xbc = _in_proj_conv(xn, w_in_t, l, ret_w + inner, conv_dim, conv_w[l], row(conv_b[l]),
                              seq, tm_main, tn_main)
        gates, dt_raw = _in_proj_tail(xn, wt_tail, TAIL_ROWS, wt_tail.shape[0] // TAIL_COL_BLOCKS)
        y_ret = _retention(proj, ret_off, cos2, sin2, row(ret_norm_w[l]), bsz, seq)
        y_ssd, w_ff1_b = _ssd(proj, z_off, p_xbc, inner, dt_raw, pad_heads(dt_bias[l]),
                              pad_heads(a_log[l]), row(jnp.repeat(d_skip[l], SSD_HEAD_DIM)),
                              row(ssd_norm_w[l]), w_ff1[l], bsz, seq)
        merged, w_ff2_b = _merge(y_ret, y_ssd, w_ret_b, w_ssd_b, gates, w_ff2[l], *MERGE_TILE)
        kv = _norm_matmul(mem.reshape(bsz * mem_len, d), row(mem_norm_w[l]), w_xkv[l],
                          BF16, bsz * mem_len, KV_COLS, "xa_kv")
        h = _mix_xattn(merged, h, kv, w_mix_b, row(norm_xa_w[l]), w_xq_b, w_xo_b,
                       bsz, seq, mem_len, XATTN_ROWS)
        h = _mlp(h, row(norm_ff_w[l]), w_ff1_b, w_ff2_b, row(final_norm_w), *MLP_TILE)
    return h.reshape(bsz, seq, d)
```

```python
import functools
import itertools
import math

import numpy as np
import jax
import jax.numpy as jnp
from jax import lax
from jax.experimental import pallas as pl
from jax.experimental.pallas import tpu as pltpu

F32 = jnp.float32
BF16 = jnp.bfloat16
EPS = 1e-6

RET_HEADS = 8
RET_DK = 128
RET_DV = 256
CHUNK = 128
ROPE_BASE = 10000.0
SSD_HEAD_DIM = 64
SSD_GROUPS = 8
SSD_STATE = 128
SSD_CONV = 4
CONV_HIST = 8
XA_HEADS = 4

V7X_VMEM_BYTES = 64 * 1024 * 1024
VMEM_LIMIT = V7X_VMEM_BYTES - 4 * 1024 * 1024
LANES = 128
ROW_SPLIT = 512
CONV_ROW_SPLIT = 256
LOG2_E = math.log2(math.e)

PRENORM_ROWS = 512
IN_PROJ_TILE = (2048, 1024)
CAST_SLABS = 32
GATES_TILE = (2048, 2048)
MERGE_TILE = (1024, 512)
KV_COLS = 1024
XATTN_ROWS = 512
MLP_TILE = (1024, 1024)
RET_STEP_CHUNKS = 8
SSD_STEP_CHUNKS = 2

_LOG_G = np.log(1.0 - np.exp2(-5.0 - np.arange(RET_HEADS, dtype=np.float64)))
_RET_CHUNK_DECAY = np.exp(CHUNK * _LOG_G)


def _params(*sem):
    return pltpu.CompilerParams(dimension_semantics=sem, vmem_limit_bytes=VMEM_LIMIT)


def _rms(x, w):
    ms = jnp.mean(x * x, axis=-1, keepdims=True)
    return x * lax.rsqrt(ms + EPS) * w


def _sigmoid(x):
    return 0.5 + 0.5 * jnp.tanh(0.5 * x)


def _silu(x):
    h = 0.5 * x
    return h + h * jnp.tanh(h)


def _dot(a, b):
    return jnp.dot(a, b, preferred_element_type=F32)


def _dot_nt(a, b):
    return lax.dot_general(a, b, (((1,), (1,)), ((), ())), preferred_element_type=F32)


def _cast_slab_specs(weights, n_steps, step_of):
    assert CAST_SLABS <= n_steps and all(w.shape[0] % CAST_SLABS == 0 for w in weights)
    slab = lambda *idx: (jnp.minimum(step_of(*idx), CAST_SLABS - 1), 0)
    return [pl.BlockSpec((w.shape[0] // CAST_SLABS, w.shape[1]), slab) for w in weights]


def _cast_slabs(src_refs, dst_refs):
    for src_ref, dst_ref in zip(src_refs, dst_refs):
        dst_ref[...] = src_ref[...].astype(dst_ref.dtype)


def _row_blocks(n_rows, step=ROW_SPLIT):
    step = min(step, n_rows)
    return [slice(r, r + step) for r in range(0, n_rows, step)]


def _prenorm_body(x_ref, nw_ref, wdt_ref, o_ref, dt_ref):
    xn = _rms(x_ref[...], nw_ref[...]).astype(o_ref.dtype)
    o_ref[...] = xn
    dt_ref[...] = _dot_nt(xn, wdt_ref[...].astype(BF16))


def _prenorm(x, nw, wt_dt, tm):
    m, k = x.shape
    n_dt = wt_dt.shape[0]
    return pl.pallas_call(
        _prenorm_body,
        grid=(m // tm,),
        in_specs=[pl.BlockSpec((tm, k), lambda i: (i, 0)), pl.BlockSpec((1, k), lambda i: (0, 0)),
                  pl.BlockSpec((n_dt, k), lambda i: (0, 0))],
        out_specs=[pl.BlockSpec((tm, k), lambda i: (i, 0)), pl.BlockSpec((tm, n_dt), lambda i: (i, 0))],
        out_shape=[jax.ShapeDtypeStruct((m, k), BF16), jax.ShapeDtypeStruct((m, n_dt), F32)],
        compiler_params=_params("parallel"),
        name="prenorm",
    )(x, nw, wt_dt)


def _matmul_nt_body(a_ref, wt_ref, g0_ref, g1_ref, *refs):
    n_cast = (len(refs) - 2) // 2
    o_ref, gate_w_ref = refs[n_cast], refs[n_cast + 1]
    wb = wt_ref[...].astype(BF16)
    for rows in _row_blocks(a_ref.shape[0]):
        o_ref[rows, :] = _dot_nt(a_ref[rows, :], wb).astype(o_ref.dtype)
    _cast_slabs(refs[:n_cast], refs[n_cast + 2:])
    slab = g0_ref.shape[0]
    gate_w_ref[0:slab, :] = g0_ref[...].astype(gate_w_ref.dtype)
    gate_w_ref[slab:2 * slab, :] = g1_ref[...].astype(gate_w_ref.dtype)


def _in_proj_main(a, w3, layer, row_off, n_cols, shift, to_bf16, gate_row0, n_gate_rows,
                  slab, tm, tn):
    m, k = a.shape
    nb = n_cols // tn
    off = row_off // tn
    n_steps = (m // tm) * nb
    assert n_cols % tn == 0 and m % tm == 0 and row_off % tn == 0
    cast_specs = _cast_slab_specs(to_bf16, n_steps, lambda i, j: i * nb + j)
    gate_blocks = n_gate_rows // (2 * slab)
    gate0 = gate_row0 // slab
    assert gate_row0 % slab == 0 and n_gate_rows % (2 * slab) == 0 and gate_blocks <= n_steps
    gate_blk = lambda i, j: jnp.minimum(i * nb + j, gate_blocks - 1)
    gate_src = lambda t: pl.BlockSpec(
        (None, slab, k), lambda i, j: (layer, gate0 + 2 * gate_blk(i, j) + t, 0))

    outs = pl.pallas_call(
        _matmul_nt_body,
        grid=(m // tm, nb),
        in_specs=[
            pl.BlockSpec((tm, k), lambda i, j: (i, 0)),
            pl.BlockSpec((None, tn, k), lambda i, j: (layer, j + off, 0)),
            gate_src(0), gate_src(1),
        ] + cast_specs,
        out_specs=[pl.BlockSpec((tm, tn), lambda i, j: (i, (j + shift) % nb)),
                   pl.BlockSpec((2 * slab, k), lambda i, j: (gate_blk(i, j), 0))] + cast_specs,
        out_shape=[jax.ShapeDtypeStruct((m, n_cols), BF16),
                   jax.ShapeDtypeStruct((n_gate_rows, k), BF16)]
        + [jax.ShapeDtypeStruct(w.shape, BF16) for w in to_bf16],
        compiler_params=_params("arbitrary", "arbitrary"),
        name="in_proj_main",
    )(a, w3, w3, w3, *to_bf16)
    return outs[0], outs[1], outs[2:]


def _in_proj_conv_body(a_ref, wt_ref, cw_ref, cb_ref, o_ref, carry_ref, *, tiles_per_seq):
    i = pl.program_id(0)
    j = pl.program_id(1)
    tn = wt_ref.shape[0]

    @pl.when(i % tiles_per_seq == 0)
    def _():
        carry_ref[j] = jnp.zeros((CONV_HIST, tn), F32)

    wb = wt_ref[...].astype(BF16)
    cwh = 0.5 * cw_ref[...]
    cbh = 0.5 * cb_ref[...]
    hist = carry_ref[j]
    for rows in _row_blocks(a_ref.shape[0], CONV_ROW_SPLIT):
        res = _dot_nt(a_ref[rows, :], wb)
        nt = res.shape[0] // CONV_HIST
        tiles = jnp.concatenate([hist, res], axis=0).reshape(nt + 1, CONV_HIST, tn)
        sub = lax.broadcasted_iota(jnp.int32, (nt, CONV_HIST, tn), 1)
        acc = cbh + cwh[SSD_CONV - 1:SSD_CONV, :] * res
        for k in range(SSD_CONV - 1):
            dist = SSD_CONV - 1 - k
            mixed = jnp.where(sub >= CONV_HIST - dist, tiles[:-1], tiles[1:])
            shifted = pltpu.roll(mixed, dist, 1).reshape(nt * CONV_HIST, tn)
            acc = acc + cwh[k:k + 1, :] * shifted
        o_ref[rows, :] = (acc + acc * jnp.tanh(acc)).astype(o_ref.dtype)
        hist = res[res.shape[0] - CONV_HIST:, :]
    carry_ref[j] = hist


def _in_proj_conv(a, w3, layer, row_off, n_cols, conv_w, conv_b, seq, tm, tn):
    m, k = a.shape
    nb = n_cols // tn
    off = row_off // tn
    assert n_cols % tn == 0 and m % tm == 0 and row_off % tn == 0 and seq % tm == 0
    assert SSD_CONV - 1 <= CONV_HIST
    body = functools.partial(_in_proj_conv_body, tiles_per_seq=seq // tm)
    return pl.pallas_call(
        body,
        grid=(m // tm, nb),
        in_specs=[
            pl.BlockSpec((tm, k), lambda i, j: (i, 0)),
            pl.BlockSpec((None, tn, k), lambda i, j: (layer, j + off, 0)),
            pl.BlockSpec((SSD_CONV, tn), lambda i, j: (0, j)),
            pl.BlockSpec((1, tn), lambda i, j: (0, j)),
        ],
        out_specs=pl.BlockSpec((tm, tn), lambda i, j: (i, j)),
        out_shape=jax.ShapeDtypeStruct((m, n_cols), BF16),
        scratch_shapes=[pltpu.VMEM((nb, CONV_HIST, tn), F32)],
        compiler_params=_params("arbitrary", "arbitrary"),
        name="in_proj_conv",
    )(a, w3, conv_w, conv_b)


def _in_proj_gates_body(a_ref, wt_ref, g_ref):
    for rows in _row_blocks(a_ref.shape[0]):
        g_ref[rows, :] = _dot_nt(a_ref[rows, :], wt_ref[...]).astype(g_ref.dtype)


def _in_proj_gates(a, wt, tm, tn):
    m, k = a.shape
    n = wt.shape[0]
    assert n % tn == 0 and m % tm == 0
    return pl.pallas_call(
        _in_proj_gates_body,
        grid=(m // tm, n // tn),
        in_specs=[
            pl.BlockSpec((tm, k), lambda i, j: (i, 0)),
            pl.BlockSpec((tn, k), lambda i, j: (j, 0)),
        ],
        out_specs=pl.BlockSpec((tm, tn), lambda i, j: (i, j)),
        out_shape=jax.ShapeDtypeStruct((m, n), BF16),
        compiler_params=_params("parallel", "arbitrary"),
        name="in_proj_gates",
    )(a, wt)


def _norm_matmul_body(x_ref, nw_ref, w_ref, o_ref, xn_ref):
    @pl.when(pl.program_id(1) == 0)
    def _():
        xn_ref[...] = _rms(x_ref[...], nw_ref[...]).astype(BF16)

    o_ref[...] = _dot(xn_ref[...], w_ref[...].astype(BF16)).astype(o_ref.dtype)


def _norm_matmul(x, nw, w, out_dtype, tm, tn, name):
    m, k = x.shape
    n = w.shape[1]
    return pl.pallas_call(
        _norm_matmul_body,
        grid=(m // tm, n // tn),
        in_specs=[
            pl.BlockSpec((tm, k), lambda i, j: (i, 0)),
            pl.BlockSpec((1, k), lambda i, j: (0, 0)),
            pl.BlockSpec((k, tn), lambda i, j: (0, j)),
        ],
        out_specs=pl.BlockSpec((tm, tn), lambda i, j: (i, j)),
        out_shape=jax.ShapeDtypeStruct((m, n), out_dtype),
        scratch_shapes=[pltpu.VMEM((tm, k), BF16)],
        compiler_params=_params("parallel", "arbitrary"),
        name=name,
    )(x, nw, w)


def _retention_tables():
    idx = np.arange(CHUNK, dtype=np.float64)
    rel = idx[:, None] - idx[None, :]
    lg = _LOG_G[:, None, None]
    scale = RET_DK ** -0.5
    decay = np.where(rel[None] >= 0, np.exp(np.maximum(rel, 0.0)[None] * lg), 0.0) * scale
    xi = np.exp((idx + 1.0)[None, :, None] * lg) * scale
    zeta = np.exp((CHUNK - 1.0 - idx)[None, :, None] * lg)
    wide = (RET_HEADS, CHUNK, RET_DK)
    return (decay.astype(np.float32), np.broadcast_to(xi, wide).astype(np.float32),
            np.broadcast_to(zeta, wide).astype(np.float32))


def _retention_body(q_ref, k_ref, v_ref, g_ref, cos_ref, sin_ref, dec_ref, xi_ref, zeta_ref,
                    nw_ref, o_ref, state_ref):
    @pl.when(pl.program_id(1) == 0)
    def _():
        state_ref[...] = jnp.zeros_like(state_ref)

    for cc, h in itertools.product(range(RET_STEP_CHUNKS), range(RET_HEADS)):
        rows = slice(cc * CHUNK, (cc + 1) * CHUNK)
        cos = cos_ref[rows, :]
        sin = sin_ref[rows, :]
        q = q_ref[rows, h * RET_DK:(h + 1) * RET_DK].astype(F32)
        k = k_ref[rows, h * RET_DK:(h + 1) * RET_DK].astype(F32)
        v = v_ref[rows, h * RET_DV:(h + 1) * RET_DV]
        qr = q * cos + pltpu.roll(q, RET_DK // 2, 1) * sin
        kr = k * cos + pltpu.roll(k, RET_DK // 2, 1) * sin
        s = _dot_nt(qr.astype(BF16), kr.astype(BF16)) * dec_ref[h]
        st = state_ref[h]
        lhs = jnp.concatenate([s.astype(BF16), (qr * xi_ref[h]).astype(BF16)], axis=1)
        rhs = jnp.concatenate([v, st.astype(BF16)], axis=0)
        y = _dot(lhs, rhs)
        kz_t = (kr * zeta_ref[h]).T.astype(BF16)
        state_ref[h] = float(_RET_CHUNK_DECAY[h]) * st + _dot(kz_t, v)
        mu = jnp.mean(y, axis=-1, keepdims=True)
        yc = y - mu
        var = jnp.mean(yc * yc, axis=-1, keepdims=True)
        yn = yc * lax.rsqrt(var + EPS)
        hsl = slice(h * RET_DV, (h + 1) * RET_DV)
        o_ref[rows, hsl] = (yn * nw_ref[:, hsl] * _silu(g_ref[rows, hsl].astype(F32))).astype(BF16)


def _retention(proj, col_off, cos2, sin2, nw, bsz, seq):
    rows = RET_STEP_CHUNKS * CHUNK
    nc = seq // rows
    qk = RET_HEADS * RET_DK
    vd = RET_HEADS * RET_DV
    assert vd == 2 * qk and col_off % vd == 0 and seq % rows == 0
    oq = col_off // qk
    ov = col_off // vd
    decay, xi, zeta = _retention_tables()
    table = lambda n: pl.BlockSpec((RET_HEADS, CHUNK, n), lambda b, c: (0, 0, 0))
    return pl.pallas_call(
        _retention_body,
        grid=(bsz, nc),
        in_specs=[
            pl.BlockSpec((rows, qk), lambda b, c: (b * nc + c, oq)),
            pl.BlockSpec((rows, qk), lambda b, c: (b * nc + c, oq + 1)),
            pl.BlockSpec((rows, vd), lambda b, c: (b * nc + c, ov + 1)),
            pl.BlockSpec((rows, vd), lambda b, c: (b * nc + c, ov + 2)),
            pl.BlockSpec((rows, RET_DK), lambda b, c: (c, 0)),
            pl.BlockSpec((rows, RET_DK), lambda b, c: (c, 0)),
            table(CHUNK), table(RET_DK), table(RET_DK),
            pl.BlockSpec((1, vd), lambda b, c: (0, 0)),
        ],
        out_specs=pl.BlockSpec((rows, vd), lambda b, c: (b * nc + c, 0)),
        out_shape=jax.ShapeDtypeStruct((bsz * seq, vd), BF16),
        scratch_shapes=[pltpu.VMEM((RET_HEADS, RET_DK, RET_DV), F32)],
        compiler_params=_params("parallel", "arbitrary"),
        name="retention",
    )(proj, proj, proj, proj, cos2, sin2, decay, xi, zeta, nw)


def _ssd_body(z_ref, xbc_ref, dt_ref, dtb_ref, alog_ref, dsk_ref, nw_ref, wsrc_ref,
              o_ref, wdst_ref, prev_ref, *, inner, heads_per_group):
    _cast_slabs([wsrc_ref], [wdst_ref])

    c = CHUNK
    n_sub = z_ref.shape[0] // c
    gw = heads_per_group * SSD_HEAD_DIM

    @pl.when(pl.program_id(1) == 0)
    def _():
        prev_ref[...] = jnp.zeros_like(prev_ref)

    row = lax.broadcasted_iota(jnp.int32, (c, c), 0)
    col = lax.broadcasted_iota(jnp.int32, (c, c), 1)
    tri = row >= col
    lane_lo = lax.broadcasted_iota(jnp.int32, (c, LANES), 1) < SSD_HEAD_DIM
    neg_inf = jnp.float32(-jnp.inf)
    a = -jnp.exp(alog_ref[...])

    for cc in range(n_sub):
        rows = slice(cc * c, (cc + 1) * c)
        dtr = dt_ref[rows, :] + dtb_ref[...]
        dt = jnp.maximum(dtr, 0.0) + jnp.log1p(jnp.exp(-jnp.abs(dtr)))
        da = dt * a
        a_cs = jnp.dot(tri.astype(F32), da, precision=lax.Precision.HIGHEST,
                       preferred_element_type=F32)
        a_cs = a_cs * LOG2_E
        a_last = a_cs[c - 1:c, :]
        w1 = jnp.exp2(a_last - a_cs) * dt
        src_t = (a_cs - jnp.log2(dt)).T
        w1_t = w1.T

        for g in range(SSD_GROUPS):
            b_gb = xbc_ref[rows, inner + g * SSD_STATE: inner + (g + 1) * SSD_STATE]
            c_gb = xbc_ref[rows, inner + (SSD_GROUPS + g) * SSD_STATE:
                           inner + (SSD_GROUPS + g + 1) * SSD_STATE]
            cb = _dot_nt(c_gb, b_gb).astype(BF16)
            b_gt = b_gb.astype(F32).T.astype(BF16)
            y_pairs = []
            for pr in range(heads_per_group // 2):
                h0 = g * heads_per_group + 2 * pr
                psl = slice(h0 * SSD_HEAD_DIM, (h0 + 2) * SSD_HEAD_DIM)
                xs_b = xbc_ref[rows, psl]
                prev = prev_ref[:, psl]
                rhs = jnp.concatenate([xs_b, prev.astype(BF16)], axis=0)
                lhs, bws, cds = [], [], []
                for h in (h0, h0 + 1):
                    colb = jnp.broadcast_to(a_cs[:, h:h + 1], (c, c))
                    rowb = jnp.broadcast_to(src_t[h:h + 1, :], (c, c))
                    m = cb * jnp.exp2(jnp.where(tri, colb - rowb, neg_inf)).astype(BF16)
                    e = jnp.exp2(colb)
                    lhs.append(jnp.concatenate([m, e.astype(BF16) * c_gb], axis=1))
                    bws.append(b_gt * jnp.broadcast_to(w1_t[h:h + 1, :], (c, c)).astype(BF16))
                    cds.append(e[c - 1:c, :])
                ys = _dot(jnp.concatenate(lhs, axis=0), rhs)
                sts = _dot(jnp.concatenate(bws, axis=0), xs_b)
                y_pairs.append(jnp.where(lane_lo, ys[0:c, :], ys[c:2 * c, :]))
                cd = jnp.where(lane_lo[0:1, :], cds[0], cds[1])
                prev_ref[:, psl] = cd * prev + jnp.where(lane_lo, sts[0:c, :], sts[c:2 * c, :])
            gsl = slice(g * gw, (g + 1) * gw)
            y = jnp.concatenate(y_pairs, axis=1) + dsk_ref[:, gsl] * xbc_ref[rows, gsl].astype(F32)
            gy = y * _silu(z_ref[rows, gsl].astype(F32))
            ms = jnp.mean(gy * gy, axis=-1, keepdims=True)
            o_ref[rows, gsl] = (gy * lax.rsqrt(ms + EPS) * nw_ref[:, gsl]).astype(BF16)


def _ssd(p_z, z_off, p_xbc, inner, dt_raw, dtb, alog, dskip, nw, w_to_bf16, bsz, seq):
    rows = SSD_STEP_CHUNKS * CHUNK
    nc = seq // rows
    n_heads = inner // SSD_HEAD_DIM
    heads_per_group = n_heads // SSD_GROUPS
    conv_dim = p_xbc.shape[1]
    assert conv_dim == inner + 2 * SSD_GROUPS * SSD_STATE and SSD_STATE == CHUNK and seq % rows == 0
    assert n_heads <= LANES and heads_per_group % 2 == 0 and 2 * SSD_HEAD_DIM == LANES
    assert z_off % inner == 0
    zb = z_off // inner
    body = functools.partial(_ssd_body, inner=inner, heads_per_group=heads_per_group)
    cast_spec = _cast_slab_specs([w_to_bf16], bsz * nc, lambda b, c: b * nc + c)
    row = lambda b, c: (b * nc + c, 0)
    fixed = lambda b, c: (0, 0)
    return pl.pallas_call(
        body,
        grid=(bsz, nc),
        in_specs=[
            pl.BlockSpec((rows, inner), lambda b, c: (b * nc + c, zb)),
            pl.BlockSpec((rows, conv_dim), row),
            pl.BlockSpec((rows, LANES), row),
            pl.BlockSpec((1, LANES), fixed),
            pl.BlockSpec((1, LANES), fixed),
            pl.BlockSpec((1, inner), fixed),
            pl.BlockSpec((1, inner), fixed),
        ] + cast_spec,
        out_specs=[pl.BlockSpec((rows, inner), row)] + cast_spec,
        out_shape=[jax.ShapeDtypeStruct((bsz * seq, inner), BF16),
                   jax.ShapeDtypeStruct(w_to_bf16.shape, BF16)],
        scratch_shapes=[pltpu.VMEM((SSD_STATE, inner), F32)],
        compiler_params=_params("arbitrary", "arbitrary"),
        name="ssd",
    )(p_z, p_xbc, dt_raw, dtb, alog, dskip, nw, w_to_bf16)


def _merge_body(yr_ref, ys_ref, wr_ref, ws_ref, ga_ref, gb_ref, wsrc_ref, o_ref, wdst_ref):
    _cast_slabs([wsrc_ref], [wdst_ref])
    for rows in _row_blocks(yr_ref.shape[0]):
        br = _dot(yr_ref[rows, :], wr_ref[...])
        bs = _dot(ys_ref[rows, :], ws_ref[...])
        ga = _sigmoid(ga_ref[rows, :].astype(F32))
        gb = _sigmoid(gb_ref[rows, :].astype(F32))
        o_ref[rows, :] = (ga * br + gb * bs).astype(o_ref.dtype)


def _merge(y_ret, y_ssd, w_ret, w_ssd, gates, w_to_bf16, tm, tn):
    m, kr = y_ret.shape
    ks = y_ssd.shape[1]
    n = w_ret.shape[1]
    nb = n // tn
    cast_spec = _cast_slab_specs([w_to_bf16], (m // tm) * nb, lambda i, j: i * nb + j)
    return pl.pallas_call(
        _merge_body,
        grid=(m // tm, nb),
        in_specs=[
            pl.BlockSpec((tm, kr), lambda i, j: (i, 0)),
            pl.BlockSpec((tm, ks), lambda i, j: (i, 0)),
            pl.BlockSpec((kr, tn), lambda i, j: (0, j)),
            pl.BlockSpec((ks, tn), lambda i, j: (0, j)),
            pl.BlockSpec((tm, tn), lambda i, j: (i, j)),
            pl.BlockSpec((tm, tn), lambda i, j: (i, j + nb)),
        ] + cast_spec,
        out_specs=[pl.BlockSpec((tm, tn), lambda i, j: (i, j))] + cast_spec,
        out_shape=[jax.ShapeDtypeStruct((m, n), BF16), jax.ShapeDtypeStruct(w_to_bf16.shape, BF16)],
        compiler_params=_params("arbitrary", "arbitrary"),
        name="merge",
    )(y_ret, y_ssd, w_ret, w_ssd, gates, gates, w_to_bf16)


def _mix_xattn_body(m_ref, x_ref, k_ref, v_ref, wmix_ref, nw_ref, wq_ref, wo_ref, o_ref, xn_ref):
    d = x_ref.shape[1]
    hd = d // XA_HEADS
    scale = hd ** -0.5
    o_ref[...] = x_ref[...] + _dot(m_ref[...], wmix_ref[...])
    xn_ref[...] = _rms(o_ref[...], nw_ref[...]).astype(BF16)
    for h in range(XA_HEADS):
        sl = slice(h * hd, (h + 1) * hd)
        q = _dot(xn_ref[...], wq_ref[:, sl]).astype(BF16)
        s = _dot_nt(q, k_ref[:, sl]) * scale
        s = s - jnp.max(s, axis=-1, keepdims=True)
        p = jnp.exp(s)
        p = p / jnp.sum(p, axis=-1, keepdims=True)
        o_h = _dot(p.astype(BF16), v_ref[:, sl]).astype(BF16)
        o_ref[...] += _dot(o_h, wo_ref[sl, :])


def _mix_xattn(merged, x, kv, w_mix, nw, w_q, w_o, bsz, seq, mem_len, tq):
    d = x.shape[1]
    nq = seq // tq
    tile = lambda b, i: (b * nq + i, 0)
    fixed = lambda b, i: (0, 0)
    resident = pl.Buffered(1)
    return pl.pallas_call(
        _mix_xattn_body,
        grid=(bsz, nq),
        in_specs=[
            pl.BlockSpec((tq, d), tile),
            pl.BlockSpec((tq, d), tile),
            pl.BlockSpec((mem_len, d), lambda b, i: (b, 0)),
            pl.BlockSpec((mem_len, d), lambda b, i: (b, 1)),
            pl.BlockSpec((d, d), fixed, pipeline_mode=resident),
            pl.BlockSpec((1, d), fixed),
            pl.BlockSpec((d, d), fixed, pipeline_mode=resident),
            pl.BlockSpec((d, d), fixed, pipeline_mode=resident),
        ],
        out_specs=pl.BlockSpec((tq, d), tile),
        out_shape=jax.ShapeDtypeStruct((bsz * seq, d), F32),
        scratch_shapes=[pltpu.VMEM((tq, d), BF16)],
        compiler_params=_params("parallel", "arbitrary"),
        name="mix_xattn",
    )(merged, x, kv, kv, w_mix, nw, w_q, w_o)


def _mlp_body(h_ref, nw_ref, w1_ref, w2_ref, fw_ref, o_ref, xn_ref):
    f = pl.program_id(1)
    blocks = _row_blocks(h_ref.shape[0])

    @pl.when(f == 0)
    def _():
        for rows in blocks:
            hh = h_ref[rows, :]
            xn_ref[rows, :] = _rms(hh, nw_ref[...]).astype(BF16)
            o_ref[rows, :] = hh

    for rows in blocks:
        u = _dot(xn_ref[rows, :], w1_ref[...])
        u = jnp.square(jnp.maximum(u, 0.0)).astype(BF16)
        o_ref[rows, :] += _dot(u, w2_ref[...])

    @pl.when(f == pl.num_programs(1) - 1)
    def _():
        for rows in blocks:
            o_ref[rows, :] = _rms(o_ref[rows, :], fw_ref[...])


def _mlp(h, nw, w1, w2, fw, tm, tf):
    m, d = h.shape
    dff = w1.shape[1]
    return pl.pallas_call(
        _mlp_body,
        grid=(m // tm, dff // tf),
        in_specs=[
            pl.BlockSpec((tm, d), lambda i, f: (i, 0)),
            pl.BlockSpec((1, d), lambda i, f: (0, 0)),
            pl.BlockSpec((d, tf), lambda i, f: (0, f)),
            pl.BlockSpec((tf, d), lambda i, f: (f, 0)),
            pl.BlockSpec((1, d), lambda i, f: (0, 0)),
        ],
        out_specs=pl.BlockSpec((tm, d), lambda i, f: (i, 0)),
        out_shape=jax.ShapeDtypeStruct((m, d), F32),
        scratch_shapes=[pltpu.VMEM((tm, d), BF16)],
        compiler_params=_params("parallel", "arbitrary"),
        name="mlp",
    )(h, nw, w1, w2, fw)


def _rope_tables(seq, dk):
    half = dk // 2
    inv = np.exp(-math.log(ROPE_BASE) * np.arange(half, dtype=np.float64) / half)
    ang = np.arange(seq, dtype=np.float64)[:, None] * inv[None, :]
    cos, sin = np.cos(ang).astype(np.float32), np.sin(ang).astype(np.float32)
    return np.concatenate([cos, cos], axis=1), np.concatenate([-sin, sin], axis=1)


def kernel(x, mem, norm_mix_w, w_in, conv_w, conv_b, dt_bias, a_log, d_skip, ret_norm_w, ssd_norm_w, w_ret_out, w_ssd_out, w_mix_out, norm_xa_w, mem_norm_w, w_xq, w_xkv, w_xo, norm_ff_w, w_ff1, w_ff2, final_norm_w):
    bsz, seq, d = x.shape
    mem_len = mem.shape[1]
    depth = w_in.shape[0]
    n_tok = bsz * seq
    ret_qk = RET_HEADS * RET_DK
    ret_v = RET_HEADS * RET_DV
    inner = w_ssd_out.shape[1]
    conv_dim = conv_w.shape[2]
    n_heads = inner // SSD_HEAD_DIM
    ret_w = 2 * ret_qk + 2 * ret_v
    off_dt = ret_w + inner + conv_dim
    off_gate = off_dt + n_heads
    assert w_in.shape[2] == off_gate + 2 * d and depth == 1

    tm_main, tn_main = IN_PROJ_TILE
    assert conv_dim % tn_main == 0 and ret_w % tn_main == 0 and inner % tn_main == 0
    shift = inner // tn_main
    z_off, ret_off = 0, inner

    cos2, sin2 = _rope_tables(seq, RET_DK)
    row = lambda v: v.reshape(1, -1)
    pad_heads = lambda v: jnp.pad(v.reshape(1, -1), ((0, 0), (0, LANES - n_heads)))

    w_in_t = jnp.swapaxes(w_in, 1, 2)

    h = x.reshape(n_tok, d)
    for l in range(depth):
        wt_dt = jnp.pad(w_in_t[l, off_dt:off_gate], ((0, LANES - n_heads), (0, 0)))
        xn, dt_raw = _prenorm(h, row(norm_mix_w[l]), wt_dt, PRENORM_ROWS)
        proj, wt_gates, (w_ret_b, w_ssd_b, w_mix_b, w_xq_b, w_xo_b) = _in_proj_main(
            xn, w_in_t, l, 0, ret_w + inner, shift,
            (w_ret_out[l], w_ssd_out[l], w_mix_out[l], w_xq[l], w_xo[l]),
            off_gate, 2 * d, n_heads, tm_main, tn_main)
        p_xbc = _in_proj_conv(xn, w_in_t, l, ret_w + inner, conv_dim, conv_w[l], row(conv_b[l]),
                              seq, tm_main, tn_main)
        gates = _in_proj_gates(xn, wt_gates, *GATES_TILE)
        y_ret = _retention(proj, ret_off, cos2, sin2, row(ret_norm_w[l]), bsz, seq)
        y_ssd, w_ff1_b = _ssd(proj, z_off, p_xbc, inner, dt_raw, pad_heads(dt_bias[l]),
                              pad_heads(a_log[l]), row(jnp.repeat(d_skip[l], SSD_HEAD_DIM)),
                              row(ssd_norm_w[l]), w_ff1[l], bsz, seq)
        merged, w_ff2_b = _merge(y_ret, y_ssd, w_ret_b, w_ssd_b, gates, w_ff2[l], *MERGE_TILE)
        kv = _norm_matmul(mem.reshape(bsz * mem_len, d), row(mem_norm_w[l]), w_xkv[l],
                          BF16, bsz * mem_len, KV_COLS, "xa_kv")
        h = _mix_xattn(merged, h, kv, w_mix_b, row(norm_xa_w[l]), w_xq_b, w_xo_b,
                       bsz, seq, mem_len, XATTN_ROWS)
        h = _mlp(h, row(norm_ff_w[l]), w_ff1_b, w_ff2_b, row(final_norm_w), *MLP_TILE)
    return h.reshape(bsz, seq, d)
```

```python
import functools
import itertools
import math

import numpy as np
import jax
import jax.numpy as jnp
from jax import lax
from jax.experimental import pallas as pl
from jax.experimental.pallas import tpu as pltpu

F32 = jnp.float32
BF16 = jnp.bfloat16
EPS = 1e-6

RET_HEADS = 8
RET_DK = 128
RET_DV = 256
CHUNK = 128
ROPE_BASE = 10000.0
SSD_HEAD_DIM = 64
SSD_GROUPS = 8
SSD_STATE = 128
SSD_CONV = 4
CONV_HIST = 8
XA_HEADS = 4

V7X_VMEM_BYTES = 64 * 1024 * 1024
VMEM_LIMIT = V7X_VMEM_BYTES - 4 * 1024 * 1024
LANES = 128
ROW_SPLIT = 512
CONV_ROW_SPLIT = 256
LOG2_E = math.log2(math.e)

PRENORM_ROWS = 512
IN_PROJ_TILE = (2048, 1024)
CAST_SLABS = 32
GATES_TILE = (2048, 2048)
MERGE_TILE = (1024, 512)
KV_COLS = 1024
XATTN_ROWS = 512
MLP_TILE = (1024, 1024)
MIX_STEP_CHUNKS = 2

_LOG_G = np.log(1.0 - np.exp2(-5.0 - np.arange(RET_HEADS, dtype=np.float64)))
_RET_CHUNK_DECAY = np.exp(CHUNK * _LOG_G)


def _params(*sem):
    return pltpu.CompilerParams(dimension_semantics=sem, vmem_limit_bytes=VMEM_LIMIT)


def _rms(x, w):
    ms = jnp.mean(x * x, axis=-1, keepdims=True)
    return x * lax.rsqrt(ms + EPS) * w


def _sigmoid(x):
    return 0.5 + 0.5 * jnp.tanh(0.5 * x)


def _silu(x):
    h = 0.5 * x
    return h + h * jnp.tanh(h)


def _dot(a, b):
    return jnp.dot(a, b, preferred_element_type=F32)


def _dot_nt(a, b):
    return lax.dot_general(a, b, (((1,), (1,)), ((), ())), preferred_element_type=F32)


def _cast_slab_specs(weights, n_steps, step_of, n_slabs=None):
    n_slabs = min(CAST_SLABS, n_steps) if n_slabs is None else n_slabs
    assert n_slabs <= n_steps and all(w.shape[0] % n_slabs == 0 for w in weights)
    slab = lambda *idx: (jnp.minimum(step_of(*idx), n_slabs - 1), 0)
    return [pl.BlockSpec((w.shape[0] // n_slabs, w.shape[1]), slab) for w in weights]


def _cast_slabs(src_refs, dst_refs):
    for src_ref, dst_ref in zip(src_refs, dst_refs):
        dst_ref[...] = src_ref[...].astype(dst_ref.dtype)


def _row_blocks(n_rows, step=ROW_SPLIT):
    step = min(step, n_rows)
    return [slice(r, r + step) for r in range(0, n_rows, step)]


def _prenorm_body(x_ref, nw_ref, wdt_ref, o_ref, dt_ref):
    xn = _rms(x_ref[...], nw_ref[...]).astype(o_ref.dtype)
    o_ref[...] = xn
    dt_ref[...] = _dot_nt(xn, wdt_ref[...].astype(BF16))


def _prenorm(x, nw, wt_dt, tm):
    m, k = x.shape
    n_dt = wt_dt.shape[0]
    return pl.pallas_call(
        _prenorm_body,
        grid=(m // tm,),
        in_specs=[pl.BlockSpec((tm, k), lambda i: (i, 0)), pl.BlockSpec((1, k), lambda i: (0, 0)),
                  pl.BlockSpec((n_dt, k), lambda i: (0, 0))],
        out_specs=[pl.BlockSpec((tm, k), lambda i: (i, 0)), pl.BlockSpec((tm, n_dt), lambda i: (i, 0))],
        out_shape=[jax.ShapeDtypeStruct((m, k), BF16), jax.ShapeDtypeStruct((m, n_dt), F32)],
        compiler_params=_params("parallel"),
        name="prenorm",
    )(x, nw, wt_dt)


def _matmul_nt_body(a_ref, wt_ref, g0_ref, g1_ref, *refs):
    n_cast = (len(refs) - 2) // 2
    o_ref, gate_w_ref = refs[n_cast], refs[n_cast + 1]
    wb = wt_ref[...].astype(BF16)
    for rows in _row_blocks(a_ref.shape[0]):
        o_ref[rows, :] = _dot_nt(a_ref[rows, :], wb).astype(o_ref.dtype)
    _cast_slabs(refs[:n_cast], refs[n_cast + 2:])
    slab = g0_ref.shape[0]
    gate_w_ref[0:slab, :] = g0_ref[...].astype(gate_w_ref.dtype)
    gate_w_ref[slab:2 * slab, :] = g1_ref[...].astype(gate_w_ref.dtype)


def _in_proj_main(a, w3, layer, row_off, n_cols, shift, to_bf16, gate_row0, n_gate_rows,
                  slab, tm, tn):
    m, k = a.shape
    nb = n_cols // tn
    off = row_off // tn
    n_steps = (m // tm) * nb
    assert n_cols % tn == 0 and m % tm == 0 and row_off % tn == 0
    cast_specs = _cast_slab_specs(to_bf16, n_steps, lambda i, j: i * nb + j)
    gate_blocks = n_gate_rows // (2 * slab)
    gate0 = gate_row0 // slab
    assert gate_row0 % slab == 0 and n_gate_rows % (2 * slab) == 0 and gate_blocks <= n_steps
    gate_blk = lambda i, j: jnp.minimum(i * nb + j, gate_blocks - 1)
    gate_src = lambda t: pl.BlockSpec(
        (None, slab, k), lambda i, j: (layer, gate0 + 2 * gate_blk(i, j) + t, 0))

    outs = pl.pallas_call(
        _matmul_nt_body,
        grid=(m // tm, nb),
        in_specs=[
            pl.BlockSpec((tm, k), lambda i, j: (i, 0)),
            pl.BlockSpec((None, tn, k), lambda i, j: (layer, j + off, 0)),
            gate_src(0), gate_src(1),
        ] + cast_specs,
        out_specs=[pl.BlockSpec((tm, tn), lambda i, j: (i, (j + shift) % nb)),
                   pl.BlockSpec((2 * slab, k), lambda i, j: (gate_blk(i, j), 0))] + cast_specs,
        out_shape=[jax.ShapeDtypeStruct((m, n_cols), BF16),
                   jax.ShapeDtypeStruct((n_gate_rows, k), BF16)]
        + [jax.ShapeDtypeStruct(w.shape, BF16) for w in to_bf16],
        compiler_params=_params("arbitrary", "arbitrary"),
        name="in_proj_main",
    )(a, w3, w3, w3, *to_bf16)
    return outs[0], outs[1], outs[2:]


def _in_proj_conv_body(a_ref, wt_ref, cw_ref, cb_ref, o_ref, carry_ref, *, tiles_per_seq):
    i = pl.program_id(0)
    j = pl.program_id(1)
    tn = wt_ref.shape[0]

    @pl.when(i % tiles_per_seq == 0)
    def _():
        carry_ref[j] = jnp.zeros((CONV_HIST, tn), F32)

    wb = wt_ref[...].astype(BF16)
    cwh = 0.5 * cw_ref[...]
    cbh = 0.5 * cb_ref[...]
    hist = carry_ref[j]
    for rows in _row_blocks(a_ref.shape[0], CONV_ROW_SPLIT):
        res = _dot_nt(a_ref[rows, :], wb)
        nt = res.shape[0] // CONV_HIST
        tiles = jnp.concatenate([hist, res], axis=0).reshape(nt + 1, CONV_HIST, tn)
        sub = lax.broadcasted_iota(jnp.int32, (nt, CONV_HIST, tn), 1)
        acc = cbh + cwh[SSD_CONV - 1:SSD_CONV, :] * res
        for k in range(SSD_CONV - 1):
            dist = SSD_CONV - 1 - k
            mixed = jnp.where(sub >= CONV_HIST - dist, tiles[:-1], tiles[1:])
            shifted = pltpu.roll(mixed, dist, 1).reshape(nt * CONV_HIST, tn)
            acc = acc + cwh[k:k + 1, :] * shifted
        o_ref[rows, :] = (acc + acc * jnp.tanh(acc)).astype(o_ref.dtype)
        hist = res[res.shape[0] - CONV_HIST:, :]
    carry_ref[j] = hist


def _in_proj_conv(a, w3, layer, row_off, n_cols, conv_w, conv_b, seq, tm, tn):
    m, k = a.shape
    nb = n_cols // tn
    off = row_off // tn
    assert n_cols % tn == 0 and m % tm == 0 and row_off % tn == 0 and seq % tm == 0
    assert SSD_CONV - 1 <= CONV_HIST
    body = functools.partial(_in_proj_conv_body, tiles_per_seq=seq // tm)
    return pl.pallas_call(
        body,
        grid=(m // tm, nb),
        in_specs=[
            pl.BlockSpec((tm, k), lambda i, j: (i, 0)),
            pl.BlockSpec((None, tn, k), lambda i, j: (layer, j + off, 0)),
            pl.BlockSpec((SSD_CONV, tn), lambda i, j: (0, j)),
            pl.BlockSpec((1, tn), lambda i, j: (0, j)),
        ],
        out_specs=pl.BlockSpec((tm, tn), lambda i, j: (i, j)),
        out_shape=jax.ShapeDtypeStruct((m, n_cols), BF16),
        scratch_shapes=[pltpu.VMEM((nb, CONV_HIST, tn), F32)],
        compiler_params=_params("arbitrary", "arbitrary"),
        name="in_proj_conv",
    )(a, w3, conv_w, conv_b)


def _in_proj_gates_body(a_ref, wt_ref, g_ref):
    for rows in _row_blocks(a_ref.shape[0]):
        g_ref[rows, :] = _dot_nt(a_ref[rows, :], wt_ref[...]).astype(g_ref.dtype)


def _in_proj_gates(a, wt, tm, tn):
    m, k = a.shape
    n = wt.shape[0]
    assert n % tn == 0 and m % tm == 0
    return pl.pallas_call(
        _in_proj_gates_body,
        grid=(m // tm, n // tn),
        in_specs=[
            pl.BlockSpec((tm, k), lambda i, j: (i, 0)),
            pl.BlockSpec((tn, k), lambda i, j: (j, 0)),
        ],
        out_specs=pl.BlockSpec((tm, tn), lambda i, j: (i, j)),
        out_shape=jax.ShapeDtypeStruct((m, n), BF16),
        compiler_params=_params("parallel", "arbitrary"),
        name="in_proj_gates",
    )(a, wt)


def _norm_matmul_body(x_ref, nw_ref, w_ref, o_ref, xn_ref):
    @pl.when(pl.program_id(1) == 0)
    def _():
        xn_ref[...] = _rms(x_ref[...], nw_ref[...]).astype(BF16)

    o_ref[...] = _dot(xn_ref[...], w_ref[...].astype(BF16)).astype(o_ref.dtype)


def _norm_matmul(x, nw, w, out_dtype, tm, tn, name):
    m, k = x.shape
    n = w.shape[1]
    return pl.pallas_call(
        _norm_matmul_body,
        grid=(m // tm, n // tn),
        in_specs=[
            pl.BlockSpec((tm, k), lambda i, j: (i, 0)),
            pl.BlockSpec((1, k), lambda i, j: (0, 0)),
            pl.BlockSpec((k, tn), lambda i, j: (0, j)),
        ],
        out_specs=pl.BlockSpec((tm, tn), lambda i, j: (i, j)),
        out_shape=jax.ShapeDtypeStruct((m, n), out_dtype),
        scratch_shapes=[pltpu.VMEM((tm, k), BF16)],
        compiler_params=_params("parallel", "arbitrary"),
        name=name,
    )(x, nw, w)


def _retention_tables():
    idx = np.arange(CHUNK, dtype=np.float64)
    rel = idx[:, None] - idx[None, :]
    lg = _LOG_G[:, None, None]
    scale = RET_DK ** -0.5
    decay = np.where(rel[None] >= 0, np.exp(np.maximum(rel, 0.0)[None] * lg), 0.0) * scale
    xi = np.exp((idx + 1.0)[None, :, None] * lg) * scale
    zeta = np.exp((CHUNK - 1.0 - idx)[None, :, None] * lg)
    wide = (RET_HEADS, CHUNK, RET_DK)
    return (decay.astype(np.float32), np.broadcast_to(xi, wide).astype(np.float32),
            np.broadcast_to(zeta, wide).astype(np.float32))


def _retention_body(q_ref, k_ref, v_ref, g_ref, cos_ref, sin_ref, dec_ref, xi_ref, zeta_ref,
                    nw_ref, o_ref, state_ref):
    for cc, h in itertools.product(range(q_ref.shape[0] // CHUNK), range(RET_HEADS)):
        rows = slice(cc * CHUNK, (cc + 1) * CHUNK)
        cos = cos_ref[rows, :]
        sin = sin_ref[rows, :]
        q = q_ref[rows, h * RET_DK:(h + 1) * RET_DK].astype(F32)
        k = k_ref[rows, h * RET_DK:(h + 1) * RET_DK].astype(F32)
        v = v_ref[rows, h * RET_DV:(h + 1) * RET_DV]
        qr = q * cos + pltpu.roll(q, RET_DK // 2, 1) * sin
        kr = k * cos + pltpu.roll(k, RET_DK // 2, 1) * sin
        s = _dot_nt(qr.astype(BF16), kr.astype(BF16)) * dec_ref[h]
        st = state_ref[h]
        lhs = jnp.concatenate([s.astype(BF16), (qr * xi_ref[h]).astype(BF16)], axis=1)
        rhs = jnp.concatenate([v, st.astype(BF16)], axis=0)
        y = _dot(lhs, rhs)
        kz_t = (kr * zeta_ref[h]).T.astype(BF16)
        state_ref[h] = float(_RET_CHUNK_DECAY[h]) * st + _dot(kz_t, v)
        mu = jnp.mean(y, axis=-1, keepdims=True)
        yc = y - mu
        var = jnp.mean(yc * yc, axis=-1, keepdims=True)
        yn = yc * lax.rsqrt(var + EPS)
        hsl = slice(h * RET_DV, (h + 1) * RET_DV)
        o_ref[rows, hsl] = (yn * nw_ref[:, hsl] * _silu(g_ref[rows, hsl].astype(F32))).astype(BF16)


def _ssd_body(z_ref, xbc_ref, dt_ref, dtb_ref, alog_ref, dsk_ref, nw_ref, wsrc_ref,
              o_ref, wdst_ref, prev_ref, *, inner, heads_per_group):
    _cast_slabs([wsrc_ref], [wdst_ref])

    c = CHUNK
    n_sub = z_ref.shape[0] // c
    gw = heads_per_group * SSD_HEAD_DIM
    row = lax.broadcasted_iota(jnp.int32, (c, c), 0)
    col = lax.broadcasted_iota(jnp.int32, (c, c), 1)
    tri = row >= col
    lane_lo = lax.broadcasted_iota(jnp.int32, (c, LANES), 1) < SSD_HEAD_DIM
    neg_inf = jnp.float32(-jnp.inf)
    a = -jnp.exp(alog_ref[...])

    for cc in range(n_sub):
        rows = slice(cc * c, (cc + 1) * c)
        dtr = dt_ref[rows, :] + dtb_ref[...]
        dt = jnp.maximum(dtr, 0.0) + jnp.log1p(jnp.exp(-jnp.abs(dtr)))
        da = dt * a
        a_cs = jnp.dot(tri.astype(F32), da, precision=lax.Precision.HIGHEST,
                       preferred_element_type=F32)
        a_cs = a_cs * LOG2_E
        a_last = a_cs[c - 1:c, :]
        w1 = jnp.exp2(a_last - a_cs) * dt
        src_t = (a_cs - jnp.log2(dt)).T
        w1_t = w1.T

        for g in range(SSD_GROUPS):
            b_gb = xbc_ref[rows, inner + g * SSD_STATE: inner + (g + 1) * SSD_STATE]
            c_gb = xbc_ref[rows, inner + (SSD_GROUPS + g) * SSD_STATE:
                           inner + (SSD_GROUPS + g + 1) * SSD_STATE]
            cb = _dot_nt(c_gb, b_gb).astype(BF16)
            b_gt = b_gb.astype(F32).T.astype(BF16)
            y_pairs = []
            for pr in range(heads_per_group // 2):
                h0 = g * heads_per_group + 2 * pr
                psl = slice(h0 * SSD_HEAD_DIM, (h0 + 2) * SSD_HEAD_DIM)
                xs_b = xbc_ref[rows, psl]
                prev = prev_ref[:, psl]
                rhs = jnp.concatenate([xs_b, prev.astype(BF16)], axis=0)
                lhs, bws, cds = [], [], []
                for h in (h0, h0 + 1):
                    colb = jnp.broadcast_to(a_cs[:, h:h + 1], (c, c))
                    rowb = jnp.broadcast_to(src_t[h:h + 1, :], (c, c))
                    m = cb * jnp.exp2(jnp.where(tri, colb - rowb, neg_inf)).astype(BF16)
                    e = jnp.exp2(colb)
                    lhs.append(jnp.concatenate([m, e.astype(BF16) * c_gb], axis=1))
                    bws.append(b_gt * jnp.broadcast_to(w1_t[h:h + 1, :], (c, c)).astype(BF16))
                    cds.append(e[c - 1:c, :])
                ys = _dot(jnp.concatenate(lhs, axis=0), rhs)
                sts = _dot(jnp.concatenate(bws, axis=0), xs_b)
                y_pairs.append(jnp.where(lane_lo, ys[0:c, :], ys[c:2 * c, :]))
                cd = jnp.where(lane_lo[0:1, :], cds[0], cds[1])
                prev_ref[:, psl] = cd * prev + jnp.where(lane_lo, sts[0:c, :], sts[c:2 * c, :])
            gsl = slice(g * gw, (g + 1) * gw)
            y = jnp.concatenate(y_pairs, axis=1) + dsk_ref[:, gsl] * xbc_ref[rows, gsl].astype(F32)
            gy = y * _silu(z_ref[rows, gsl].astype(F32))
            ms = jnp.mean(gy * gy, axis=-1, keepdims=True)
            o_ref[rows, gsl] = (gy * lax.rsqrt(ms + EPS) * nw_ref[:, gsl]).astype(BF16)


def _mixers_body(*refs, n_ret_in, n_ssd_in, inner, heads_per_group):
    ret_in = refs[:n_ret_in]
    ssd_in = refs[n_ret_in:n_ret_in + n_ssd_in]
    y_ret_ref, y_ssd_ref, wdst_ref, state_ref, prev_ref = refs[n_ret_in + n_ssd_in:]

    @pl.when(pl.program_id(1) == 0)
    def _():
        state_ref[...] = jnp.zeros_like(state_ref)
        prev_ref[...] = jnp.zeros_like(prev_ref)

    _retention_body(*ret_in, y_ret_ref, state_ref)
    _ssd_body(*ssd_in, y_ssd_ref, wdst_ref, prev_ref, inner=inner, heads_per_group=heads_per_group)


def _mixers(proj, ret_off, z_off, p_xbc, inner, cos2, sin2, ret_nw, dt_raw, dtb, alog, dskip,
            ssd_nw, w_to_bf16, bsz, seq):
    rows = MIX_STEP_CHUNKS * CHUNK
    nc = seq // rows
    qk = RET_HEADS * RET_DK
    vd = RET_HEADS * RET_DV
    conv_dim = p_xbc.shape[1]
    n_heads = inner // SSD_HEAD_DIM
    heads_per_group = n_heads // SSD_GROUPS
    assert seq % rows == 0 and vd == 2 * qk and ret_off % vd == 0 and z_off % inner == 0
    assert conv_dim == inner + 2 * SSD_GROUPS * SSD_STATE and SSD_STATE == CHUNK
    assert n_heads <= LANES and heads_per_group % 2 == 0 and 2 * SSD_HEAD_DIM == LANES
    oq, ov, zb = ret_off // qk, ret_off // vd, z_off // inner
    decay, xi, zeta = _retention_tables()
    tok = lambda blk: (lambda b, c: (b * nc + c, blk))
    fixed = lambda b, c: (0, 0)
    table = lambda n: pl.BlockSpec((RET_HEADS, CHUNK, n), lambda b, c: (0, 0, 0))
    cast_spec = _cast_slab_specs([w_to_bf16], bsz * nc, lambda b, c: b * nc + c)
    ret_specs = [
        pl.BlockSpec((rows, qk), tok(oq)),
        pl.BlockSpec((rows, qk), tok(oq + 1)),
        pl.BlockSpec((rows, vd), tok(ov + 1)),
        pl.BlockSpec((rows, vd), tok(ov + 2)),
        pl.BlockSpec((rows, RET_DK), lambda b, c: (c, 0)),
        pl.BlockSpec((rows, RET_DK), lambda b, c: (c, 0)),
        table(CHUNK), table(RET_DK), table(RET_DK),
        pl.BlockSpec((1, vd), fixed),
    ]
    ssd_specs = [
        pl.BlockSpec((rows, inner), tok(zb)),
        pl.BlockSpec((rows, conv_dim), tok(0)),
        pl.BlockSpec((rows, LANES), tok(0)),
        pl.BlockSpec((1, LANES), fixed),
        pl.BlockSpec((1, LANES), fixed),
        pl.BlockSpec((1, inner), fixed),
        pl.BlockSpec((1, inner), fixed),
    ] + cast_spec
    body = functools.partial(_mixers_body, n_ret_in=len(ret_specs), n_ssd_in=len(ssd_specs),
                             inner=inner, heads_per_group=heads_per_group)
    return pl.pallas_call(
        body,
        grid=(bsz, nc),
        in_specs=ret_specs + ssd_specs,
        out_specs=[pl.BlockSpec((rows, vd), tok(0)), pl.BlockSpec((rows, inner), tok(0))] + cast_spec,
        out_shape=[jax.ShapeDtypeStruct((bsz * seq, vd), BF16),
                   jax.ShapeDtypeStruct((bsz * seq, inner), BF16),
                   jax.ShapeDtypeStruct(w_to_bf16.shape, BF16)],
        scratch_shapes=[pltpu.VMEM((RET_HEADS, RET_DK, RET_DV), F32),
                        pltpu.VMEM((SSD_STATE, inner), F32)],
        compiler_params=_params("arbitrary", "arbitrary"),
        name="mixers",
    )(proj, proj, proj, proj, cos2, sin2, decay, xi, zeta, ret_nw,
      proj, p_xbc, dt_raw, dtb, alog, dskip, ssd_nw, w_to_bf16)


def _merge_body(yr_ref, ys_ref, wr_ref, ws_ref, ga_ref, gb_ref, wsrc_ref, o_ref, wdst_ref):
    _cast_slabs([wsrc_ref], [wdst_ref])
    for rows in _row_blocks(yr_ref.shape[0]):
        br = _dot(yr_ref[rows, :], wr_ref[...])
        bs = _dot(ys_ref[rows, :], ws_ref[...])
        ga = _sigmoid(ga_ref[rows, :].astype(F32))
        gb = _sigmoid(gb_ref[rows, :].astype(F32))
        o_ref[rows, :] = (ga * br + gb * bs).astype(o_ref.dtype)


def _merge(y_ret, y_ssd, w_ret, w_ssd, gates, w_to_bf16, tm, tn):
    m, kr = y_ret.shape
    ks = y_ssd.shape[1]
    n = w_ret.shape[1]
    nb = n // tn
    cast_spec = _cast_slab_specs([w_to_bf16], (m // tm) * nb, lambda i, j: i * nb + j)
    return pl.pallas_call(
        _merge_body,
        grid=(m // tm, nb),
        in_specs=[
            pl.BlockSpec((tm, kr), lambda i, j: (i, 0)),
            pl.BlockSpec((tm, ks), lambda i, j: (i, 0)),
            pl.BlockSpec((kr, tn), lambda i, j: (0, j)),
            pl.BlockSpec((ks, tn), lambda i, j: (0, j)),
            pl.BlockSpec((tm, tn), lambda i, j: (i, j)),
            pl.BlockSpec((tm, tn), lambda i, j: (i, j + nb)),
        ] + cast_spec,
        out_specs=[pl.BlockSpec((tm, tn), lambda i, j: (i, j))] + cast_spec,
        out_shape=[jax.ShapeDtypeStruct((m, n), BF16), jax.ShapeDtypeStruct(w_to_bf16.shape, BF16)],
        compiler_params=_params("arbitrary", "arbitrary"),
        name="merge",
    )(y_ret, y_ssd, w_ret, w_ssd, gates, gates, w_to_bf16)


def _mix_xattn_body(m_ref, x_ref, k_ref, v_ref, wmix_ref, nw_ref, wq_ref, wo_ref, o_ref, xn_ref):
    d = x_ref.shape[1]
    hd = d // XA_HEADS
    scale = hd ** -0.5
    o_ref[...] = x_ref[...] + _dot(m_ref[...], wmix_ref[...])
    xn_ref[...] = _rms(o_ref[...], nw_ref[...]).astype(BF16)
    for h in range(XA_HEADS):
        sl = slice(h * hd, (h + 1) * hd)
        q = _dot(xn_ref[...], wq_ref[:, sl]).astype(BF16)
        s = _dot_nt(q, k_ref[:, sl]) * scale
        s = s - jnp.max(s, axis=-1, keepdims=True)
        p = jnp.exp(s)
        p = p / jnp.sum(p, axis=-1, keepdims=True)
        o_h = _dot(p.astype(BF16), v_ref[:, sl]).astype(BF16)
        o_ref[...] += _dot(o_h, wo_ref[sl, :])


def _mix_xattn(merged, x, kv, w_mix, nw, w_q, w_o, bsz, seq, mem_len, tq):
    d = x.shape[1]
    nq = seq // tq
    tile = lambda b, i: (b * nq + i, 0)
    fixed = lambda b, i: (0, 0)
    resident = pl.Buffered(1)
    return pl.pallas_call(
        _mix_xattn_body,
        grid=(bsz, nq),
        in_specs=[
            pl.BlockSpec((tq, d), tile),
            pl.BlockSpec((tq, d), tile),
            pl.BlockSpec((mem_len, d), lambda b, i: (b, 0)),
            pl.BlockSpec((mem_len, d), lambda b, i: (b, 1)),
            pl.BlockSpec((d, d), fixed, pipeline_mode=resident),
            pl.BlockSpec((1, d), fixed),
            pl.BlockSpec((d, d), fixed, pipeline_mode=resident),
            pl.BlockSpec((d, d), fixed, pipeline_mode=resident),
        ],
        out_specs=pl.BlockSpec((tq, d), tile),
        out_shape=jax.ShapeDtypeStruct((bsz * seq, d), F32),
        scratch_shapes=[pltpu.VMEM((tq, d), BF16)],
        compiler_params=_params("parallel", "arbitrary"),
        name="mix_xattn",
    )(merged, x, kv, kv, w_mix, nw, w_q, w_o)


def _mlp_body(h_ref, nw_ref, w1_ref, w2_ref, fw_ref, o_ref, xn_ref):
    f = pl.program_id(1)
    blocks = _row_blocks(h_ref.shape[0])

    @pl.when(f == 0)
    def _():
        for rows in blocks:
            hh = h_ref[rows, :]
            xn_ref[rows, :] = _rms(hh, nw_ref[...]).astype(BF16)
            o_ref[rows, :] = hh

    for rows in blocks:
        u = _dot(xn_ref[rows, :], w1_ref[...])
        u = jnp.square(jnp.maximum(u, 0.0)).astype(BF16)
        o_ref[rows, :] += _dot(u, w2_ref[...])

    @pl.when(f == pl.num_programs(1) - 1)
    def _():
        for rows in blocks:
            o_ref[rows, :] = _rms(o_ref[rows, :], fw_ref[...])


def _mlp(h, nw, w1, w2, fw, tm, tf):
    m, d = h.shape
    dff = w1.shape[1]
    return pl.pallas_call(
        _mlp_body,
        grid=(m // tm, dff // tf),
        in_specs=[
            pl.BlockSpec((tm, d), lambda i, f: (i, 0)),
            pl.BlockSpec((1, d), lambda i, f: (0, 0)),
            pl.BlockSpec((d, tf), lambda i, f: (0, f)),
            pl.BlockSpec((tf, d), lambda i, f: (f, 0)),
            pl.BlockSpec((1, d), lambda i, f: (0, 0)),
        ],
        out_specs=pl.BlockSpec((tm, d), lambda i, f: (i, 0)),
        out_shape=jax.ShapeDtypeStruct((m, d), F32),
        scratch_shapes=[pltpu.VMEM((tm, d), BF16)],
        compiler_params=_params("parallel", "arbitrary"),
        name="mlp",
    )(h, nw, w1, w2, fw)


def _rope_tables(seq, dk):
    half = dk // 2
    inv = np.exp(-math.log(ROPE_BASE) * np.arange(half, dtype=np.float64) / half)
    ang = np.arange(seq, dtype=np.float64)[:, None] * inv[None, :]
    cos, sin = np.cos(ang).astype(np.float32), np.sin(ang).astype(np.float32)
    return np.concatenate([cos, cos], axis=1), np.concatenate([-sin, sin], axis=1)


def kernel(x, mem, norm_mix_w, w_in, conv_w, conv_b, dt_bias, a_log, d_skip, ret_norm_w, ssd_norm_w, w_ret_out, w_ssd_out, w_mix_out, norm_xa_w, mem_norm_w, w_xq, w_xkv, w_xo, norm_ff_w, w_ff1, w_ff2, final_norm_w):
    bsz, seq, d = x.shape
    mem_len = mem.shape[1]
    depth = w_in.shape[0]
    n_tok = bsz * seq
    ret_qk = RET_HEADS * RET_DK
    ret_v = RET_HEADS * RET_DV
    inner = w_ssd_out.shape[1]
    conv_dim = conv_w.shape[2]
    n_heads = inner // SSD_HEAD_DIM
    ret_w = 2 * ret_qk + 2 * ret_v
    off_dt = ret_w + inner + conv_dim
    off_gate = off_dt + n_heads
    assert w_in.shape[2] == off_gate + 2 * d and depth == 1

    tm_main, tn_main = IN_PROJ_TILE
    assert conv_dim % tn_main == 0 and ret_w % tn_main == 0 and inner % tn_main == 0
    shift = inner // tn_main
    z_off, ret_off = 0, inner

    cos2, sin2 = _rope_tables(seq, RET_DK)
    row = lambda v: v.reshape(1, -1)
    pad_heads = lambda v: jnp.pad(v.reshape(1, -1), ((0, 0), (0, LANES - n_heads)))

    w_in_t = jnp.swapaxes(w_in, 1, 2)

    h = x.reshape(n_tok, d)
    for l in range(depth):
        wt_dt = jnp.pad(w_in_t[l, off_dt:off_gate], ((0, LANES - n_heads), (0, 0)))
        xn, dt_raw = _prenorm(h, row(norm_mix_w[l]), wt_dt, PRENORM_ROWS)
        proj, wt_gates, (w_ret_b, w_ssd_b, w_mix_b, w_xq_b, w_xo_b) = _in_proj_main(
            xn, w_in_t, l, 0, ret_w + inner, shift,
            (w_ret_out[l], w_ssd_out[l], w_mix_out[l], w_xq[l], w_xo[l]),
            off_gate, 2 * d, n_heads, tm_main, tn_main)
        p_xbc = _in_proj_conv(xn, w_in_t, l, ret_w + inner, conv_dim, conv_w[l], row(conv_b[l]),
                              seq, tm_main, tn_main)
        gates = _in_proj_gates(xn, wt_gates, *GATES_TILE)
        y_ret, y_ssd, w_ff1_b = _mixers(
            proj, ret_off, z_off, p_xbc, inner, cos2, sin2, row(ret_norm_w[l]), dt_raw,
            pad_heads(dt_bias[l]), pad_heads(a_log[l]), row(jnp.repeat(d_skip[l], SSD_HEAD_DIM)),
            row(ssd_norm_w[l]), w_ff1[l], bsz, seq)
        merged, w_ff2_b = _merge(y_ret, y_ssd, w_ret_b, w_ssd_b, gates, w_ff2[l], *MERGE_TILE)
        kv = _norm_matmul(mem.reshape(bsz * mem_len, d), row(mem_norm_w[l]), w_xkv[l],
                          BF16, bsz * mem_len, KV_COLS, "xa_kv")
        h = _mix_xattn(merged, h, kv, w_mix_b, row(norm_xa_w[l]), w_xq_b, w_xo_b,
                       bsz, seq, mem_len, XATTN_ROWS)
        h = _mlp(h, row(norm_ff_w[l]), w_ff1_b, w_ff2_b, row(final_norm_w), *MLP_TILE)
    return h.reshape(bsz, seq, d)
```

```python
import functools
import itertools
import math

import numpy as np
import jax
import jax.numpy as jnp
from jax import lax
from jax.experimental import pallas as pl
from jax.experimental.pallas import tpu as pltpu

F32 = jnp.float32
BF16 = jnp.bfloat16
EPS = 1e-6

RET_HEADS = 8
RET_DK = 128
RET_DV = 256
CHUNK = 128
ROPE_BASE = 10000.0
SSD_HEAD_DIM = 64
SSD_GROUPS = 8
SSD_STATE = 128
SSD_CONV = 4
CONV_HIST = 8
XA_HEADS = 4

V7X_VMEM_BYTES = 64 * 1024 * 1024
VMEM_LIMIT = V7X_VMEM_BYTES - 4 * 1024 * 1024
LANES = 128
ROW_SPLIT = 512
CONV_ROW_SPLIT = 256
LOG2_E = math.log2(math.e)

PRENORM_ROWS = 512
IN_PROJ_TILE = (2048, 1024)
CAST_SLABS = 32
GATES_TILE = (2048, 2048)
MERGE_TILE = (1024, 512)
KV_COLS = 1024
XATTN_ROWS = 512
MLP_TILE = (1024, 1024)
MIX_STEP_CHUNKS = 2

_LOG_G = np.log(1.0 - np.exp2(-5.0 - np.arange(RET_HEADS, dtype=np.float64)))
_RET_CHUNK_DECAY = np.exp(CHUNK * _LOG_G)


def _params(*sem):
    return pltpu.CompilerParams(dimension_semantics=sem, vmem_limit_bytes=VMEM_LIMIT)


def _rms(x, w):
    ms = jnp.mean(x * x, axis=-1, keepdims=True)
    return x * lax.rsqrt(ms + EPS) * w


def _sigmoid(x):
    return 0.5 + 0.5 * jnp.tanh(0.5 * x)


def _silu(x):
    h = 0.5 * x
    return h + h * jnp.tanh(h)


def _dot(a, b):
    return jnp.dot(a, b, preferred_element_type=F32)


def _dot_nt(a, b):
    return lax.dot_general(a, b, (((1,), (1,)), ((), ())), preferred_element_type=F32)


def _cast_slab_specs(weights, n_steps, step_of, n_slabs=None):
    n_slabs = min(CAST_SLABS, n_steps) if n_slabs is None else n_slabs
    assert n_slabs <= n_steps and all(w.shape[0] % n_slabs == 0 for w in weights)
    slab = lambda *idx: (jnp.minimum(step_of(*idx), n_slabs - 1), 0)
    return [pl.BlockSpec((w.shape[0] // n_slabs, w.shape[1]), slab) for w in weights]


def _cast_slabs(src_refs, dst_refs):
    for src_ref, dst_ref in zip(src_refs, dst_refs):
        dst_ref[...] = src_ref[...].astype(dst_ref.dtype)


def _row_blocks(n_rows, step=ROW_SPLIT):
    step = min(step, n_rows)
    return [slice(r, r + step) for r in range(0, n_rows, step)]


def _prenorm_body(x_ref, nw_ref, wdt_ref, o_ref, dt_ref):
    xn = _rms(x_ref[...], nw_ref[...]).astype(o_ref.dtype)
    o_ref[...] = xn
    dt_ref[...] = _dot_nt(xn, wdt_ref[...].astype(BF16))


def _prenorm(x, nw, wt_dt, tm):
    m, k = x.shape
    n_dt = wt_dt.shape[0]
    return pl.pallas_call(
        _prenorm_body,
        grid=(m // tm,),
        in_specs=[pl.BlockSpec((tm, k), lambda i: (i, 0)), pl.BlockSpec((1, k), lambda i: (0, 0)),
                  pl.BlockSpec((n_dt, k), lambda i: (0, 0))],
        out_specs=[pl.BlockSpec((tm, k), lambda i: (i, 0)), pl.BlockSpec((tm, n_dt), lambda i: (i, 0))],
        out_shape=[jax.ShapeDtypeStruct((m, k), BF16), jax.ShapeDtypeStruct((m, n_dt), F32)],
        compiler_params=_params("parallel"),
        name="prenorm",
    )(x, nw, wt_dt)


def _matmul_nt_body(a_ref, wt_ref, g0_ref, g1_ref, *refs):
    n_cast = (len(refs) - 2) // 2
    o_ref, gate_w_ref = refs[n_cast], refs[n_cast + 1]
    wb = wt_ref[...].astype(BF16)
    for rows in _row_blocks(a_ref.shape[0]):
        o_ref[rows, :] = _dot_nt(a_ref[rows, :], wb).astype(o_ref.dtype)
    _cast_slabs(refs[:n_cast], refs[n_cast + 2:])
    slab = g0_ref.shape[0]
    gate_w_ref[0:slab, :] = g0_ref[...].astype(gate_w_ref.dtype)
    gate_w_ref[slab:2 * slab, :] = g1_ref[...].astype(gate_w_ref.dtype)


def _in_proj_main(a, w3, layer, row_off, n_cols, shift, to_bf16, gate_row0, n_gate_rows,
                  slab, tm, tn):
    m, k = a.shape
    nb = n_cols // tn
    off = row_off // tn
    n_steps = (m // tm) * nb
    assert n_cols % tn == 0 and m % tm == 0 and row_off % tn == 0
    cast_specs = _cast_slab_specs(to_bf16, n_steps, lambda i, j: i * nb + j)
    gate_blocks = n_gate_rows // (2 * slab)
    gate0 = gate_row0 // slab
    assert gate_row0 % slab == 0 and n_gate_rows % (2 * slab) == 0 and gate_blocks <= n_steps
    gate_blk = lambda i, j: jnp.minimum(i * nb + j, gate_blocks - 1)
    gate_src = lambda t: pl.BlockSpec(
        (None, slab, k), lambda i, j: (layer, gate0 + 2 * gate_blk(i, j) + t, 0))

    outs = pl.pallas_call(
        _matmul_nt_body,
        grid=(m // tm, nb),
        in_specs=[
            pl.BlockSpec((tm, k), lambda i, j: (i, 0)),
            pl.BlockSpec((None, tn, k), lambda i, j: (layer, j + off, 0)),
            gate_src(0), gate_src(1),
        ] + cast_specs,
        out_specs=[pl.BlockSpec((tm, tn), lambda i, j: (i, (j + shift) % nb)),
                   pl.BlockSpec((2 * slab, k), lambda i, j: (gate_blk(i, j), 0))] + cast_specs,
        out_shape=[jax.ShapeDtypeStruct((m, n_cols), BF16),
                   jax.ShapeDtypeStruct((n_gate_rows, k), BF16)]
        + [jax.ShapeDtypeStruct(w.shape, BF16) for w in to_bf16],
        compiler_params=_params("arbitrary", "arbitrary"),
        name="in_proj_main",
    )(a, w3, w3, w3, *to_bf16)
    return outs[0], outs[1], outs[2:]


def _in_proj_conv_body(a_ref, wt_ref, cw_ref, cb_ref, o_ref, carry_ref, *, tiles_per_seq):
    i = pl.program_id(0)
    j = pl.program_id(1)
    tn = wt_ref.shape[0]

    @pl.when(i % tiles_per_seq == 0)
    def _():
        carry_ref[j] = jnp.zeros((CONV_HIST, tn), F32)

    wb = wt_ref[...].astype(BF16)
    cwh = 0.5 * cw_ref[...]
    cbh = 0.5 * cb_ref[...]
    hist = carry_ref[j]
    for rows in _row_blocks(a_ref.shape[0], CONV_ROW_SPLIT):
        res = _dot_nt(a_ref[rows, :], wb)
        nt = res.shape[0] // CONV_HIST
        tiles = jnp.concatenate([hist, res], axis=0).reshape(nt + 1, CONV_HIST, tn)
        sub = lax.broadcasted_iota(jnp.int32, (nt, CONV_HIST, tn), 1)
        acc = cbh + cwh[SSD_CONV - 1:SSD_CONV, :] * res
        for k in range(SSD_CONV - 1):
            dist = SSD_CONV - 1 - k
            mixed = jnp.where(sub >= CONV_HIST - dist, tiles[:-1], tiles[1:])
            shifted = pltpu.roll(mixed, dist, 1).reshape(nt * CONV_HIST, tn)
            acc = acc + cwh[k:k + 1, :] * shifted
        o_ref[rows, :] = (acc + acc * jnp.tanh(acc)).astype(o_ref.dtype)
        hist = res[res.shape[0] - CONV_HIST:, :]
    carry_ref[j] = hist


def _in_proj_conv(a, w3, layer, row_off, n_cols, conv_w, conv_b, seq, tm, tn):
    m, k = a.shape
    nb = n_cols // tn
    off = row_off // tn
    assert n_cols % tn == 0 and m % tm == 0 and row_off % tn == 0 and seq % tm == 0
    assert SSD_CONV - 1 <= CONV_HIST
    body = functools.partial(_in_proj_conv_body, tiles_per_seq=seq // tm)
    return pl.pallas_call(
        body,
        grid=(m // tm, nb),
        in_specs=[
            pl.BlockSpec((tm, k), lambda i, j: (i, 0)),
            pl.BlockSpec((None, tn, k), lambda i, j: (layer, j + off, 0)),
            pl.BlockSpec((SSD_CONV, tn), lambda i, j: (0, j)),
            pl.BlockSpec((1, tn), lambda i, j: (0, j)),
        ],
        out_specs=pl.BlockSpec((tm, tn), lambda i, j: (i, j)),
        out_shape=jax.ShapeDtypeStruct((m, n_cols), BF16),
        scratch_shapes=[pltpu.VMEM((nb, CONV_HIST, tn), F32)],
        compiler_params=_params("arbitrary", "arbitrary"),
        name="in_proj_conv",
    )(a, w3, conv_w, conv_b)


def _in_proj_gates_body(a_ref, wt_ref, g_ref):
    for rows in _row_blocks(a_ref.shape[0]):
        g_ref[rows, :] = _dot_nt(a_ref[rows, :], wt_ref[...]).astype(g_ref.dtype)


def _in_proj_gates(a, wt, tm, tn):
    m, k = a.shape
    n = wt.shape[0]
    assert n % tn == 0 and m % tm == 0
    return pl.pallas_call(
        _in_proj_gates_body,
        grid=(m // tm, n // tn),
        in_specs=[
            pl.BlockSpec((tm, k), lambda i, j: (i, 0)),
            pl.BlockSpec((tn, k), lambda i, j: (j, 0)),
        ],
        out_specs=pl.BlockSpec((tm, tn), lambda i, j: (i, j)),
        out_shape=jax.ShapeDtypeStruct((m, n), BF16),
        compiler_params=_params("parallel", "arbitrary"),
        name="in_proj_gates",
    )(a, wt)


def _norm_matmul_body(x_ref, nw_ref, w_ref, o_ref, xn_ref):
    @pl.when(pl.program_id(1) == 0)
    def _():
        xn_ref[...] = _rms(x_ref[...], nw_ref[...]).astype(BF16)

    o_ref[...] = _dot(xn_ref[...], w_ref[...].astype(BF16)).astype(o_ref.dtype)


def _norm_matmul(x, nw, w, out_dtype, tm, tn, name):
    m, k = x.shape
    n = w.shape[1]
    return pl.pallas_call(
        _norm_matmul_body,
        grid=(m // tm, n // tn),
        in_specs=[
            pl.BlockSpec((tm, k), lambda i, j: (i, 0)),
            pl.BlockSpec((1, k), lambda i, j: (0, 0)),
            pl.BlockSpec((k, tn), lambda i, j: (0, j)),
        ],
        out_specs=pl.BlockSpec((tm, tn), lambda i, j: (i, j)),
        out_shape=jax.ShapeDtypeStruct((m, n), out_dtype),
        scratch_shapes=[pltpu.VMEM((tm, k), BF16)],
        compiler_params=_params("parallel", "arbitrary"),
        name=name,
    )(x, nw, w)


def _retention_tables():
    idx = np.arange(CHUNK, dtype=np.float64)
    rel = idx[:, None] - idx[None, :]
    lg = _LOG_G[:, None, None]
    scale = RET_DK ** -0.5
    decay = np.where(rel[None] >= 0, np.exp(np.maximum(rel, 0.0)[None] * lg), 0.0) * scale
    xi = np.exp((idx + 1.0)[None, :, None] * lg) * scale
    zeta = np.exp((CHUNK - 1.0 - idx)[None, :, None] * lg)
    wide = (RET_HEADS, CHUNK, RET_DK)
    return (decay.astype(np.float32), np.broadcast_to(xi, wide).astype(np.float32),
            np.broadcast_to(zeta, wide).astype(np.float32))


def _retention_body(q_ref, k_ref, v_ref, g_ref, cos_ref, sin_ref, dec_ref, xi_ref, zeta_ref,
                    nw_ref, o_ref, state_ref):
    for cc, h in itertools.product(range(q_ref.shape[0] // CHUNK), range(RET_HEADS)):
        rows = slice(cc * CHUNK, (cc + 1) * CHUNK)
        cos = cos_ref[rows, :]
        sin = sin_ref[rows, :]
        q = q_ref[rows, h * RET_DK:(h + 1) * RET_DK].astype(F32)
        k = k_ref[rows, h * RET_DK:(h + 1) * RET_DK].astype(F32)
        v = v_ref[rows, h * RET_DV:(h + 1) * RET_DV]
        qr = q * cos + pltpu.roll(q, RET_DK // 2, 1) * sin
        kr = k * cos + pltpu.roll(k, RET_DK // 2, 1) * sin
        s = _dot_nt(qr.astype(BF16), kr.astype(BF16)) * dec_ref[h]
        st = state_ref[h]
        lhs = jnp.concatenate([s.astype(BF16), (qr * xi_ref[h]).astype(BF16)], axis=1)
        rhs = jnp.concatenate([v, st.astype(BF16)], axis=0)
        y = _dot(lhs, rhs)
        kz_t = (kr * zeta_ref[h]).T.astype(BF16)
        state_ref[h] = float(_RET_CHUNK_DECAY[h]) * st + _dot(kz_t, v)
        mu = jnp.mean(y, axis=-1, keepdims=True)
        yc = y - mu
        var = jnp.mean(yc * yc, axis=-1, keepdims=True)
        yn = yc * lax.rsqrt(var + EPS)
        hsl = slice(h * RET_DV, (h + 1) * RET_DV)
        o_ref[rows, hsl] = (yn * nw_ref[:, hsl] * _silu(g_ref[rows, hsl].astype(F32))).astype(BF16)


def _ssd_body(z_ref, xbc_ref, dt_ref, dtb_ref, alog_ref, dsk_ref, nw_ref,
              o_ref, prev_ref, *, inner, heads_per_group):

    c = CHUNK
    n_sub = z_ref.shape[0] // c
    gw = heads_per_group * SSD_HEAD_DIM
    row = lax.broadcasted_iota(jnp.int32, (c, c), 0)
    col = lax.broadcasted_iota(jnp.int32, (c, c), 1)
    tri = row >= col
    lane_lo = lax.broadcasted_iota(jnp.int32, (c, LANES), 1) < SSD_HEAD_DIM
    neg_inf = jnp.float32(-jnp.inf)
    a = -jnp.exp(alog_ref[...])

    for cc in range(n_sub):
        rows = slice(cc * c, (cc + 1) * c)
        dtr = dt_ref[rows, :] + dtb_ref[...]
        dt = jnp.maximum(dtr, 0.0) + jnp.log1p(jnp.exp(-jnp.abs(dtr)))
        da = dt * a
        a_cs = jnp.dot(tri.astype(F32), da, precision=lax.Precision.HIGHEST,
                       preferred_element_type=F32)
        a_cs = a_cs * LOG2_E
        a_last = a_cs[c - 1:c, :]
        w1 = jnp.exp2(a_last - a_cs) * dt
        src_t = (a_cs - jnp.log2(dt)).T
        w1_t = w1.T

        for g in range(SSD_GROUPS):
            b_gb = xbc_ref[rows, inner + g * SSD_STATE: inner + (g + 1) * SSD_STATE]
            c_gb = xbc_ref[rows, inner + (SSD_GROUPS + g) * SSD_STATE:
                           inner + (SSD_GROUPS + g + 1) * SSD_STATE]
            cb = _dot_nt(c_gb, b_gb).astype(BF16)
            b_gt = b_gb.astype(F32).T.astype(BF16)
            y_pairs = []
            for pr in range(heads_per_group // 2):
                h0 = g * heads_per_group + 2 * pr
                psl = slice(h0 * SSD_HEAD_DIM, (h0 + 2) * SSD_HEAD_DIM)
                xs_b = xbc_ref[rows, psl]
                prev = prev_ref[:, psl]
                rhs = jnp.concatenate([xs_b, prev.astype(BF16)], axis=0)
                lhs, bws, cds = [], [], []
                for h in (h0, h0 + 1):
                    colb = jnp.broadcast_to(a_cs[:, h:h + 1], (c, c))
                    rowb = jnp.broadcast_to(src_t[h:h + 1, :], (c, c))
                    m = cb * jnp.exp2(jnp.where(tri, colb - rowb, neg_inf)).astype(BF16)
                    e = jnp.exp2(colb)
                    lhs.append(jnp.concatenate([m, e.astype(BF16) * c_gb], axis=1))
                    bws.append(b_gt * jnp.broadcast_to(w1_t[h:h + 1, :], (c, c)).astype(BF16))
                    cds.append(e[c - 1:c, :])
                ys = _dot(jnp.concatenate(lhs, axis=0), rhs)
                sts = _dot(jnp.concatenate(bws, axis=0), xs_b)
                y_pairs.append(jnp.where(lane_lo, ys[0:c, :], ys[c:2 * c, :]))
                cd = jnp.where(lane_lo[0:1, :], cds[0], cds[1])
                prev_ref[:, psl] = cd * prev + jnp.where(lane_lo, sts[0:c, :], sts[c:2 * c, :])
            gsl = slice(g * gw, (g + 1) * gw)
            y = jnp.concatenate(y_pairs, axis=1) + dsk_ref[:, gsl] * xbc_ref[rows, gsl].astype(F32)
            gy = y * _silu(z_ref[rows, gsl].astype(F32))
            ms = jnp.mean(gy * gy, axis=-1, keepdims=True)
            o_ref[rows, gsl] = (gy * lax.rsqrt(ms + EPS) * nw_ref[:, gsl]).astype(BF16)


def _mixers_body(*refs, n_ret_in, n_ssd_in, n_cast, inner, heads_per_group):
    ret_in = refs[:n_ret_in]
    ssd_in = refs[n_ret_in:n_ret_in + n_ssd_in]
    rest = refs[n_ret_in + n_ssd_in:]
    cast_src, (y_ret_ref, y_ssd_ref) = rest[:n_cast], rest[n_cast:n_cast + 2]
    cast_dst, (state_ref, prev_ref) = rest[n_cast + 2:2 * n_cast + 2], rest[2 * n_cast + 2:]
    _cast_slabs(cast_src, cast_dst)

    @pl.when(pl.program_id(1) == 0)
    def _():
        state_ref[...] = jnp.zeros_like(state_ref)
        prev_ref[...] = jnp.zeros_like(prev_ref)

    _retention_body(*ret_in, y_ret_ref, state_ref)
    _ssd_body(*ssd_in, y_ssd_ref, prev_ref, inner=inner, heads_per_group=heads_per_group)


def _mixers(proj, ret_off, z_off, p_xbc, inner, cos2, sin2, ret_nw, dt_raw, dtb, alog, dskip,
            ssd_nw, to_bf16, bsz, seq):
    rows = MIX_STEP_CHUNKS * CHUNK
    nc = seq // rows
    qk = RET_HEADS * RET_DK
    vd = RET_HEADS * RET_DV
    conv_dim = p_xbc.shape[1]
    n_heads = inner // SSD_HEAD_DIM
    heads_per_group = n_heads // SSD_GROUPS
    assert seq % rows == 0 and vd == 2 * qk and ret_off % vd == 0 and z_off % inner == 0
    assert conv_dim == inner + 2 * SSD_GROUPS * SSD_STATE and SSD_STATE == CHUNK
    assert n_heads <= LANES and heads_per_group % 2 == 0 and 2 * SSD_HEAD_DIM == LANES
    oq, ov, zb = ret_off // qk, ret_off // vd, z_off // inner
    decay, xi, zeta = _retention_tables()
    tok = lambda blk: (lambda b, c: (b * nc + c, blk))
    fixed = lambda b, c: (0, 0)
    table = lambda n: pl.BlockSpec((RET_HEADS, CHUNK, n), lambda b, c: (0, 0, 0))
    cast_specs = _cast_slab_specs(to_bf16, bsz * nc, lambda b, c: b * nc + c)
    ret_specs = [
        pl.BlockSpec((rows, qk), tok(oq)),
        pl.BlockSpec((rows, qk), tok(oq + 1)),
        pl.BlockSpec((rows, vd), tok(ov + 1)),
        pl.BlockSpec((rows, vd), tok(ov + 2)),
        pl.BlockSpec((rows, RET_DK), lambda b, c: (c, 0)),
        pl.BlockSpec((rows, RET_DK), lambda b, c: (c, 0)),
        table(CHUNK), table(RET_DK), table(RET_DK),
        pl.BlockSpec((1, vd), fixed),
    ]
    ssd_specs = [
        pl.BlockSpec((rows, inner), tok(zb)),
        pl.BlockSpec((rows, conv_dim), tok(0)),
        pl.BlockSpec((rows, LANES), tok(0)),
        pl.BlockSpec((1, LANES), fixed),
        pl.BlockSpec((1, LANES), fixed),
        pl.BlockSpec((1, inner), fixed),
        pl.BlockSpec((1, inner), fixed),
    ]
    body = functools.partial(_mixers_body, n_ret_in=len(ret_specs), n_ssd_in=len(ssd_specs),
                             n_cast=len(to_bf16), inner=inner, heads_per_group=heads_per_group)
    return pl.pallas_call(
        body,
        grid=(bsz, nc),
        in_specs=ret_specs + ssd_specs + cast_specs,
        out_specs=[pl.BlockSpec((rows, vd), tok(0)), pl.BlockSpec((rows, inner), tok(0))] + cast_specs,
        out_shape=[jax.ShapeDtypeStruct((bsz * seq, vd), BF16),
                   jax.ShapeDtypeStruct((bsz * seq, inner), BF16)]
        + [jax.ShapeDtypeStruct(w.shape, BF16) for w in to_bf16],
        scratch_shapes=[pltpu.VMEM((RET_HEADS, RET_DK, RET_DV), F32),
                        pltpu.VMEM((SSD_STATE, inner), F32)],
        compiler_params=_params("arbitrary", "arbitrary"),
        name="mixers",
    )(proj, proj, proj, proj, cos2, sin2, decay, xi, zeta, ret_nw,
      proj, p_xbc, dt_raw, dtb, alog, dskip, ssd_nw, *to_bf16)


def _merge_body(yr_ref, ys_ref, wr_ref, ws_ref, ga_ref, gb_ref, o_ref):
    for rows in _row_blocks(yr_ref.shape[0]):
        br = _dot(yr_ref[rows, :], wr_ref[...])
        bs = _dot(ys_ref[rows, :], ws_ref[...])
        ga = _sigmoid(ga_ref[rows, :].astype(F32))
        gb = _sigmoid(gb_ref[rows, :].astype(F32))
        o_ref[rows, :] = (ga * br + gb * bs).astype(o_ref.dtype)


def _merge(y_ret, y_ssd, w_ret, w_ssd, gates, tm, tn):
    m, kr = y_ret.shape
    ks = y_ssd.shape[1]
    n = w_ret.shape[1]
    nb = n // tn
    return pl.pallas_call(
        _merge_body,
        grid=(m // tm, nb),
        in_specs=[
            pl.BlockSpec((tm, kr), lambda i, j: (i, 0)),
            pl.BlockSpec((tm, ks), lambda i, j: (i, 0)),
            pl.BlockSpec((kr, tn), lambda i, j: (0, j)),
            pl.BlockSpec((ks, tn), lambda i, j: (0, j)),
            pl.BlockSpec((tm, tn), lambda i, j: (i, j)),
            pl.BlockSpec((tm, tn), lambda i, j: (i, j + nb)),
        ],
        out_specs=pl.BlockSpec((tm, tn), lambda i, j: (i, j)),
        out_shape=jax.ShapeDtypeStruct((m, n), BF16),
        compiler_params=_params("parallel", "arbitrary"),
        name="merge",
    )(y_ret, y_ssd, w_ret, w_ssd, gates, gates)


def _mix_xattn_body(m_ref, x_ref, k_ref, v_ref, wmix_ref, nw_ref, wq_ref, wo_ref, o_ref, xn_ref):
    d = x_ref.shape[1]
    hd = d // XA_HEADS
    scale = hd ** -0.5
    o_ref[...] = x_ref[...] + _dot(m_ref[...], wmix_ref[...])
    xn_ref[...] = _rms(o_ref[...], nw_ref[...]).astype(BF16)
    for h in range(XA_HEADS):
        sl = slice(h * hd, (h + 1) * hd)
        q = _dot(xn_ref[...], wq_ref[:, sl]).astype(BF16)
        s = _dot_nt(q, k_ref[:, sl]) * scale
        s = s - jnp.max(s, axis=-1, keepdims=True)
        p = jnp.exp(s)
        p = p / jnp.sum(p, axis=-1, keepdims=True)
        o_h = _dot(p.astype(BF16), v_ref[:, sl]).astype(BF16)
        o_ref[...] += _dot(o_h, wo_ref[sl, :])


def _mix_xattn(merged, x, kv, w_mix, nw, w_q, w_o, bsz, seq, mem_len, tq):
    d = x.shape[1]
    nq = seq // tq
    tile = lambda b, i: (b * nq + i, 0)
    fixed = lambda b, i: (0, 0)
    resident = pl.Buffered(1)
    return pl.pallas_call(
        _mix_xattn_body,
        grid=(bsz, nq),
        in_specs=[
            pl.BlockSpec((tq, d), tile),
            pl.BlockSpec((tq, d), tile),
            pl.BlockSpec((mem_len, d), lambda b, i: (b, 0)),
            pl.BlockSpec((mem_len, d), lambda b, i: (b, 1)),
            pl.BlockSpec((d, d), fixed, pipeline_mode=resident),
            pl.BlockSpec((1, d), fixed),
            pl.BlockSpec((d, d), fixed, pipeline_mode=resident),
            pl.BlockSpec((d, d), fixed, pipeline_mode=resident),
        ],
        out_specs=pl.BlockSpec((tq, d), tile),
        out_shape=jax.ShapeDtypeStruct((bsz * seq, d), F32),
        scratch_shapes=[pltpu.VMEM((tq, d), BF16)],
        compiler_params=_params("parallel", "arbitrary"),
        name="mix_xattn",
    )(merged, x, kv, kv, w_mix, nw, w_q, w_o)


def _mlp_body(h_ref, nw_ref, w1_ref, w2_ref, fw_ref, o_ref, xn_ref):
    f = pl.program_id(1)
    blocks = _row_blocks(h_ref.shape[0])

    @pl.when(f == 0)
    def _():
        for rows in blocks:
            hh = h_ref[rows, :]
            xn_ref[rows, :] = _rms(hh, nw_ref[...]).astype(BF16)
            o_ref[rows, :] = hh

    for rows in blocks:
        u = _dot(xn_ref[rows, :], w1_ref[...])
        u = jnp.square(jnp.maximum(u, 0.0)).astype(BF16)
        o_ref[rows, :] += _dot(u, w2_ref[...])

    @pl.when(f == pl.num_programs(1) - 1)
    def _():
        for rows in blocks:
            o_ref[rows, :] = _rms(o_ref[rows, :], fw_ref[...])


def _mlp(h, nw, w1, w2, fw, tm, tf):
    m, d = h.shape
    dff = w1.shape[1]
    return pl.pallas_call(
        _mlp_body,
        grid=(m // tm, dff // tf),
        in_specs=[
            pl.BlockSpec((tm, d), lambda i, f: (i, 0)),
            pl.BlockSpec((1, d), lambda i, f: (0, 0)),
            pl.BlockSpec((d, tf), lambda i, f: (0, f)),
            pl.BlockSpec((tf, d), lambda i, f: (f, 0)),
            pl.BlockSpec((1, d), lambda i, f: (0, 0)),
        ],
        out_specs=pl.BlockSpec((tm, d), lambda i, f: (i, 0)),
        out_shape=jax.ShapeDtypeStruct((m, d), F32),
        scratch_shapes=[pltpu.VMEM((tm, d), BF16)],
        compiler_params=_params("parallel", "arbitrary"),
        name="mlp",
    )(h, nw, w1, w2, fw)


def _rope_tables(seq, dk):
    half = dk // 2
    inv = np.exp(-math.log(ROPE_BASE) * np.arange(half, dtype=np.float64) / half)
    ang = np.arange(seq, dtype=np.float64)[:, None] * inv[None, :]
    cos, sin = np.cos(ang).astype(np.float32), np.sin(ang).astype(np.float32)
    return np.concatenate([cos, cos], axis=1), np.concatenate([-sin, sin], axis=1)


def kernel(x, mem, norm_mix_w, w_in, conv_w, conv_b, dt_bias, a_log, d_skip, ret_norm_w, ssd_norm_w, w_ret_out, w_ssd_out, w_mix_out, norm_xa_w, mem_norm_w, w_xq, w_xkv, w_xo, norm_ff_w, w_ff1, w_ff2, final_norm_w):
    bsz, seq, d = x.shape
    mem_len = mem.shape[1]
    depth = w_in.shape[0]
    n_tok = bsz * seq
    ret_qk = RET_HEADS * RET_DK
    ret_v = RET_HEADS * RET_DV
    inner = w_ssd_out.shape[1]
    conv_dim = conv_w.shape[2]
    n_heads = inner // SSD_HEAD_DIM
    ret_w = 2 * ret_qk + 2 * ret_v
    off_dt = ret_w + inner + conv_dim
    off_gate = off_dt + n_heads
    assert w_in.shape[2] == off_gate + 2 * d and depth == 1

    tm_main, tn_main = IN_PROJ_TILE
    assert conv_dim % tn_main == 0 and ret_w % tn_main == 0 and inner % tn_main == 0
    shift = inner // tn_main
    z_off, ret_off = 0, inner

    cos2, sin2 = _rope_tables(seq, RET_DK)
    row = lambda v: v.reshape(1, -1)
    pad_heads = lambda v: jnp.pad(v.reshape(1, -1), ((0, 0), (0, LANES - n_heads)))

    w_in_t = jnp.swapaxes(w_in, 1, 2)

    h = x.reshape(n_tok, d)
    for l in range(depth):
        wt_dt = jnp.pad(w_in_t[l, off_dt:off_gate], ((0, LANES - n_heads), (0, 0)))
        xn, dt_raw = _prenorm(h, row(norm_mix_w[l]), wt_dt, PRENORM_ROWS)
        proj, wt_gates, (w_ret_b, w_ssd_b, w_mix_b, w_xq_b, w_xo_b) = _in_proj_main(
            xn, w_in_t, l, 0, ret_w + inner, shift,
            (w_ret_out[l], w_ssd_out[l], w_mix_out[l], w_xq[l], w_xo[l]),
            off_gate, 2 * d, n_heads, tm_main, tn_main)
        p_xbc = _in_proj_conv(xn, w_in_t, l, ret_w + inner, conv_dim, conv_w[l], row(conv_b[l]),
                              seq, tm_main, tn_main)
        gates = _in_proj_gates(xn, wt_gates, *GATES_TILE)
        y_ret, y_ssd, w_ff1_b, w_ff2_b = _mixers(
            proj, ret_off, z_off, p_xbc, inner, cos2, sin2, row(ret_norm_w[l]), dt_raw,
            pad_heads(dt_bias[l]), pad_heads(a_log[l]), row(jnp.repeat(d_skip[l], SSD_HEAD_DIM)),
            row(ssd_norm_w[l]), (w_ff1[l], w_ff2[l]), bsz, seq)
        merged = _merge(y_ret, y_ssd, w_ret_b, w_ssd_b, gates, *MERGE_TILE)
        kv = _norm_matmul(mem.reshape(bsz * mem_len, d), row(mem_norm_w[l]), w_xkv[l],
                          BF16, bsz * mem_len, KV_COLS, "xa_kv")
        h = _mix_xattn(merged, h, kv, w_mix_b, row(norm_xa_w[l]), w_xq_b, w_xo_b,
                       bsz, seq, mem_len, XATTN_ROWS)
        h = _mlp(h, row(norm_ff_w[l]), w_ff1_b, w_ff2_b, row(final_norm_w), *MLP_TILE)
    return h.reshape(bsz, seq, d)
```

```python
import functools
import itertools
import math

import numpy as np
import jax
import jax.numpy as jnp
from jax import lax
from jax.experimental import pallas as pl
from jax.experimental.pallas import tpu as pltpu

F32 = jnp.float32
BF16 = jnp.bfloat16
EPS = 1e-6

RET_HEADS = 8
RET_DK = 128
RET_DV = 256
CHUNK = 128
ROPE_BASE = 10000.0
SSD_HEAD_DIM = 64
SSD_GROUPS = 8
SSD_STATE = 128
SSD_CONV = 4
CONV_HIST = 8
XA_HEADS = 4

V7X_VMEM_BYTES = 64 * 1024 * 1024
VMEM_LIMIT = V7X_VMEM_BYTES - 4 * 1024 * 1024
LANES = 128
ROW_SPLIT = 512
CONV_ROW_SPLIT = 256
LOG2_E = math.log2(math.e)

PRENORM_ROWS = 1024
IN_PROJ_TILE = (2048, 1024)
CAST_SLABS = 32
GATES_TILE = (2048, 2048)
MERGE_TILE = (1024, 512)
KV_COLS = 1024
XATTN_ROWS = 512
MLP_TILE = (1024, 1024)
MIX_STEP_CHUNKS = 2

_LOG_G = np.log(1.0 - np.exp2(-5.0 - np.arange(RET_HEADS, dtype=np.float64)))
_RET_CHUNK_DECAY = np.exp(CHUNK * _LOG_G)


def _params(*sem):
    return pltpu.CompilerParams(dimension_semantics=sem, vmem_limit_bytes=VMEM_LIMIT)


def _rms(x, w):
    ms = jnp.mean(x * x, axis=-1, keepdims=True)
    return x * lax.rsqrt(ms + EPS) * w


def _sigmoid(x):
    return 0.5 + 0.5 * jnp.tanh(0.5 * x)


def _silu(x):
    h = 0.5 * x
    return h + h * jnp.tanh(h)


def _dot(a, b):
    return jnp.dot(a, b, preferred_element_type=F32)


def _dot_nt(a, b):
    return lax.dot_general(a, b, (((1,), (1,)), ((), ())), preferred_element_type=F32)


def _cast_slab_specs(weights, n_steps, step_of, n_slabs=None):
    n_slabs = min(CAST_SLABS, n_steps) if n_slabs is None else n_slabs
    assert n_slabs <= n_steps and all(w.shape[0] % n_slabs == 0 for w in weights)
    slab = lambda *idx: (jnp.minimum(step_of(*idx), n_slabs - 1), 0)
    return [pl.BlockSpec((w.shape[0] // n_slabs, w.shape[1]), slab) for w in weights]


def _cast_slabs(src_refs, dst_refs):
    for src_ref, dst_ref in zip(src_refs, dst_refs):
        dst_ref[...] = src_ref[...].astype(dst_ref.dtype)


def _row_blocks(n_rows, step=ROW_SPLIT):
    step = min(step, n_rows)
    return [slice(r, r + step) for r in range(0, n_rows, step)]


def _prenorm_body(x_ref, nw_ref, wdt_ref, o_ref, dt_ref):
    xn = _rms(x_ref[...], nw_ref[...]).astype(o_ref.dtype)
    o_ref[...] = xn
    dt_ref[...] = _dot_nt(xn, wdt_ref[...].astype(BF16))


def _prenorm(x, nw, wt_dt, tm):
    m, k = x.shape
    n_dt = wt_dt.shape[0]
    return pl.pallas_call(
        _prenorm_body,
        grid=(m // tm,),
        in_specs=[pl.BlockSpec((tm, k), lambda i: (i, 0)), pl.BlockSpec((1, k), lambda i: (0, 0)),
                  pl.BlockSpec((n_dt, k), lambda i: (0, 0))],
        out_specs=[pl.BlockSpec((tm, k), lambda i: (i, 0)), pl.BlockSpec((tm, n_dt), lambda i: (i, 0))],
        out_shape=[jax.ShapeDtypeStruct((m, k), BF16), jax.ShapeDtypeStruct((m, n_dt), F32)],
        compiler_params=_params("parallel"),
        name="prenorm",
    )(x, nw, wt_dt)


def _matmul_nt_body(a_ref, wt_ref, g0_ref, g1_ref, *refs):
    n_cast = (len(refs) - 2) // 2
    o_ref, gate_w_ref = refs[n_cast], refs[n_cast + 1]
    wb = wt_ref[...].astype(BF16)
    for rows in _row_blocks(a_ref.shape[0]):
        o_ref[rows, :] = _dot_nt(a_ref[rows, :], wb).astype(o_ref.dtype)
    _cast_slabs(refs[:n_cast], refs[n_cast + 2:])
    slab = g0_ref.shape[0]
    gate_w_ref[0:slab, :] = g0_ref[...].astype(gate_w_ref.dtype)
    gate_w_ref[slab:2 * slab, :] = g1_ref[...].astype(gate_w_ref.dtype)


def _in_proj_main(a, w3, layer, row_off, n_cols, shift, to_bf16, gate_row0, n_gate_rows,
                  slab, tm, tn):
    m, k = a.shape
    nb = n_cols // tn
    off = row_off // tn
    n_steps = (m // tm) * nb
    assert n_cols % tn == 0 and m % tm == 0 and row_off % tn == 0
    cast_specs = _cast_slab_specs(to_bf16, n_steps, lambda i, j: i * nb + j)
    gate_blocks = n_gate_rows // (2 * slab)
    gate0 = gate_row0 // slab
    assert gate_row0 % slab == 0 and n_gate_rows % (2 * slab) == 0 and gate_blocks <= n_steps
    gate_blk = lambda i, j: jnp.minimum(i * nb + j, gate_blocks - 1)
    gate_src = lambda t: pl.BlockSpec(
        (None, slab, k), lambda i, j: (layer, gate0 + 2 * gate_blk(i, j) + t, 0))

    outs = pl.pallas_call(
        _matmul_nt_body,
        grid=(m // tm, nb),
        in_specs=[
            pl.BlockSpec((tm, k), lambda i, j: (i, 0)),
            pl.BlockSpec((None, tn, k), lambda i, j: (layer, j + off, 0)),
            gate_src(0), gate_src(1),
        ] + cast_specs,
        out_specs=[pl.BlockSpec((tm, tn), lambda i, j: (i, (j + shift) % nb)),
                   pl.BlockSpec((2 * slab, k), lambda i, j: (gate_blk(i, j), 0))] + cast_specs,
        out_shape=[jax.ShapeDtypeStruct((m, n_cols), BF16),
                   jax.ShapeDtypeStruct((n_gate_rows, k), BF16)]
        + [jax.ShapeDtypeStruct(w.shape, BF16) for w in to_bf16],
        compiler_params=_params("arbitrary", "arbitrary"),
        name="in_proj_main",
    )(a, w3, w3, w3, *to_bf16)
    return outs[0], outs[1], outs[2:]


def _in_proj_conv_body(a_ref, wt_ref, cw_ref, cb_ref, o_ref, carry_ref, *, tiles_per_seq):
    i = pl.program_id(0)
    j = pl.program_id(1)
    tn = wt_ref.shape[0]

    @pl.when(i % tiles_per_seq == 0)
    def _():
        carry_ref[j] = jnp.zeros((CONV_HIST, tn), F32)

    wb = wt_ref[...].astype(BF16)
    cwh = 0.5 * cw_ref[...]
    cbh = 0.5 * cb_ref[...]
    hist = carry_ref[j]
    for rows in _row_blocks(a_ref.shape[0], CONV_ROW_SPLIT):
        res = _dot_nt(a_ref[rows, :], wb)
        nt = res.shape[0] // CONV_HIST
        tiles = jnp.concatenate([hist, res], axis=0).reshape(nt + 1, CONV_HIST, tn)
        sub = lax.broadcasted_iota(jnp.int32, (nt, CONV_HIST, tn), 1)
        acc = cbh + cwh[SSD_CONV - 1:SSD_CONV, :] * res
        for k in range(SSD_CONV - 1):
            dist = SSD_CONV - 1 - k
            mixed = jnp.where(sub >= CONV_HIST - dist, tiles[:-1], tiles[1:])
            shifted = pltpu.roll(mixed, dist, 1).reshape(nt * CONV_HIST, tn)
            acc = acc + cwh[k:k + 1, :] * shifted
        o_ref[rows, :] = (acc + acc * jnp.tanh(acc)).astype(o_ref.dtype)
        hist = res[res.shape[0] - CONV_HIST:, :]
    carry_ref[j] = hist


def _in_proj_conv(a, w3, layer, row_off, n_cols, conv_w, conv_b, seq, tm, tn):
    m, k = a.shape
    nb = n_cols // tn
    off = row_off // tn
    assert n_cols % tn == 0 and m % tm == 0 and row_off % tn == 0 and seq % tm == 0
    assert SSD_CONV - 1 <= CONV_HIST
    body = functools.partial(_in_proj_conv_body, tiles_per_seq=seq // tm)
    return pl.pallas_call(
        body,
        grid=(m // tm, nb),
        in_specs=[
            pl.BlockSpec((tm, k), lambda i, j: (i, 0)),
            pl.BlockSpec((None, tn, k), lambda i, j: (layer, j + off, 0)),
            pl.BlockSpec((SSD_CONV, tn), lambda i, j: (0, j)),
            pl.BlockSpec((1, tn), lambda i, j: (0, j)),
        ],
        out_specs=pl.BlockSpec((tm, tn), lambda i, j: (i, j)),
        out_shape=jax.ShapeDtypeStruct((m, n_cols), BF16),
        scratch_shapes=[pltpu.VMEM((nb, CONV_HIST, tn), F32)],
        compiler_params=_params("arbitrary", "arbitrary"),
        name="in_proj_conv",
    )(a, w3, conv_w, conv_b)


def _in_proj_gates_body(a_ref, wt_ref, g_ref):
    for rows in _row_blocks(a_ref.shape[0]):
        g_ref[rows, :] = _dot_nt(a_ref[rows, :], wt_ref[...]).astype(g_ref.dtype)


def _in_proj_gates(a, wt, tm, tn):
    m, k = a.shape
    n = wt.shape[0]
    assert n % tn == 0 and m % tm == 0
    return pl.pallas_call(
        _in_proj_gates_body,
        grid=(m // tm, n // tn),
        in_specs=[
            pl.BlockSpec((tm, k), lambda i, j: (i, 0)),
            pl.BlockSpec((tn, k), lambda i, j: (j, 0)),
        ],
        out_specs=pl.BlockSpec((tm, tn), lambda i, j: (i, j)),
        out_shape=jax.ShapeDtypeStruct((m, n), BF16),
        compiler_params=_params("parallel", "arbitrary"),
        name="in_proj_gates",
    )(a, wt)


def _norm_matmul_body(x_ref, nw_ref, w_ref, o_ref, xn_ref):
    @pl.when(pl.program_id(1) == 0)
    def _():
        xn_ref[...] = _rms(x_ref[...], nw_ref[...]).astype(BF16)

    o_ref[...] = _dot(xn_ref[...], w_ref[...].astype(BF16)).astype(o_ref.dtype)


def _norm_matmul(x, nw, w, out_dtype, tm, tn, name):
    m, k = x.shape
    n = w.shape[1]
    return pl.pallas_call(
        _norm_matmul_body,
        grid=(m // tm, n // tn),
        in_specs=[
            pl.BlockSpec((tm, k), lambda i, j: (i, 0)),
            pl.BlockSpec((1, k), lambda i, j: (0, 0)),
            pl.BlockSpec((k, tn), lambda i, j: (0, j)),
        ],
        out_specs=pl.BlockSpec((tm, tn), lambda i, j: (i, j)),
        out_shape=jax.ShapeDtypeStruct((m, n), out_dtype),
        scratch_shapes=[pltpu.VMEM((tm, k), BF16)],
        compiler_params=_params("parallel", "arbitrary"),
        name=name,
    )(x, nw, w)


def _retention_tables():
    idx = np.arange(CHUNK, dtype=np.float64)
    rel = idx[:, None] - idx[None, :]
    lg = _LOG_G[:, None, None]
    scale = RET_DK ** -0.5
    decay = np.where(rel[None] >= 0, np.exp(np.maximum(rel, 0.0)[None] * lg), 0.0) * scale
    xi = np.exp((idx + 1.0)[None, :, None] * lg) * scale
    zeta = np.exp((CHUNK - 1.0 - idx)[None, :, None] * lg)
    wide = (RET_HEADS, CHUNK, RET_DK)
    return (decay.astype(np.float32), np.broadcast_to(xi, wide).astype(np.float32),
            np.broadcast_to(zeta, wide).astype(np.float32))


def _retention_body(q_ref, k_ref, v_ref, g_ref, cos_ref, sin_ref, dec_ref, xi_ref, zeta_ref,
                    nw_ref, o_ref, state_ref):
    for cc, h in itertools.product(range(q_ref.shape[0] // CHUNK), range(RET_HEADS)):
        rows = slice(cc * CHUNK, (cc + 1) * CHUNK)
        cos = cos_ref[rows, :]
        sin = sin_ref[rows, :]
        q = q_ref[rows, h * RET_DK:(h + 1) * RET_DK].astype(F32)
        k = k_ref[rows, h * RET_DK:(h + 1) * RET_DK].astype(F32)
        v = v_ref[rows, h * RET_DV:(h + 1) * RET_DV]
        qr = q * cos + pltpu.roll(q, RET_DK // 2, 1) * sin
        kr = k * cos + pltpu.roll(k, RET_DK // 2, 1) * sin
        s = _dot_nt(qr.astype(BF16), kr.astype(BF16)) * dec_ref[h]
        st = state_ref[h]
        lhs = jnp.concatenate([s.astype(BF16), (qr * xi_ref[h]).astype(BF16)], axis=1)
        rhs = jnp.concatenate([v, st.astype(BF16)], axis=0)
        y = _dot(lhs, rhs)
        kz_t = (kr * zeta_ref[h]).T.astype(BF16)
        state_ref[h] = float(_RET_CHUNK_DECAY[h]) * st + _dot(kz_t, v)
        mu = jnp.mean(y, axis=-1, keepdims=True)
        yc = y - mu
        var = jnp.mean(yc * yc, axis=-1, keepdims=True)
        yn = yc * lax.rsqrt(var + EPS)
        hsl = slice(h * RET_DV, (h + 1) * RET_DV)
        o_ref[rows, hsl] = (yn * nw_ref[:, hsl] * _silu(g_ref[rows, hsl].astype(F32))).astype(BF16)


def _ssd_body(z_ref, xbc_ref, dt_ref, dtb_ref, alog_ref, dsk_ref, nw_ref, wsrc_ref,
              o_ref, wdst_ref, prev_ref, *, inner, heads_per_group):
    _cast_slabs([wsrc_ref], [wdst_ref])

    c = CHUNK
    n_sub = z_ref.shape[0] // c
    gw = heads_per_group * SSD_HEAD_DIM
    row = lax.broadcasted_iota(jnp.int32, (c, c), 0)
    col = lax.broadcasted_iota(jnp.int32, (c, c), 1)
    tri = row >= col
    tri_b = tri.astype(BF16)
    tri3 = jnp.concatenate([tri_b, tri_b, tri_b], axis=1)
    lane_lo = lax.broadcasted_iota(jnp.int32, (c, LANES), 1) < SSD_HEAD_DIM
    neg_inf = jnp.float32(-jnp.inf)
    a = -jnp.exp(alog_ref[...])

    for cc in range(n_sub):
        rows = slice(cc * c, (cc + 1) * c)
        dtr = dt_ref[rows, :] + dtb_ref[...]
        dt = jnp.maximum(dtr, 0.0) + jnp.log1p(jnp.exp(-jnp.abs(dtr)))
        da = dt * a
        da_hi = da.astype(BF16)
        da_r1 = da - da_hi.astype(F32)
        da_mid = da_r1.astype(BF16)
        da_lo = (da_r1 - da_mid.astype(F32)).astype(BF16)
        a_cs = _dot(tri3, jnp.concatenate([da_hi, da_mid, da_lo], axis=0))
        a_cs = a_cs * LOG2_E
        a_last = a_cs[c - 1:c, :]
        w1 = jnp.exp2(a_last - a_cs) * dt
        src_t = (a_cs - jnp.log2(dt)).T
        w1_t = w1.T

        for g in range(SSD_GROUPS):
            b_gb = xbc_ref[rows, inner + g * SSD_STATE: inner + (g + 1) * SSD_STATE]
            c_gb = xbc_ref[rows, inner + (SSD_GROUPS + g) * SSD_STATE:
                           inner + (SSD_GROUPS + g + 1) * SSD_STATE]
            cb = _dot_nt(c_gb, b_gb).astype(BF16)
            b_gt = b_gb.astype(F32).T.astype(BF16)
            y_pairs = []
            for pr in range(heads_per_group // 2):
                h0 = g * heads_per_group + 2 * pr
                psl = slice(h0 * SSD_HEAD_DIM, (h0 + 2) * SSD_HEAD_DIM)
                xs_b = xbc_ref[rows, psl]
                prev = prev_ref[:, psl]
                rhs = jnp.concatenate([xs_b, prev.astype(BF16)], axis=0)
                lhs, bws, cds = [], [], []
                for h in (h0, h0 + 1):
                    colb = jnp.broadcast_to(a_cs[:, h:h + 1], (c, c))
                    rowb = jnp.broadcast_to(src_t[h:h + 1, :], (c, c))
                    m = cb * jnp.exp2(jnp.where(tri, colb - rowb, neg_inf)).astype(BF16)
                    e = jnp.exp2(colb)
                    lhs.append(jnp.concatenate([m, e.astype(BF16) * c_gb], axis=1))
                    bws.append(b_gt * jnp.broadcast_to(w1_t[h:h + 1, :], (c, c)).astype(BF16))
                    cds.append(e[c - 1:c, :])
                ys = _dot(jnp.concatenate(lhs, axis=0), rhs)
                sts = _dot(jnp.concatenate(bws, axis=0), xs_b)
                y_pairs.append(jnp.where(lane_lo, ys[0:c, :], ys[c:2 * c, :]))
                cd = jnp.where(lane_lo[0:1, :], cds[0], cds[1])
                prev_ref[:, psl] = cd * prev + jnp.where(lane_lo, sts[0:c, :], sts[c:2 * c, :])
            gsl = slice(g * gw, (g + 1) * gw)
            y = jnp.concatenate(y_pairs, axis=1) + dsk_ref[:, gsl] * xbc_ref[rows, gsl].astype(F32)
            gy = y * _silu(z_ref[rows, gsl].astype(F32))
            ms = jnp.mean(gy * gy, axis=-1, keepdims=True)
            o_ref[rows, gsl] = (gy * lax.rsqrt(ms + EPS) * nw_ref[:, gsl]).astype(BF16)


def _mixers_body(*refs, n_ret_in, n_ssd_in, inner, heads_per_group):
    ret_in = refs[:n_ret_in]
    ssd_in = refs[n_ret_in:n_ret_in + n_ssd_in]
    y_ret_ref, y_ssd_ref, wdst_ref, state_ref, prev_ref = refs[n_ret_in + n_ssd_in:]

    @pl.when(pl.program_id(1) == 0)
    def _():
        state_ref[...] = jnp.zeros_like(state_ref)
        prev_ref[...] = jnp.zeros_like(prev_ref)

    _retention_body(*ret_in, y_ret_ref, state_ref)
    _ssd_body(*ssd_in, y_ssd_ref, wdst_ref, prev_ref, inner=inner, heads_per_group=heads_per_group)


def _mixers(proj, ret_off, z_off, p_xbc, inner, cos2, sin2, ret_nw, dt_raw, dtb, alog, dskip,
            ssd_nw, w_to_bf16, bsz, seq):
    rows = MIX_STEP_CHUNKS * CHUNK
    nc = seq // rows
    qk = RET_HEADS * RET_DK
    vd = RET_HEADS * RET_DV
    conv_dim = p_xbc.shape[1]
    n_heads = inner // SSD_HEAD_DIM
    heads_per_group = n_heads // SSD_GROUPS
    assert seq % rows == 0 and vd == 2 * qk and ret_off % vd == 0 and z_off % inner == 0
    assert conv_dim == inner + 2 * SSD_GROUPS * SSD_STATE and SSD_STATE == CHUNK
    assert n_heads <= LANES and heads_per_group % 2 == 0 and 2 * SSD_HEAD_DIM == LANES
    oq, ov, zb = ret_off // qk, ret_off // vd, z_off // inner
    decay, xi, zeta = _retention_tables()
    tok = lambda blk: (lambda b, c: (b * nc + c, blk))
    fixed = lambda b, c: (0, 0)
    table = lambda n: pl.BlockSpec((RET_HEADS, CHUNK, n), lambda b, c: (0, 0, 0))
    cast_spec = _cast_slab_specs([w_to_bf16], bsz * nc, lambda b, c: b * nc + c)
    ret_specs = [
        pl.BlockSpec((rows, qk), tok(oq)),
        pl.BlockSpec((rows, qk), tok(oq + 1)),
        pl.BlockSpec((rows, vd), tok(ov + 1)),
        pl.BlockSpec((rows, vd), tok(ov + 2)),
        pl.BlockSpec((rows, RET_DK), lambda b, c: (c, 0)),
        pl.BlockSpec((rows, RET_DK), lambda b, c: (c, 0)),
        table(CHUNK), table(RET_DK), table(RET_DK),
        pl.BlockSpec((1, vd), fixed),
    ]
    ssd_specs = [
        pl.BlockSpec((rows, inner), tok(zb)),
        pl.BlockSpec((rows, conv_dim), tok(0)),
        pl.BlockSpec((rows, LANES), tok(0)),
        pl.BlockSpec((1, LANES), fixed),
        pl.BlockSpec((1, LANES), fixed),
        pl.BlockSpec((1, inner), fixed),
        pl.BlockSpec((1, inner), fixed),
    ] + cast_spec
    body = functools.partial(_mixers_body, n_ret_in=len(ret_specs), n_ssd_in=len(ssd_specs),
                             inner=inner, heads_per_group=heads_per_group)
    return pl.pallas_call(
        body,
        grid=(bsz, nc),
        in_specs=ret_specs + ssd_specs,
        out_specs=[pl.BlockSpec((rows, vd), tok(0)), pl.BlockSpec((rows, inner), tok(0))] + cast_spec,
        out_shape=[jax.ShapeDtypeStruct((bsz * seq, vd), BF16),
                   jax.ShapeDtypeStruct((bsz * seq, inner), BF16),
                   jax.ShapeDtypeStruct(w_to_bf16.shape, BF16)],
        scratch_shapes=[pltpu.VMEM((RET_HEADS, RET_DK, RET_DV), F32),
                        pltpu.VMEM((SSD_STATE, inner), F32)],
        compiler_params=_params("arbitrary", "arbitrary"),
        name="mixers",
    )(proj, proj, proj, proj, cos2, sin2, decay, xi, zeta, ret_nw,
      proj, p_xbc, dt_raw, dtb, alog, dskip, ssd_nw, w_to_bf16)


def _merge_body(yr_ref, ys_ref, wr_ref, ws_ref, ga_ref, gb_ref, wsrc_ref, o_ref, wdst_ref):
    _cast_slabs([wsrc_ref], [wdst_ref])
    for rows in _row_blocks(yr_ref.shape[0]):
        br = _dot(yr_ref[rows, :], wr_ref[...])
        bs = _dot(ys_ref[rows, :], ws_ref[...])
        ga = _sigmoid(ga_ref[rows, :].astype(F32))
        gb = _sigmoid(gb_ref[rows, :].astype(F32))
        o_ref[rows, :] = (ga * br + gb * bs).astype(o_ref.dtype)


def _merge(y_ret, y_ssd, w_ret, w_ssd, gates, w_to_bf16, tm, tn):
    m, kr = y_ret.shape
    ks = y_ssd.shape[1]
    n = w_ret.shape[1]
    nb = n // tn
    cast_spec = _cast_slab_specs([w_to_bf16], (m // tm) * nb, lambda i, j: i * nb + j)
    return pl.pallas_call(
        _merge_body,
        grid=(m // tm, nb),
        in_specs=[
            pl.BlockSpec((tm, kr), lambda i, j: (i, 0)),
            pl.BlockSpec((tm, ks), lambda i, j: (i, 0)),
            pl.BlockSpec((kr, tn), lambda i, j: (0, j)),
            pl.BlockSpec((ks, tn), lambda i, j: (0, j)),
            pl.BlockSpec((tm, tn), lambda i, j: (i, j)),
            pl.BlockSpec((tm, tn), lambda i, j: (i, j + nb)),
        ] + cast_spec,
        out_specs=[pl.BlockSpec((tm, tn), lambda i, j: (i, j))] + cast_spec,
        out_shape=[jax.ShapeDtypeStruct((m, n), BF16), jax.ShapeDtypeStruct(w_to_bf16.shape, BF16)],
        compiler_params=_params("arbitrary", "arbitrary"),
        name="merge",
    )(y_ret, y_ssd, w_ret, w_ssd, gates, gates, w_to_bf16)


def _mix_xattn_body(m_ref, x_ref, k_ref, v_ref, wmix_ref, nw_ref, wq_ref, wo_ref, o_ref,
                    xn_ref, q_ref):
    d = x_ref.shape[1]
    hd = d // XA_HEADS
    scale = hd ** -0.5
    o_ref[...] = x_ref[...] + _dot(m_ref[...], wmix_ref[...])
    xn_ref[...] = _rms(o_ref[...], nw_ref[...]).astype(BF16)
    q_ref[...] = _dot(xn_ref[...], wq_ref[...]).astype(BF16)
    for h in range(XA_HEADS):
        sl = slice(h * hd, (h + 1) * hd)
        s = _dot_nt(q_ref[:, sl], k_ref[:, sl]) * scale
        s = s - jnp.max(s, axis=-1, keepdims=True)
        p = jnp.exp(s)
        p = p / jnp.sum(p, axis=-1, keepdims=True)
        xn_ref[:, sl] = _dot(p.astype(BF16), v_ref[:, sl]).astype(BF16)
    o_ref[...] += _dot(xn_ref[...], wo_ref[...])


def _mix_xattn(merged, x, kv, w_mix, nw, w_q, w_o, bsz, seq, mem_len, tq):
    d = x.shape[1]
    nq = seq // tq
    tile = lambda b, i: (b * nq + i, 0)
    fixed = lambda b, i: (0, 0)
    resident = pl.Buffered(1)
    return pl.pallas_call(
        _mix_xattn_body,
        grid=(bsz, nq),
        in_specs=[
            pl.BlockSpec((tq, d), tile),
            pl.BlockSpec((tq, d), tile),
            pl.BlockSpec((mem_len, d), lambda b, i: (b, 0)),
            pl.BlockSpec((mem_len, d), lambda b, i: (b, 1)),
            pl.BlockSpec((d, d), fixed, pipeline_mode=resident),
            pl.BlockSpec((1, d), fixed),
            pl.BlockSpec((d, d), fixed, pipeline_mode=resident),
            pl.BlockSpec((d, d), fixed, pipeline_mode=resident),
        ],
        out_specs=pl.BlockSpec((tq, d), tile),
        out_shape=jax.ShapeDtypeStruct((bsz * seq, d), F32),
        scratch_shapes=[pltpu.VMEM((tq, d), BF16), pltpu.VMEM((tq, d), BF16)],
        compiler_params=_params("parallel", "arbitrary"),
        name="mix_xattn",
    )(merged, x, kv, kv, w_mix, nw, w_q, w_o)


def _mlp_body(h_ref, nw_ref, w1_ref, w2_ref, fw_ref, o_ref, xn_ref):
    f = pl.program_id(1)
    blocks = _row_blocks(h_ref.shape[0])

    @pl.when(f == 0)
    def _():
        for rows in blocks:
            hh = h_ref[rows, :]
            xn_ref[rows, :] = _rms(hh, nw_ref[...]).astype(BF16)
            o_ref[rows, :] = hh

    for rows in blocks:
        u = _dot(xn_ref[rows, :], w1_ref[...])
        u = jnp.square(jnp.maximum(u, 0.0)).astype(BF16)
        o_ref[rows, :] += _dot(u, w2_ref[...])

    @pl.when(f == pl.num_programs(1) - 1)
    def _():
        for rows in blocks:
            o_ref[rows, :] = _rms(o_ref[rows, :], fw_ref[...])


def _mlp(h, nw, w1, w2, fw, tm, tf):
    m, d = h.shape
    dff = w1.shape[1]
    return pl.pallas_call(
        _mlp_body,
        grid=(m // tm, dff // tf),
        in_specs=[
            pl.BlockSpec((tm, d), lambda i, f: (i, 0)),
            pl.BlockSpec((1, d), lambda i, f: (0, 0)),
            pl.BlockSpec((d, tf), lambda i, f: (0, f)),
            pl.BlockSpec((tf, d), lambda i, f: (f, 0)),
            pl.BlockSpec((1, d), lambda i, f: (0, 0)),
        ],
        out_specs=pl.BlockSpec((tm, d), lambda i, f: (i, 0)),
        out_shape=jax.ShapeDtypeStruct((m, d), F32),
        scratch_shapes=[pltpu.VMEM((tm, d), BF16)],
        compiler_params=_params("parallel", "arbitrary"),
        name="mlp",
    )(h, nw, w1, w2, fw)


def _rope_tables(seq, dk):
    half = dk // 2
    inv = np.exp(-math.log(ROPE_BASE) * np.arange(half, dtype=np.float64) / half)
    ang = np.arange(seq, dtype=np.float64)[:, None] * inv[None, :]
    cos, sin = np.cos(ang).astype(np.float32), np.sin(ang).astype(np.float32)
    return np.concatenate([cos, cos], axis=1), np.concatenate([-sin, sin], axis=1)


def kernel(x, mem, norm_mix_w, w_in, conv_w, conv_b, dt_bias, a_log, d_skip, ret_norm_w, ssd_norm_w, w_ret_out, w_ssd_out, w_mix_out, norm_xa_w, mem_norm_w, w_xq, w_xkv, w_xo, norm_ff_w, w_ff1, w_ff2, final_norm_w):
    bsz, seq, d = x.shape
    mem_len = mem.shape[1]
    depth = w_in.shape[0]
    n_tok = bsz * seq
    ret_qk = RET_HEADS * RET_DK
    ret_v = RET_HEADS * RET_DV
    inner = w_ssd_out.shape[1]
    conv_dim = conv_w.shape[2]
    n_heads = inner // SSD_HEAD_DIM
    ret_w = 2 * ret_qk + 2 * ret_v
    off_dt = ret_w + inner + conv_dim
    off_gate = off_dt + n_heads
    assert w_in.shape[2] == off_gate + 2 * d and depth == 1

    tm_main, tn_main = IN_PROJ_TILE
    assert conv_dim % tn_main == 0 and ret_w % tn_main == 0 and inner % tn_main == 0
    shift = inner // tn_main
    z_off, ret_off = 0, inner

    cos2, sin2 = _rope_tables(seq, RET_DK)
    row = lambda v: v.reshape(1, -1)
    pad_heads = lambda v: jnp.pad(v.reshape(1, -1), ((0, 0), (0, LANES - n_heads)))

    w_in_t = jnp.swapaxes(w_in, 1, 2)

    h = x.reshape(n_tok, d)
    for l in range(depth):
        wt_dt = jnp.pad(w_in_t[l, off_dt:off_gate], ((0, LANES - n_heads), (0, 0)))
        xn, dt_raw = _prenorm(h, row(norm_mix_w[l]), wt_dt, PRENORM_ROWS)
        proj, wt_gates, (w_ret_b, w_ssd_b, w_mix_b, w_xq_b, w_xo_b) = _in_proj_main(
            xn, w_in_t, l, 0, ret_w + inner, shift,
            (w_ret_out[l], w_ssd_out[l], w_mix_out[l], w_xq[l], w_xo[l]),
            off_gate, 2 * d, n_heads, tm_main, tn_main)
        p_xbc = _in_proj_conv(xn, w_in_t, l, ret_w + inner, conv_dim, conv_w[l], row(conv_b[l]),
                              seq, tm_main, tn_main)
        gates = _in_proj_gates(xn, wt_gates, *GATES_TILE)
        y_ret, y_ssd, w_ff1_b = _mixers(
            proj, ret_off, z_off, p_xbc, inner, cos2, sin2, row(ret_norm_w[l]), dt_raw,
            pad_heads(dt_bias[l]), pad_heads(a_log[l]), row(jnp.repeat(d_skip[l], SSD_HEAD_DIM)),
            row(ssd_norm_w[l]), w_ff1[l], bsz, seq)
        merged, w_ff2_b = _merge(y_ret, y_ssd, w_ret_b, w_ssd_b, gates, w_ff2[l], *MERGE_TILE)
        kv = _norm_matmul(mem.reshape(bsz * mem_len, d), row(mem_norm_w[l]), w_xkv[l],
                          BF16, bsz * mem_len, KV_COLS, "xa_kv")
        h = _mix_xattn(merged, h, kv, w_mix_b, row(norm_xa_w[l]), w_xq_b, w_xo_b,
                       bsz, seq, mem_len, XATTN_ROWS)
        h = _mlp(h, row(norm_ff_w[l]), w_ff1_b, w_ff2_b, row(final_norm_w), *MLP_TILE)
    return h.reshape(bsz, seq, d)
```

```python
import functools
import itertools
import math

import numpy as np
import jax
import jax.numpy as jnp
from jax import lax
from jax.experimental import pallas as pl
from jax.experimental.pallas import tpu as pltpu

F32 = jnp.float32
BF16 = jnp.bfloat16
EPS = 1e-6

RET_HEADS = 8
RET_DK = 128
RET_DV = 256
CHUNK = 128
ROPE_BASE = 10000.0
SSD_HEAD_DIM = 64
SSD_GROUPS = 8
SSD_STATE = 128
SSD_CONV = 4
CONV_HIST = 8
XA_HEADS = 4

V7X_VMEM_BYTES = 64 * 1024 * 1024
VMEM_LIMIT = V7X_VMEM_BYTES - 4 * 1024 * 1024
LANES = 128
ROW_SPLIT = 512
CONV_ROW_SPLIT = 256
LOG2_E = math.log2(math.e)

PRENORM_ROWS = 1024
IN_PROJ_TILE = (2048, 1024)
CAST_SLABS = 32
GATES_TILE = (2048, 2048)
MERGE_TILE = (1024, 512)
KV_COLS = 1024
XATTN_ROWS = 512
MLP_TILE = (1024, 1024)
MIX_STEP_CHUNKS = 2

_LOG_G = np.log(1.0 - np.exp2(-5.0 - np.arange(RET_HEADS, dtype=np.float64)))
_RET_CHUNK_DECAY = np.exp(CHUNK * _LOG_G)


def _params(*sem):
    return pltpu.CompilerParams(dimension_semantics=sem, vmem_limit_bytes=VMEM_LIMIT)


def _rms(x, w):
    ms = jnp.mean(x * x, axis=-1, keepdims=True)
    return x * lax.rsqrt(ms + EPS) * w


def _sigmoid(x):
    return 0.5 + 0.5 * jnp.tanh(0.5 * x)


def _silu(x):
    h = 0.5 * x
    return h + h * jnp.tanh(h)


def _dot(a, b):
    return jnp.dot(a, b, preferred_element_type=F32)


def _dot_nt(a, b):
    return lax.dot_general(a, b, (((1,), (1,)), ((), ())), preferred_element_type=F32)


def _cast_slab_specs(weights, n_steps, step_of, n_slabs=None):
    n_slabs = min(CAST_SLABS, n_steps) if n_slabs is None else n_slabs
    assert n_slabs <= n_steps and all(w.shape[0] % n_slabs == 0 for w in weights)
    slab = lambda *idx: (jnp.minimum(step_of(*idx), n_slabs - 1), 0)
    return [pl.BlockSpec((w.shape[0] // n_slabs, w.shape[1]), slab) for w in weights]


def _cast_slabs(src_refs, dst_refs):
    for src_ref, dst_ref in zip(src_refs, dst_refs):
        dst_ref[...] = src_ref[...].astype(dst_ref.dtype)


def _row_blocks(n_rows, step=ROW_SPLIT):
    step = min(step, n_rows)
    return [slice(r, r + step) for r in range(0, n_rows, step)]


def _prenorm_body(x_ref, nw_ref, wdt_ref, o_ref, dt_ref):
    xn = _rms(x_ref[...], nw_ref[...]).astype(o_ref.dtype)
    o_ref[...] = xn
    dt_ref[...] = _dot_nt(xn, wdt_ref[...].astype(BF16))


def _prenorm(x, nw, wt_dt, tm):
    m, k = x.shape
    n_dt = wt_dt.shape[0]
    return pl.pallas_call(
        _prenorm_body,
        grid=(m // tm,),
        in_specs=[pl.BlockSpec((tm, k), lambda i: (i, 0)), pl.BlockSpec((1, k), lambda i: (0, 0)),
                  pl.BlockSpec((n_dt, k), lambda i: (0, 0))],
        out_specs=[pl.BlockSpec((tm, k), lambda i: (i, 0)), pl.BlockSpec((tm, n_dt), lambda i: (i, 0))],
        out_shape=[jax.ShapeDtypeStruct((m, k), BF16), jax.ShapeDtypeStruct((m, n_dt), F32)],
        compiler_params=_params("parallel"),
        name="prenorm",
    )(x, nw, wt_dt)


def _matmul_nt_body(a_ref, wt_ref, g0_ref, g1_ref, *refs):
    n_cast = (len(refs) - 2) // 2
    o_ref, gate_w_ref = refs[n_cast], refs[n_cast + 1]
    wb = wt_ref[...].astype(BF16)
    for rows in _row_blocks(a_ref.shape[0]):
        o_ref[rows, :] = _dot_nt(a_ref[rows, :], wb).astype(o_ref.dtype)
    _cast_slabs(refs[:n_cast], refs[n_cast + 2:])
    slab = g0_ref.shape[0]
    gate_w_ref[0:slab, :] = g0_ref[...].astype(gate_w_ref.dtype)
    gate_w_ref[slab:2 * slab, :] = g1_ref[...].astype(gate_w_ref.dtype)


def _in_proj_main(a, w3, layer, row_off, n_cols, shift, to_bf16, gate_row0, n_gate_rows,
                  slab, tm, tn):
    m, k = a.shape
    nb = n_cols // tn
    off = row_off // tn
    n_steps = (m // tm) * nb
    assert n_cols % tn == 0 and m % tm == 0 and row_off % tn == 0
    cast_specs = _cast_slab_specs(to_bf16, n_steps, lambda i, j: i * nb + j)
    gate_blocks = n_gate_rows // (2 * slab)
    gate0 = gate_row0 // slab
    assert gate_row0 % slab == 0 and n_gate_rows % (2 * slab) == 0 and gate_blocks <= n_steps
    gate_blk = lambda i, j: jnp.minimum(i * nb + j, gate_blocks - 1)
    gate_src = lambda t: pl.BlockSpec(
        (None, slab, k), lambda i, j: (layer, gate0 + 2 * gate_blk(i, j) + t, 0))

    outs = pl.pallas_call(
        _matmul_nt_body,
        grid=(m // tm, nb),
        in_specs=[
            pl.BlockSpec((tm, k), lambda i, j: (i, 0)),
            pl.BlockSpec((None, tn, k), lambda i, j: (layer, j + off, 0)),
            gate_src(0), gate_src(1),
        ] + cast_specs,
        out_specs=[pl.BlockSpec((tm, tn), lambda i, j: (i, (j + shift) % nb)),
                   pl.BlockSpec((2 * slab, k), lambda i, j: (gate_blk(i, j), 0))] + cast_specs,
        out_shape=[jax.ShapeDtypeStruct((m, n_cols), BF16),
                   jax.ShapeDtypeStruct((n_gate_rows, k), BF16)]
        + [jax.ShapeDtypeStruct(w.shape, BF16) for w in to_bf16],
        compiler_params=_params("arbitrary", "arbitrary"),
        name="in_proj_main",
    )(a, w3, w3, w3, *to_bf16)
    return outs[0], outs[1], outs[2:]


def _in_proj_conv_body(a_ref, wt_ref, cw_ref, cb_ref, o_ref, carry_ref, *, tiles_per_seq):
    i = pl.program_id(0)
    j = pl.program_id(1)
    tn = wt_ref.shape[0]

    @pl.when(i % tiles_per_seq == 0)
    def _():
        carry_ref[j] = jnp.zeros((CONV_HIST, tn), F32)

    wb = wt_ref[...].astype(BF16)
    cwh = 0.5 * cw_ref[...]
    cbh = 0.5 * cb_ref[...]
    hist = carry_ref[j]
    for rows in _row_blocks(a_ref.shape[0], CONV_ROW_SPLIT):
        res = _dot_nt(a_ref[rows, :], wb)
        nt = res.shape[0] // CONV_HIST
        tiles = jnp.concatenate([hist, res], axis=0).reshape(nt + 1, CONV_HIST, tn)
        sub = lax.broadcasted_iota(jnp.int32, (nt, CONV_HIST, tn), 1)
        acc = cbh + cwh[SSD_CONV - 1:SSD_CONV, :] * res
        for k in range(SSD_CONV - 1):
            dist = SSD_CONV - 1 - k
            mixed = jnp.where(sub >= CONV_HIST - dist, tiles[:-1], tiles[1:])
            shifted = pltpu.roll(mixed, dist, 1).reshape(nt * CONV_HIST, tn)
            acc = acc + cwh[k:k + 1, :] * shifted
        o_ref[rows, :] = (acc + acc * jnp.tanh(acc)).astype(o_ref.dtype)
        hist = res[res.shape[0] - CONV_HIST:, :]
    carry_ref[j] = hist


def _in_proj_conv(a, w3, layer, row_off, n_cols, conv_w, conv_b, seq, tm, tn):
    m, k = a.shape
    nb = n_cols // tn
    off = row_off // tn
    assert n_cols % tn == 0 and m % tm == 0 and row_off % tn == 0 and seq % tm == 0
    assert SSD_CONV - 1 <= CONV_HIST
    body = functools.partial(_in_proj_conv_body, tiles_per_seq=seq // tm)
    return pl.pallas_call(
        body,
        grid=(m // tm, nb),
        in_specs=[
            pl.BlockSpec((tm, k), lambda i, j: (i, 0)),
            pl.BlockSpec((None, tn, k), lambda i, j: (layer, j + off, 0)),
            pl.BlockSpec((SSD_CONV, tn), lambda i, j: (0, j)),
            pl.BlockSpec((1, tn), lambda i, j: (0, j)),
        ],
        out_specs=pl.BlockSpec((tm, tn), lambda i, j: (i, j)),
        out_shape=jax.ShapeDtypeStruct((m, n_cols), BF16),
        scratch_shapes=[pltpu.VMEM((nb, CONV_HIST, tn), F32)],
        compiler_params=_params("arbitrary", "arbitrary"),
        name="in_proj_conv",
    )(a, w3, conv_w, conv_b)


def _in_proj_gates_body(a_ref, wt_ref, g_ref):
    for rows in _row_blocks(a_ref.shape[0]):
        g_ref[rows, :] = _dot_nt(a_ref[rows, :], wt_ref[...]).astype(g_ref.dtype)


def _in_proj_gates(a, wt, tm, tn):
    m, k = a.shape
    n = wt.shape[0]
    assert n % tn == 0 and m % tm == 0
    return pl.pallas_call(
        _in_proj_gates_body,
        grid=(m // tm, n // tn),
        in_specs=[
            pl.BlockSpec((tm, k), lambda i, j: (i, 0)),
            pl.BlockSpec((tn, k), lambda i, j: (j, 0)),
        ],
        out_specs=pl.BlockSpec((tm, tn), lambda i, j: (i, j)),
        out_shape=jax.ShapeDtypeStruct((m, n), BF16),
        compiler_params=_params("parallel", "arbitrary"),
        name="in_proj_gates",
    )(a, wt)


def _norm_matmul_body(x_ref, nw_ref, w_ref, o_ref, xn_ref):
    @pl.when(pl.program_id(1) == 0)
    def _():
        xn_ref[...] = _rms(x_ref[...], nw_ref[...]).astype(BF16)

    o_ref[...] = _dot(xn_ref[...], w_ref[...].astype(BF16)).astype(o_ref.dtype)


def _norm_matmul(x, nw, w, out_dtype, tm, tn, name):
    m, k = x.shape
    n = w.shape[1]
    return pl.pallas_call(
        _norm_matmul_body,
        grid=(m // tm, n // tn),
        in_specs=[
            pl.BlockSpec((tm, k), lambda i, j: (i, 0)),
            pl.BlockSpec((1, k), lambda i, j: (0, 0)),
            pl.BlockSpec((k, tn), lambda i, j: (0, j)),
        ],
        out_specs=pl.BlockSpec((tm, tn), lambda i, j: (i, j)),
        out_shape=jax.ShapeDtypeStruct((m, n), out_dtype),
        scratch_shapes=[pltpu.VMEM((tm, k), BF16)],
        compiler_params=_params("parallel", "arbitrary"),
        name=name,
    )(x, nw, w)


def _retention_tables():
    idx = np.arange(CHUNK, dtype=np.float64)
    rel = idx[:, None] - idx[None, :]
    lg = _LOG_G[:, None, None]
    scale = RET_DK ** -0.5
    decay = np.where(rel[None] >= 0, np.exp(np.maximum(rel, 0.0)[None] * lg), 0.0) * scale
    xi = np.exp((idx + 1.0)[None, :, None] * lg) * scale
    zeta = np.exp((CHUNK - 1.0 - idx)[None, :, None] * lg)
    wide = (RET_HEADS, CHUNK, RET_DK)
    return (decay.astype(np.float32), np.broadcast_to(xi, wide).astype(np.float32),
            np.broadcast_to(zeta, wide).astype(np.float32))


def _retention_body(q_ref, k_ref, v_ref, g_ref, cos_ref, sin_ref, dec_ref, xi_ref, zeta_ref,
                    nw_ref, o_ref, state_ref):
    for cc, h in itertools.product(range(q_ref.shape[0] // CHUNK), range(RET_HEADS)):
        rows = slice(cc * CHUNK, (cc + 1) * CHUNK)
        cos = cos_ref[rows, :]
        sin = sin_ref[rows, :]
        q = q_ref[rows, h * RET_DK:(h + 1) * RET_DK].astype(F32)
        k = k_ref[rows, h * RET_DK:(h + 1) * RET_DK].astype(F32)
        v = v_ref[rows, h * RET_DV:(h + 1) * RET_DV]
        qr = q * cos + pltpu.roll(q, RET_DK // 2, 1) * sin
        kr = k * cos + pltpu.roll(k, RET_DK // 2, 1) * sin
        s = _dot_nt(qr.astype(BF16), kr.astype(BF16)) * dec_ref[h]
        st = state_ref[h]
        lhs = jnp.concatenate([s.astype(BF16), (qr * xi_ref[h]).astype(BF16)], axis=1)
        rhs = jnp.concatenate([v, st.astype(BF16)], axis=0)
        y = _dot(lhs, rhs)
        kz = (kr * zeta_ref[h]).astype(BF16)
        kv = lax.dot_general(kz, v, (((0,), (0,)), ((), ())), preferred_element_type=F32)
        state_ref[h] = float(_RET_CHUNK_DECAY[h]) * st + kv
        mu = jnp.mean(y, axis=-1, keepdims=True)
        yc = y - mu
        var = jnp.mean(yc * yc, axis=-1, keepdims=True)
        yn = yc * lax.rsqrt(var + EPS)
        hsl = slice(h * RET_DV, (h + 1) * RET_DV)
        o_ref[rows, hsl] = (yn * nw_ref[:, hsl] * _silu(g_ref[rows, hsl].astype(F32))).astype(BF16)


def _ssd_body(z_ref, xbc_ref, dt_ref, dtb_ref, alog_ref, dsk_ref, nw_ref, wsrc_ref,
              o_ref, wdst_ref, prev_ref, *, inner, heads_per_group):
    _cast_slabs([wsrc_ref], [wdst_ref])

    c = CHUNK
    n_sub = z_ref.shape[0] // c
    gw = heads_per_group * SSD_HEAD_DIM
    row = lax.broadcasted_iota(jnp.int32, (c, c), 0)
    col = lax.broadcasted_iota(jnp.int32, (c, c), 1)
    tri = row >= col
    tri_b = tri.astype(BF16)
    tri3 = jnp.concatenate([tri_b, tri_b, tri_b], axis=1)
    lane_lo = lax.broadcasted_iota(jnp.int32, (c, LANES), 1) < SSD_HEAD_DIM
    neg_inf = jnp.float32(-jnp.inf)
    a = -jnp.exp(alog_ref[...])

    for cc in range(n_sub):
        rows = slice(cc * c, (cc + 1) * c)
        dtr = dt_ref[rows, :] + dtb_ref[...]
        dt = jnp.maximum(dtr, 0.0) + jnp.log1p(jnp.exp(-jnp.abs(dtr)))
        da = dt * a
        da_hi = da.astype(BF16)
        da_r1 = da - da_hi.astype(F32)
        da_mid = da_r1.astype(BF16)
        da_lo = (da_r1 - da_mid.astype(F32)).astype(BF16)
        a_cs = _dot(tri3, jnp.concatenate([da_hi, da_mid, da_lo], axis=0))
        a_cs = a_cs * LOG2_E
        a_last = a_cs[c - 1:c, :]
        w1 = jnp.exp2(a_last - a_cs) * dt
        src_t = (a_cs - jnp.log2(dt)).T
        w1_t = w1.T

        for g in range(SSD_GROUPS):
            b_gb = xbc_ref[rows, inner + g * SSD_STATE: inner + (g + 1) * SSD_STATE]
            c_gb = xbc_ref[rows, inner + (SSD_GROUPS + g) * SSD_STATE:
                           inner + (SSD_GROUPS + g + 1) * SSD_STATE]
            cb = _dot_nt(c_gb, b_gb).astype(BF16)
            b_gt = b_gb.astype(F32).T.astype(BF16)
            y_pairs = []
            for pr in range(heads_per_group // 2):
                h0 = g * heads_per_group + 2 * pr
                psl = slice(h0 * SSD_HEAD_DIM, (h0 + 2) * SSD_HEAD_DIM)
                xs_b = xbc_ref[rows, psl]
                prev = prev_ref[:, psl]
                rhs = jnp.concatenate([xs_b, prev.astype(BF16)], axis=0)
                lhs, bws, cds = [], [], []
                for h in (h0, h0 + 1):
                    colb = jnp.broadcast_to(a_cs[:, h:h + 1], (c, c))
                    rowb = jnp.broadcast_to(src_t[h:h + 1, :], (c, c))
                    m = cb * jnp.exp2(jnp.where(tri, colb - rowb, neg_inf)).astype(BF16)
                    e = jnp.exp2(colb)
                    lhs.append(jnp.concatenate([m, e.astype(BF16) * c_gb], axis=1))
                    bws.append(b_gt * jnp.broadcast_to(w1_t[h:h + 1, :], (c, c)).astype(BF16))
                    cds.append(e[c - 1:c, :])
                ys = _dot(jnp.concatenate(lhs, axis=0), rhs)
                sts = _dot(jnp.concatenate(bws, axis=0), xs_b)
                y_pairs.append(jnp.where(lane_lo, ys[0:c, :], ys[c:2 * c, :]))
                cd = jnp.where(lane_lo[0:1, :], cds[0], cds[1])
                prev_ref[:, psl] = cd * prev + jnp.where(lane_lo, sts[0:c, :], sts[c:2 * c, :])
            gsl = slice(g * gw, (g + 1) * gw)
            y = jnp.concatenate(y_pairs, axis=1) + dsk_ref[:, gsl] * xbc_ref[rows, gsl].astype(F32)
            gy = y * _silu(z_ref[rows, gsl].astype(F32))
            ms = jnp.mean(gy * gy, axis=-1, keepdims=True)
            o_ref[rows, gsl] = (gy * lax.rsqrt(ms + EPS) * nw_ref[:, gsl]).astype(BF16)


def _mixers_body(*refs, n_ret_in, n_ssd_in, inner, heads_per_group):
    ret_in = refs[:n_ret_in]
    ssd_in = refs[n_ret_in:n_ret_in + n_ssd_in]
    y_ret_ref, y_ssd_ref, wdst_ref, state_ref, prev_ref = refs[n_ret_in + n_ssd_in:]

    @pl.when(pl.program_id(1) == 0)
    def _():
        state_ref[...] = jnp.zeros_like(state_ref)
        prev_ref[...] = jnp.zeros_like(prev_ref)

    _retention_body(*ret_in, y_ret_ref, state_ref)
    _ssd_body(*ssd_in, y_ssd_ref, wdst_ref, prev_ref, inner=inner, heads_per_group=heads_per_group)


def _mixers(proj, ret_off, z_off, p_xbc, inner, cos2, sin2, ret_nw, dt_raw, dtb, alog, dskip,
            ssd_nw, w_to_bf16, bsz, seq):
    rows = MIX_STEP_CHUNKS * CHUNK
    nc = seq // rows
    qk = RET_HEADS * RET_DK
    vd = RET_HEADS * RET_DV
    conv_dim = p_xbc.shape[1]
    n_heads = inner // SSD_HEAD_DIM
    heads_per_group = n_heads // SSD_GROUPS
    assert seq % rows == 0 and vd == 2 * qk and ret_off % vd == 0 and z_off % inner == 0
    assert conv_dim == inner + 2 * SSD_GROUPS * SSD_STATE and SSD_STATE == CHUNK
    assert n_heads <= LANES and heads_per_group % 2 == 0 and 2 * SSD_HEAD_DIM == LANES
    oq, ov, zb = ret_off // qk, ret_off // vd, z_off // inner
    decay, xi, zeta = _retention_tables()
    tok = lambda blk: (lambda b, c: (b * nc + c, blk))
    fixed = lambda b, c: (0, 0)
    table = lambda n: pl.BlockSpec((RET_HEADS, CHUNK, n), lambda b, c: (0, 0, 0))
    cast_spec = _cast_slab_specs([w_to_bf16], bsz * nc, lambda b, c: b * nc + c)
    ret_specs = [
        pl.BlockSpec((rows, qk), tok(oq)),
        pl.BlockSpec((rows, qk), tok(oq + 1)),
        pl.BlockSpec((rows, vd), tok(ov + 1)),
        pl.BlockSpec((rows, vd), tok(ov + 2)),
        pl.BlockSpec((rows, RET_DK), lambda b, c: (c, 0)),
        pl.BlockSpec((rows, RET_DK), lambda b, c: (c, 0)),
        table(CHUNK), table(RET_DK), table(RET_DK),
        pl.BlockSpec((1, vd), fixed),
    ]
    ssd_specs = [
        pl.BlockSpec((rows, inner), tok(zb)),
        pl.BlockSpec((rows, conv_dim), tok(0)),
        pl.BlockSpec((rows, LANES), tok(0)),
        pl.BlockSpec((1, LANES), fixed),
        pl.BlockSpec((1, LANES), fixed),
        pl.BlockSpec((1, inner), fixed),
        pl.BlockSpec((1, inner), fixed),
    ] + cast_spec
    body = functools.partial(_mixers_body, n_ret_in=len(ret_specs), n_ssd_in=len(ssd_specs),
                             inner=inner, heads_per_group=heads_per_group)
    return pl.pallas_call(
        body,
        grid=(bsz, nc),
        in_specs=ret_specs + ssd_specs,
        out_specs=[pl.BlockSpec((rows, vd), tok(0)), pl.BlockSpec((rows, inner), tok(0))] + cast_spec,
        out_shape=[jax.ShapeDtypeStruct((bsz * seq, vd), BF16),
                   jax.ShapeDtypeStruct((bsz * seq, inner), BF16),
                   jax.ShapeDtypeStruct(w_to_bf16.shape, BF16)],
        scratch_shapes=[pltpu.VMEM((RET_HEADS, RET_DK, RET_DV), F32),
                        pltpu.VMEM((SSD_STATE, inner), F32)],
        compiler_params=_params("arbitrary", "arbitrary"),
        name="mixers",
    )(proj, proj, proj, proj, cos2, sin2, decay, xi, zeta, ret_nw,
      proj, p_xbc, dt_raw, dtb, alog, dskip, ssd_nw, w_to_bf16)


def _merge_body(yr_ref, ys_ref, wr_ref, ws_ref, ga_ref, gb_ref, wsrc_ref, o_ref, wdst_ref):
    _cast_slabs([wsrc_ref], [wdst_ref])
    for rows in _row_blocks(yr_ref.shape[0]):
        br = _dot(yr_ref[rows, :], wr_ref[...])
        bs = _dot(ys_ref[rows, :], ws_ref[...])
        ga = _sigmoid(ga_ref[rows, :].astype(F32))
        gb = _sigmoid(gb_ref[rows, :].astype(F32))
        o_ref[rows, :] = (ga * br + gb * bs).astype(o_ref.dtype)


def _merge(y_ret, y_ssd, w_ret, w_ssd, gates, w_to_bf16, tm, tn):
    m, kr = y_ret.shape
    ks = y_ssd.shape[1]
    n = w_ret.shape[1]
    nb = n // tn
    cast_spec = _cast_slab_specs([w_to_bf16], (m // tm) * nb, lambda i, j: i * nb + j)
    return pl.pallas_call(
        _merge_body,
        grid=(m // tm, nb),
        in_specs=[
            pl.BlockSpec((tm, kr), lambda i, j: (i, 0)),
            pl.BlockSpec((tm, ks), lambda i, j: (i, 0)),
            pl.BlockSpec((kr, tn), lambda i, j: (0, j)),
            pl.BlockSpec((ks, tn), lambda i, j: (0, j)),
            pl.BlockSpec((tm, tn), lambda i, j: (i, j)),
            pl.BlockSpec((tm, tn), lambda i, j: (i, j + nb)),
        ] + cast_spec,
        out_specs=[pl.BlockSpec((tm, tn), lambda i, j: (i, j))] + cast_spec,
        out_shape=[jax.ShapeDtypeStruct((m, n), BF16), jax.ShapeDtypeStruct(w_to_bf16.shape, BF16)],
        compiler_params=_params("arbitrary", "arbitrary"),
        name="merge",
    )(y_ret, y_ssd, w_ret, w_ssd, gates, gates, w_to_bf16)


def _mix_xattn_body(m_ref, x_ref, k_ref, v_ref, wmix_ref, nw_ref, wq_ref, wo_ref, o_ref,
                    xn_ref, q_ref):
    d = x_ref.shape[1]
    hd = d // XA_HEADS
    scale = hd ** -0.5
    o_ref[...] = x_ref[...] + _dot(m_ref[...], wmix_ref[...])
    xn_ref[...] = _rms(o_ref[...], nw_ref[...]).astype(BF16)
    q_ref[...] = _dot(xn_ref[...], wq_ref[...]).astype(BF16)
    for h in range(XA_HEADS):
        sl = slice(h * hd, (h + 1) * hd)
        s = _dot_nt(q_ref[:, sl], k_ref[:, sl]) * scale
        s = s - jnp.max(s, axis=-1, keepdims=True)
        p = jnp.exp(s)
        p = p / jnp.sum(p, axis=-1, keepdims=True)
        xn_ref[:, sl] = _dot(p.astype(BF16), v_ref[:, sl]).astype(BF16)
    o_ref[...] += _dot(xn_ref[...], wo_ref[...])


def _mix_xattn(merged, x, kv, w_mix, nw, w_q, w_o, bsz, seq, mem_len, tq):
    d = x.shape[1]
    nq = seq // tq
    tile = lambda b, i: (b * nq + i, 0)
    fixed = lambda b, i: (0, 0)
    resident = pl.Buffered(1)
    return pl.pallas_call(
        _mix_xattn_body,
        grid=(bsz, nq),
        in_specs=[
            pl.BlockSpec((tq, d), tile),
            pl.BlockSpec((tq, d), tile),
            pl.BlockSpec((mem_len, d), lambda b, i: (b, 0)),
            pl.BlockSpec((mem_len, d), lambda b, i: (b, 1)),
            pl.BlockSpec((d, d), fixed, pipeline_mode=resident),
            pl.BlockSpec((1, d), fixed),
            pl.BlockSpec((d, d), fixed, pipeline_mode=resident),
            pl.BlockSpec((d, d), fixed, pipeline_mode=resident),
        ],
        out_specs=pl.BlockSpec((tq, d), tile),
        out_shape=jax.ShapeDtypeStruct((bsz * seq, d), F32),
        scratch_shapes=[pltpu.VMEM((tq, d), BF16), pltpu.VMEM((tq, d), BF16)],
        compiler_params=_params("parallel", "arbitrary"),
        name="mix_xattn",
    )(merged, x, kv, kv, w_mix, nw, w_q, w_o)


def _mlp_body(h_ref, nw_ref, w1_ref, w2_ref, fw_ref, o_ref, xn_ref):
    f = pl.program_id(1)
    blocks = _row_blocks(h_ref.shape[0])

    @pl.when(f == 0)
    def _():
        for rows in blocks:
            hh = h_ref[rows, :]
            xn_ref[rows, :] = _rms(hh, nw_ref[...]).astype(BF16)
            o_ref[rows, :] = hh

    for rows in blocks:
        u = _dot(xn_ref[rows, :], w1_ref[...])
        u = jnp.square(jnp.maximum(u, 0.0)).astype(BF16)
        o_ref[rows, :] += _dot(u, w2_ref[...])

    @pl.when(f == pl.num_programs(1) - 1)
    def _():
        for rows in blocks:
            o_ref[rows, :] = _rms(o_ref[rows, :], fw_ref[...])


def _mlp(h, nw, w1, w2, fw, tm, tf):
    m, d = h.shape
    dff = w1.shape[1]
    return pl.pallas_call(
        _mlp_body,
        grid=(m // tm, dff // tf),
        in_specs=[
            pl.BlockSpec((tm, d), lambda i, f: (i, 0)),
            pl.BlockSpec((1, d), lambda i, f: (0, 0)),
            pl.BlockSpec((d, tf), lambda i, f: (0, f)),
            pl.BlockSpec((tf, d), lambda i, f: (f, 0)),
            pl.BlockSpec((1, d), lambda i, f: (0, 0)),
        ],
        out_specs=pl.BlockSpec((tm, d), lambda i, f: (i, 0)),
        out_shape=jax.ShapeDtypeStruct((m, d), F32),
        scratch_shapes=[pltpu.VMEM((tm, d), BF16)],
        compiler_params=_params("parallel", "arbitrary"),
        name="mlp",
    )(h, nw, w1, w2, fw)


def _rope_tables(seq, dk):
    half = dk // 2
    inv = np.exp(-math.log(ROPE_BASE) * np.arange(half, dtype=np.float64) / half)
    ang = np.arange(seq, dtype=np.float64)[:, None] * inv[None, :]
    cos, sin = np.cos(ang).astype(np.float32), np.sin(ang).astype(np.float32)
    return np.concatenate([cos, cos], axis=1), np.concatenate([-sin, sin], axis=1)


def kernel(x, mem, norm_mix_w, w_in, conv_w, conv_b, dt_bias, a_log, d_skip, ret_norm_w, ssd_norm_w, w_ret_out, w_ssd_out, w_mix_out, norm_xa_w, mem_norm_w, w_xq, w_xkv, w_xo, norm_ff_w, w_ff1, w_ff2, final_norm_w):
    bsz, seq, d = x.shape
    mem_len = mem.shape[1]
    depth = w_in.shape[0]
    n_tok = bsz * seq
    ret_qk = RET_HEADS * RET_DK
    ret_v = RET_HEADS * RET_DV
    inner = w_ssd_out.shape[1]
    conv_dim = conv_w.shape[2]
    n_heads = inner // SSD_HEAD_DIM
    ret_w = 2 * ret_qk + 2 * ret_v
    off_dt = ret_w + inner + conv_dim
    off_gate = off_dt + n_heads
    assert w_in.shape[2] == off_gate + 2 * d and depth == 1

    tm_main, tn_main = IN_PROJ_TILE
    assert conv_dim % tn_main == 0 and ret_w % tn_main == 0 and inner % tn_main == 0
    shift = inner // tn_main
    z_off, ret_off = 0, inner

    cos2, sin2 = _rope_tables(seq, RET_DK)
    row = lambda v: v.reshape(1, -1)
    pad_heads = lambda v: jnp.pad(v.reshape(1, -1), ((0, 0), (0, LANES - n_heads)))

    w_in_t = jnp.swapaxes(w_in, 1, 2)

    h = x.reshape(n_tok, d)
    for l in range(depth):
        wt_dt = jnp.pad(w_in_t[l, off_dt:off_gate], ((0, LANES - n_heads), (0, 0)))
        xn, dt_raw = _prenorm(h, row(norm_mix_w[l]), wt_dt, PRENORM_ROWS)
        proj, wt_gates, (w_ret_b, w_ssd_b, w_mix_b, w_xq_b, w_xo_b) = _in_proj_main(
            xn, w_in_t, l, 0, ret_w + inner, shift,
            (w_ret_out[l], w_ssd_out[l], w_mix_out[l], w_xq[l], w_xo[l]),
            off_gate, 2 * d, n_heads, tm_main, tn_main)
        p_xbc = _in_proj_conv(xn, w_in_t, l, ret_w + inner, conv_dim, conv_w[l], row(conv_b[l]),
                              seq, tm_main, tn_main)
        gates = _in_proj_gates(xn, wt_gates, *GATES_TILE)
        y_ret, y_ssd, w_ff1_b = _mixers(
            proj, ret_off, z_off, p_xbc, inner, cos2, sin2, row(ret_norm_w[l]), dt_raw,
            pad_heads(dt_bias[l]), pad_heads(a_log[l]), row(jnp.repeat(d_skip[l], SSD_HEAD_DIM)),
            row(ssd_norm_w[l]), w_ff1[l], bsz, seq)
        merged, w_ff2_b = _merge(y_ret, y_ssd, w_ret_b, w_ssd_b, gates, w_ff2[l], *MERGE_TILE)
        kv = _norm_matmul(mem.reshape(bsz * mem_len, d), row(mem_norm_w[l]), w_xkv[l],
                          BF16, bsz * mem_len, KV_COLS, "xa_kv")
        h = _mix_xattn(merged, h, kv, w_mix_b, row(norm_xa_w[l]), w_xq_b, w_xo_b,
                       bsz, seq, mem_len, XATTN_ROWS)
        h = _mlp(h, row(norm_ff_w[l]), w_ff1_b, w_ff2_b, row(final_norm_w), *MLP_TILE)
    return h.reshape(bsz, seq, d)
```
